```python
import jax, jax.numpy as jnp
from jax import lax
import numpy as np

D_MODEL = 2048
BATCH = 2
SEQ = 8192
DEPTH = 4

GRID_W = 64
CTX_LEN = 256
N_EVEN = (DEPTH + 1) // 2
N_ODD = DEPTH // 2
NORM_EPS = 1e-6
ROPE_BASE = 10000.0
MIX_WIDTH = D_MODEL

A_HEAD_DIM = 128
A_HEADS = (MIX_WIDTH // 2) // A_HEAD_DIM
A_KV_HEADS = A_HEADS // 4
A_WIDTH = A_HEADS * A_HEAD_DIM
A_KV_WIDTH = A_KV_HEADS * A_HEAD_DIM
A_WINDOW = 128
A_BLOCK = 128
B_WIDTH = MIX_WIDTH - A_WIDTH
B_CONV = 3
C_HEADS = 4
C_V_WIDTH = MIX_WIDTH // 2
C_V_DIM = C_V_WIDTH // C_HEADS
C_QK_DIM = C_V_DIM // 2
C_QK_WIDTH = C_HEADS * C_QK_DIM
C_CHUNK = 128
D_WIDTH = MIX_WIDTH - C_V_WIDTH
D_CONV = 31
FFN_DIM = 11 * D_MODEL // 4
N_EXPERTS = 8
TOP_K = 2
EXPERT_DIM = 11 * D_MODEL // 4

EVEN_SPLITS = (A_WIDTH, A_KV_WIDTH, A_KV_WIDTH, B_WIDTH, B_WIDTH, B_WIDTH)
ODD_SPLITS = (C_QK_WIDTH, C_QK_WIDTH, C_V_WIDTH, 4 * C_HEADS, C_V_WIDTH, 2 * D_WIDTH)

kernel_name = 'hybrid_swa_shortconv_mlstm_conformer_moe_dit'


def _split(z, sizes):
    idx = [int(i) for i in np.cumsum(sizes)[:-1]]
    return jnp.split(z, idx, axis=-1)


def rmsnorm(x, w):
    xf = x.astype(jnp.float32)
    y = xf * lax.rsqrt(jnp.mean(xf * xf, axis=-1, keepdims=True) + NORM_EPS)
    return (y * w.astype(jnp.float32)).astype(x.dtype)


def layernorm(x, w, b):
    xf = x.astype(jnp.float32)
    mu = jnp.mean(xf, axis=-1, keepdims=True)
    xc = xf - mu
    var = jnp.mean(xc * xc, axis=-1, keepdims=True)
    return (xc * lax.rsqrt(var + NORM_EPS) * w.astype(jnp.float32) + b.astype(jnp.float32)).astype(x.dtype)


def modulate(h, shift, scale):
    return h * (1 + scale) + shift


def swiglu(h, w_gate, w_up, w_down):
    return (jax.nn.silu(h @ w_gate) * (h @ w_up)) @ w_down


def depthwise_conv(x, w):
    k_w, ch = w.shape
    p = k_w // 2
    return lax.conv_general_dilated(x, w[:, None, :].astype(x.dtype), (1,), [(p, p)],
                                    dimension_numbers=('NWC', 'WIO', 'NWC'), feature_group_count=ch)


def axial_rope_tables(rows):
    row = jnp.repeat(jnp.arange(rows, dtype=jnp.float32), GRID_W)
    col = jnp.tile(jnp.arange(GRID_W, dtype=jnp.float32), rows)
    n_freq = A_HEAD_DIM // 4
    inv_freq = ROPE_BASE ** (-jnp.arange(n_freq, dtype=jnp.float32) / n_freq)
    ang = jnp.concatenate([row[:, None] * inv_freq, col[:, None] * inv_freq], axis=-1)
    return jnp.cos(ang), jnp.sin(ang)


def apply_rope(x, cos, sin):
    x1, x2 = jnp.split(x, 2, axis=-1)
    cs = cos[None, :, None, :]
    sn = sin[None, :, None, :]
    return jnp.concatenate([x1 * cs - x2 * sn, x1 * sn + x2 * cs], axis=-1)


def window_attention(q, k, v, k_ctx, v_ctx, sink):
    b_, s_len, hq, dh = q.shape
    hkv = k.shape[2]
    g = hq // hkv
    lc = k_ctx.shape[1]
    nb = s_len // A_BLOCK
    scale = dh ** -0.5
    qb = q.reshape(b_, nb, A_BLOCK, hkv, g, dh)
    pad = ((0, 0), (A_BLOCK, A_BLOCK), (0, 0), (0, 0))
    kp = jnp.pad(k, pad).reshape(b_, nb + 2, A_BLOCK, hkv, dh)
    vp = jnp.pad(v, pad).reshape(b_, nb + 2, A_BLOCK, hkv, dh)
    kb = jnp.concatenate([kp[:, :-2], kp[:, 1:-1], kp[:, 2:]], axis=2)
    vb = jnp.concatenate([vp[:, :-2], vp[:, 1:-1], vp[:, 2:]], axis=2)
    s_loc = jnp.einsum('bnqhgd,bnkhd->bnhgqk', qb, kb).astype(jnp.float32) * scale
    band_k = jnp.arange(3 * A_BLOCK)
    band = jnp.abs((jnp.arange(A_BLOCK) + A_BLOCK)[:, None] - band_k[None, :]) <= A_WINDOW
    k_abs = (jnp.arange(nb)[:, None] - 1) * A_BLOCK + band_k[None, :]
    valid = band[None] & ((k_abs >= 0) & (k_abs < s_len))[:, None, :]
    s_loc = jnp.where(valid[None, :, None, None], s_loc, -jnp.inf)
    s_ctx = jnp.einsum('bnqhgd,bchd->bnhgqc', qb, k_ctx).astype(jnp.float32) * scale
    s_sink = jnp.broadcast_to(sink.astype(jnp.float32).reshape(hkv, g)[None, None, :, :, None, None],
                              s_loc.shape[:-1] + (1,))
    p = jax.nn.softmax(jnp.concatenate([s_loc, s_ctx, s_sink], axis=-1), axis=-1).astype(v.dtype)
    o = (jnp.einsum('bnhgqk,bnkhd->bnqhgd', p[..., :3 * A_BLOCK], vb)
         + jnp.einsum('bnhgqc,bchd->bnqhgd', p[..., 3 * A_BLOCK:3 * A_BLOCK + lc], v_ctx))
    return o.reshape(b_, s_len, hq * dh)


def context_attention(q, k, v, sink):
    b_, lc, hq, dh = q.shape
    hkv = k.shape[2]
    g = hq // hkv
    qg = q.reshape(b_, lc, hkv, g, dh)
    s = jnp.einsum('bqhgd,bkhd->bhgqk', qg, k).astype(jnp.float32) * (dh ** -0.5)
    s_sink = jnp.broadcast_to(sink.astype(jnp.float32).reshape(hkv, g)[None, :, :, None, None], s.shape[:-1] + (1,))
    p = jax.nn.softmax(jnp.concatenate([s, s_sink], axis=-1), axis=-1)[..., :lc].astype(v.dtype)
    return jnp.einsum('bhgqk,bkhd->bqhgd', p, v).reshape(b_, lc, hq * dh)


def even_mixer(a_lat, a_ctx, w_in, sink, conv_w, w_out, cos, sin, with_ctx_out):
    b_, s_len, _ = a_lat.shape
    lc = a_ctx.shape[1]
    q, k, v, bg, cg, xb = _split(a_lat @ w_in, EVEN_SPLITS)
    if with_ctx_out:
        q_c, k_c, v_c, bg_c, cg_c, xb_c = _split(a_ctx @ w_in, EVEN_SPLITS)
    else:
        k_c, v_c = jnp.split(a_ctx @ w_in[:, A_WIDTH:A_WIDTH + 2 * A_KV_WIDTH], 2, axis=-1)
    k_c = k_c.reshape(b_, lc, A_KV_HEADS, A_HEAD_DIM)
    v_c = v_c.reshape(b_, lc, A_KV_HEADS, A_HEAD_DIM)
    q = apply_rope(q.reshape(b_, s_len, A_HEADS, A_HEAD_DIM), cos, sin)
    k = apply_rope(k.reshape(b_, s_len, A_KV_HEADS, A_HEAD_DIM), cos, sin)
    v = v.reshape(b_, s_len, A_KV_HEADS, A_HEAD_DIM)
    o_a = window_attention(q, k, v, k_c, v_c, sink)
    o_b = bg * depthwise_conv(cg * xb, conv_w)
    y_lat = jnp.concatenate([o_a, o_b], axis=-1) @ w_out
    y_ctx = None
    if with_ctx_out:
        o_ac = context_attention(q_c.reshape(b_, lc, A_HEADS, A_HEAD_DIM), k_c, v_c, sink)
        o_bc = bg_c * depthwise_conv(cg_c * xb_c, conv_w)
        y_ctx = jnp.concatenate([o_ac, o_bc], axis=-1) @ w_out
    return y_lat, y_ctx


def heads(t, n_heads):
    b_, s_len, w = t.shape
    return t.reshape(b_, s_len, n_heads, w // n_heads).transpose(0, 2, 1, 3)


def _flip(a):
    return jnp.flip(a, axis=2)


def mlstm_chunkwise(q, k, v, ig, lf, state):
    b_, h_, s_len, dk = q.shape
    nc = s_len // C_CHUNK
    ln = C_CHUNK

    def chunks(a):
        return jnp.moveaxis(a.reshape(a.shape[:2] + (nc, ln) + a.shape[3:]), 2, 0)

    tril = jnp.tril(jnp.ones((ln, ln), dtype=bool))

    def step(carry, inp):
        c_m, n_v, m_s = carry
        qc, kc, vc, ic, fc = inp
        bcum = jnp.cumsum(fc, axis=-1)
        dlog = jnp.where(tril, bcum[..., :, None] - bcum[..., None, :] + ic[..., None, :], -jnp.inf)
        inter = bcum + m_s[..., None]
        m_t = jnp.maximum(inter, jnp.max(dlog, axis=-1))
        w_inter = jnp.exp(inter - m_t)
        s = jnp.einsum('bhtd,bhsd->bhts', qc, kc) * jnp.exp(dlog - m_t[..., None])
        num = w_inter[..., None] * jnp.einsum('bhtd,bhvd->bhtv', qc, c_m) + jnp.einsum('bhts,bhsv->bhtv', s, vc)
        den = w_inter * jnp.einsum('bhtd,bhd->bht', qc, n_v) + jnp.sum(s, axis=-1)
        h = num / jnp.maximum(jnp.abs(den), jnp.exp(-m_t))[..., None]
        gl = bcum[..., -1:] - bcum + ic
        m_new = jnp.maximum(bcum[..., -1] + m_s, jnp.max(gl, axis=-1))
        wg = jnp.exp(gl - m_new[..., None])
        decay = jnp.exp(bcum[..., -1] + m_s - m_new)
        c_new = decay[..., None, None] * c_m + jnp.einsum('bhs,bhsv,bhsd->bhvd', wg, vc, kc)
        n_new = decay[..., None] * n_v + jnp.einsum('bhs,bhsd->bhd', wg, kc)
        return (c_new, n_new, m_new), h

    final, hs = lax.scan(step, state, (chunks(q), chunks(k), chunks(v), chunks(ig), chunks(lf)))
    return jnp.moveaxis(hs, 0, 2).reshape(b_, h_, s_len, v.shape[-1]), final


def _mlstm_inputs(q, k, v, gates, gate_b):
    qh = heads(q, C_HEADS).astype(jnp.float32) * (C_QK_DIM ** -0.5)
    kh = heads(k, C_HEADS).astype(jnp.float32)
    vh = heads(v, C_HEADS).astype(jnp.float32)
    b_, s_len, _ = gates.shape
    g = (gates.astype(jnp.float32) + gate_b.astype(jnp.float32)).reshape(b_, s_len, 4, C_HEADS).transpose(2, 0, 3, 1)
    fwd = (g[0], jax.nn.log_sigmoid(g[1]))
    bwd = (g[2], jax.nn.log_sigmoid(g[3]))
    return qh, kh, vh, fwd, bwd


def mlstm_bidirectional(lat_parts, ctx_parts, gate_b, with_ctx_out):
    ql, kl, vl, gfl, gbl = _mlstm_inputs(*lat_parts, gate_b)
    qc, kc, vc, gfc, gbc = _mlstm_inputs(*ctx_parts, gate_b)
    b_ = ql.shape[0]
    init = (jnp.zeros((b_, C_HEADS, C_V_DIM, C_QK_DIM), jnp.float32),
            jnp.zeros((b_, C_HEADS, C_QK_DIM), jnp.float32),
            jnp.zeros((b_, C_HEADS), jnp.float32))
    h_cf, st_f = mlstm_chunkwise(qc, kc, vc, gfc[0], gfc[1], init)
    h_lf, _ = mlstm_chunkwise(ql, kl, vl, gfl[0], gfl[1], st_f)
    h_cb, st_b = mlstm_chunkwise(_flip(qc), _flip(kc), _flip(vc), _flip(gbc[0]), _flip(gbc[1]), init)
    h_lb, _ = mlstm_chunkwise(_flip(ql), _flip(kl), _flip(vl), _flip(gbl[0]), _flip(gbl[1]), st_b)
    h_lat = h_lf + _flip(h_lb)
    h_ctx = h_cf + _flip(h_cb) if with_ctx_out else None
    return h_lat, h_ctx


def mlstm_head_out(h, mnorm_w, o_pre):
    mu = jnp.mean(h, axis=-1, keepdims=True)
    hc = h - mu
    hn = hc * lax.rsqrt(jnp.mean(hc * hc, axis=-1, keepdims=True) + NORM_EPS)
    b_, h_, s_len, dv = h.shape
    hn = hn.transpose(0, 2, 1, 3).reshape(b_, s_len, h_ * dv) * mnorm_w.astype(jnp.float32)
    return (hn * jax.nn.sigmoid(o_pre.astype(jnp.float32))).astype(o_pre.dtype)


def conformer_conv(glu_in, dw_w, dw_b, ln_w, ln_b):
    a, g = jnp.split(glu_in, 2, axis=-1)
    u = depthwise_conv(a * jax.nn.sigmoid(g), dw_w) + dw_b
    return jax.nn.silu(layernorm(u, ln_w, ln_b))


def odd_mixer(a_lat, a_ctx, w_in, gate_b, mnorm_w, dw_w, dw_b, ln_w, ln_b, w_out, with_ctx_out):
    q, k, v, gates, o_pre, glu_in = _split(a_lat @ w_in, ODD_SPLITS)
    if with_ctx_out:
        q_c, k_c, v_c, g_c, o_c, glu_c = _split(a_ctx @ w_in, ODD_SPLITS)
    else:
        n_rec = int(sum(ODD_SPLITS[:4]))
        q_c, k_c, v_c, g_c = _split(a_ctx @ w_in[:, :n_rec], ODD_SPLITS[:4])
    h_lat, h_ctx = mlstm_bidirectional((q, k, v, gates), (q_c, k_c, v_c, g_c), gate_b, with_ctx_out)

    def merge(h, o_in, g_in):
        m_out = mlstm_head_out(h, mnorm_w, o_in)
        u_out = conformer_conv(g_in, dw_w, dw_b, ln_w, ln_b)
        return jnp.concatenate([m_out, u_out], axis=-1) @ w_out

    y_lat = merge(h_lat, o_pre, glu_in)
    y_ctx = merge(h_ctx, o_c, glu_c) if with_ctx_out else None
    return y_lat, y_ctx


def moe_swiglu(h, router_w, router_b, w_gate, w_up, w_down):
    logits = (h @ router_w).astype(jnp.float32) + router_b.astype(jnp.float32)
    top_val, top_idx = lax.top_k(logits, TOP_K)
    top_w = jax.nn.softmax(top_val, axis=-1)
    combine = jnp.einsum('btk,btke->bte', top_w,
                         jax.nn.one_hot(top_idx, N_EXPERTS, dtype=jnp.float32)).astype(h.dtype)
    out = jnp.zeros_like(h)
    for e in range(N_EXPERTS):
        out = out + combine[..., e:e + 1] * swiglu(h, w_gate[e], w_up[e], w_down[e])
    return out


def setup_inputs(seed: int = 0) -> dict:
    key = jax.random.key(seed)
    ks = jax.random.split(key, 29)
    f32 = jnp.float32
    dm = D_MODEL

    def nrm(i, shape, fan_in, gain=1.0):
        return jax.random.normal(ks[i], shape, f32) * (gain * fan_in ** -0.5)

    def near_one(i, shape):
        return 1.0 + 0.02 * jax.random.normal(ks[i], shape, f32)

    def small(i, shape, s=0.02):
        return s * jax.random.normal(ks[i], shape, f32)

    gate_base = jnp.tile(jnp.repeat(jnp.array([0.0, 3.0, 0.0, 3.0], f32), C_HEADS), (N_ODD, 1))
    return {
        'x': jax.random.normal(ks[0], (BATCH, SEQ, dm), f32),
        'c': jax.random.normal(ks[1], (BATCH, dm), f32),
        'ctx': jax.random.normal(ks[2], (BATCH, CTX_LEN, dm), f32),
        'c_ctx': jax.random.normal(ks[3], (dm,), f32),
        'ada_w': nrm(4, (DEPTH, dm, 6 * dm), dm, 0.5),
        'ada_b': small(5, (DEPTH, 6 * dm)),
        'norm1_w': near_one(6, (DEPTH, dm)),
        'norm2_w': near_one(7, (DEPTH, dm)),
        'ev_w_in': nrm(8, (N_EVEN, dm, sum(EVEN_SPLITS)), dm),
        'ev_sink': 0.5 * jax.random.normal(ks[9], (N_EVEN, A_HEADS), f32),
        'ev_conv_w': nrm(10, (N_EVEN, B_CONV, B_WIDTH), B_CONV),
        'ev_w_out': nrm(11, (N_EVEN, MIX_WIDTH, dm), MIX_WIDTH),
        'ffn_w_gate': nrm(12, (N_EVEN, dm, FFN_DIM), dm),
        'ffn_w_up': nrm(13, (N_EVEN, dm, FFN_DIM), dm),
        'ffn_w_down': nrm(14, (N_EVEN, FFN_DIM, dm), FFN_DIM),
        'od_w_in': nrm(15, (N_ODD, dm, sum(ODD_SPLITS)), dm),
        'od_gate_b': gate_base + small(16, (N_ODD, 4 * C_HEADS), 0.3),
        'od_mnorm_w': near_one(17, (N_ODD, C_V_WIDTH)),
        'od_dw_w': nrm(18, (N_ODD, D_CONV, D_WIDTH), D_CONV),
        'od_dw_b': small(19, (N_ODD, D_WIDTH)),
        'od_ln_w': near_one(20, (N_ODD, D_WIDTH)),
        'od_ln_b': small(21, (N_ODD, D_WIDTH)),
        'od_w_out': nrm(22, (N_ODD, MIX_WIDTH, dm), MIX_WIDTH),
        'moe_router_w': nrm(23, (N_ODD, dm, N_EXPERTS), dm),
        'moe_router_b': small(24, (N_ODD, N_EXPERTS), 0.01),
        'moe_w_gate': nrm(25, (N_ODD, N_EXPERTS, dm, EXPERT_DIM), dm),
        'moe_w_up': nrm(26, (N_ODD, N_EXPERTS, dm, EXPERT_DIM), dm),
        'moe_w_down': nrm(27, (N_ODD, N_EXPERTS, EXPERT_DIM, dm), EXPERT_DIM),
        'final_w': near_one(28, (dm,)),
    }


def reference(x, c, ctx, c_ctx, ada_w, ada_b, norm1_w, norm2_w,
              ev_w_in, ev_sink, ev_conv_w, ev_w_out, ffn_w_gate, ffn_w_up, ffn_w_down,
              od_w_in, od_gate_b, od_mnorm_w, od_dw_w, od_dw_b, od_ln_w, od_ln_b, od_w_out,
              moe_router_w, moe_router_b, moe_w_gate, moe_w_up, moe_w_down, final_w):
    n_lat = x.shape[1]
    ROWS = n_lat // GRID_W
    cos, sin = axial_rope_tables(ROWS)
    cos = cos.astype(x.dtype)
    sin = sin.astype(x.dtype)
    silu_c = jax.nn.silu(c)
    silu_cc = jax.nn.silu(c_ctx)
    h_lat, h_ctx = x, ctx
    for layer in range(DEPTH):
        last = layer == DEPTH - 1
        j = layer // 2
        mod_l = jnp.split((silu_c @ ada_w[layer] + ada_b[layer])[:, None, :], 6, axis=-1)
        mod_c = jnp.split((silu_cc @ ada_w[layer] + ada_b[layer])[None, None, :], 6, axis=-1)
        a_lat = modulate(rmsnorm(h_lat, norm1_w[layer]), mod_l[0], mod_l[1])
        a_ctx = modulate(rmsnorm(h_ctx, norm1_w[layer]), mod_c[0], mod_c[1])
        if layer % 2 == 0:
            y_lat, y_ctx = even_mixer(a_lat, a_ctx, ev_w_in[j], ev_sink[j], ev_conv_w[j], ev_w_out[j],
                                      cos, sin, not last)
        else:
            y_lat, y_ctx = odd_mixer(a_lat, a_ctx, od_w_in[j], od_gate_b[j], od_mnorm_w[j], od_dw_w[j],
                                     od_dw_b[j], od_ln_w[j], od_ln_b[j], od_w_out[j], not last)

        def channel_mixer(t):
            if layer % 2 == 0:
                return swiglu(t, ffn_w_gate[j], ffn_w_up[j], ffn_w_down[j])
            return moe_swiglu(t, moe_router_w[j], moe_router_b[j], moe_w_gate[j], moe_w_up[j], moe_w_down[j])

        h_lat = h_lat + mod_l[2] * y_lat
        h_lat = h_lat + mod_l[5] * channel_mixer(modulate(rmsnorm(h_lat, norm2_w[layer]), mod_l[3], mod_l[4]))
        if not last:
            h_ctx = h_ctx + mod_c[2] * y_ctx
            h_ctx = h_ctx + mod_c[5] * channel_mixer(modulate(rmsnorm(h_ctx, norm2_w[layer]), mod_c[3], mod_c[4]))
    return rmsnorm(h_lat, final_w)
```

```python
import functools

import jax
import jax.numpy as jnp
from jax import lax
from jax.experimental import pallas as pl
from jax.experimental.pallas import tpu as pltpu

F32 = jnp.float32
BF16 = jnp.bfloat16

NORM_EPS = 1e-6
ROPE_BASE = 10000.0
GRID_W = 64
HEAD_DIM = 128
Q_HEADS = 8
KV_HEADS = 2
ATT_BLOCK = 128
M_HEADS = 4
M_QK = 128
M_V = 256
M_CHUNK = 128
CONV_SHORT = 3
CONV_LONG = 31
N_EXPERTS = 8
TOP_K = 2

ROW_TILE = 512
CONV_TILE = 256
HALO = 16
NEG = -1e30
V7X_VMEM_LIMIT = 56 * 1024 * 1024


def _params(n_axes):
    return pltpu.CompilerParams(dimension_semantics=("arbitrary",) * n_axes, vmem_limit_bytes=V7X_VMEM_LIMIT)


def _sigmoid(x):
    return 1.0 / (1.0 + jnp.exp(-x))


def _silu(x):
    return x * _sigmoid(x)


def _ada_kernel(c_ref, w_ref, b_ref, o_ref):
    s = _silu(c_ref[...]).astype(BF16)
    o_ref[...] = jnp.dot(s, w_ref[...].astype(BF16), preferred_element_type=F32) + b_ref[...]


def _ada_table(cvec, ada_w, ada_b, tn=1024):
    depth, d, n = ada_w.shape
    rows = cvec.shape[0]
    return pl.pallas_call(
        _ada_kernel,
        grid=(depth, n // tn),
        in_specs=[pl.BlockSpec((rows, d), lambda l, j: (0, 0)),
                  pl.BlockSpec((None, d, tn), lambda l, j: (l, 0, j)),
                  pl.BlockSpec((None, 1, tn), lambda l, j: (l, 0, j))],
        out_specs=pl.BlockSpec((None, rows, tn), lambda l, j: (l, 0, j)),
        out_shape=jax.ShapeDtypeStruct((depth, rows, n), F32),
        compiler_params=_params(2),
        name="ada_table",
    )(cvec, ada_w, ada_b.reshape(depth, 1, n))


def _normmod(x, w, shift, scale):
    ms = jnp.mean(x * x, axis=-1, keepdims=True)
    y = x * lax.rsqrt(ms + NORM_EPS) * w
    return y * (1.0 + scale) + shift


def _normmod_kernel(h_ref, w_ref, sh_ref, sc_ref, o_ref):
    o_ref[...] = _normmod(h_ref[...], w_ref[...], sh_ref[...], sc_ref[...]).astype(o_ref.dtype)


def _route_kernel(h_ref, w_ref, sh_ref, sc_ref, rw_ref, rb_ref, o_ref, r_ref):
    a = _normmod(h_ref[...], w_ref[...], sh_ref[...], sc_ref[...])
    o_ref[...] = a.astype(o_ref.dtype)
    logits = jnp.dot(a, rw_ref[...], precision=lax.Precision.HIGHEST, preferred_element_type=F32) + rb_ref[...]
    lane = lax.broadcasted_iota(jnp.int32, logits.shape, 1)
    logits = jnp.where(lane < N_EXPERTS, logits, NEG)
    big = jnp.int32(1 << 20)
    m1 = jnp.max(logits, axis=-1, keepdims=True)
    i1 = jnp.min(jnp.where(logits == m1, lane, big), axis=-1, keepdims=True)
    rest = jnp.where(lane == i1, NEG, logits)
    m2 = jnp.max(rest, axis=-1, keepdims=True)
    i2 = jnp.min(jnp.where(rest == m2, lane, big), axis=-1, keepdims=True)
    e2 = jnp.exp(m2 - m1)
    w1 = 1.0 / (1.0 + e2)
    w2 = e2 / (1.0 + e2)
    r = jnp.where(lane == 0, i1.astype(F32), 0.0)
    r = jnp.where(lane == 1, i2.astype(F32), r)
    r = jnp.where(lane == 2, w1, r)
    r = jnp.where(lane == 3, w2, r)
    r_ref[...] = r


def _seg_index(tiles_per_seq):
    return lambda i: (i + tiles_per_seq - 1) // tiles_per_seq


def _norm_modulate(h, w, shift, scale, tiles_per_seq, out_dtype=BF16, row_offset_tiles=0, n_rows=None):
    t, d = h.shape
    n_rows = t if n_rows is None else n_rows
    seg = _seg_index(tiles_per_seq)
    off = row_offset_tiles
    return pl.pallas_call(
        _normmod_kernel,
        grid=(n_rows // ROW_TILE,),
        in_specs=[pl.BlockSpec((ROW_TILE, d), lambda i: (i + off, 0)),
                  pl.BlockSpec((1, d), lambda i: (0, 0)),
                  pl.BlockSpec((None, 1, d), lambda i: (seg(i + off), 0, 0)),
                  pl.BlockSpec((None, 1, d), lambda i: (seg(i + off), 0, 0))],
        out_specs=pl.BlockSpec((ROW_TILE, d), lambda i: (i, 0)),
        out_shape=jax.ShapeDtypeStruct((n_rows, d), out_dtype),
        compiler_params=_params(1),
        name="norm_modulate",
    )(h, w.reshape(1, d), shift, scale)


def _norm_modulate_route(h, w, shift, scale, router_w, router_b, tiles_per_seq):
    t, d = h.shape
    seg = _seg_index(tiles_per_seq)
    rw = jnp.pad(router_w, ((0, 0), (0, 128 - N_EXPERTS)))
    rb = jnp.pad(router_b, (0, 128 - N_EXPERTS)).reshape(1, 128)
    return pl.pallas_call(
        _route_kernel,
        grid=(t // ROW_TILE,),
        in_specs=[pl.BlockSpec((ROW_TILE, d), lambda i: (i, 0)),
                  pl.BlockSpec((1, d), lambda i: (0, 0)),
                  pl.BlockSpec((None, 1, d), lambda i: (seg(i), 0, 0)),
                  pl.BlockSpec((None, 1, d), lambda i: (seg(i), 0, 0)),
                  pl.BlockSpec((d, 128), lambda i: (0, 0)),
                  pl.BlockSpec((1, 128), lambda i: (0, 0))],
        out_specs=[pl.BlockSpec((ROW_TILE, d), lambda i: (i, 0)),
                   pl.BlockSpec((ROW_TILE, 128), lambda i: (i, 0))],
        out_shape=[jax.ShapeDtypeStruct((t, d), BF16), jax.ShapeDtypeStruct((t, 128), F32)],
        compiler_params=_params(1),
        name="norm_modulate_route",
    )(h, w.reshape(1, d), shift, scale, rw, rb)


def _mm_kernel(x_ref, w_ref, o_ref, wb_ref):
    @pl.when(pl.program_id(1) == 0)
    def _():
        wb_ref[...] = w_ref[...].astype(BF16)

    o_ref[...] = jnp.dot(x_ref[...], wb_ref[...], preferred_element_type=F32).astype(o_ref.dtype)


def _matmul(x, w3, layer, n_cols, tn=512, out_dtype=F32):
    t, k = x.shape
    return pl.pallas_call(
        _mm_kernel,
        grid=(n_cols // tn, t // ROW_TILE),
        in_specs=[pl.BlockSpec((ROW_TILE, k), lambda j, i: (i, 0)),
                  pl.BlockSpec((None, k, tn), lambda j, i: (layer, 0, j))],
        out_specs=pl.BlockSpec((ROW_TILE, tn), lambda j, i: (i, j)),
        out_shape=jax.ShapeDtypeStruct((t, n_cols), out_dtype),
        scratch_shapes=[pltpu.VMEM((k, tn), BF16)],
        compiler_params=_params(2),
        name="matmul",
    )(x, w3)


def _mm2_resid_kernel(x1_ref, x2_ref, w1_ref, w2_ref, h_ref, g_ref, o_ref, wb1_ref, wb2_ref):
    @pl.when(pl.program_id(1) == 0)
    def _():
        wb1_ref[...] = w1_ref[...].astype(BF16)
        wb2_ref[...] = w2_ref[...].astype(BF16)

    y = jnp.dot(x1_ref[...], wb1_ref[...], preferred_element_type=F32)
    y = y + jnp.dot(x2_ref[...], wb2_ref[...], preferred_element_type=F32)
    o_ref[...] = h_ref[...] + g_ref[...] * y


def _out_proj_residual(x1, x2, w3, layer, h, gate, tiles_per_seq, tn=512):
    t, k1 = x1.shape
    k2 = x2.shape[1]
    assert k1 == k2
    n = w3.shape[2]
    seg = _seg_index(tiles_per_seq)
    return pl.pallas_call(
        _mm2_resid_kernel,
        grid=(n // tn, t // ROW_TILE),
        in_specs=[pl.BlockSpec((ROW_TILE, k1), lambda j, i: (i, 0)),
                  pl.BlockSpec((ROW_TILE, k2), lambda j, i: (i, 0)),
                  pl.BlockSpec((None, k1, tn), lambda j, i: (layer, 0, j)),
                  pl.BlockSpec((None, k2, tn), lambda j, i: (layer, 1, j)),
                  pl.BlockSpec((ROW_TILE, tn), lambda j, i: (i, j)),
                  pl.BlockSpec((None, 1, tn), lambda j, i: (seg(i), 0, j))],
        out_specs=pl.BlockSpec((ROW_TILE, tn), lambda j, i: (i, j)),
        out_shape=jax.ShapeDtypeStruct((t, n), F32),
        scratch_shapes=[pltpu.VMEM((k1, tn), BF16), pltpu.VMEM((k2, tn), BF16)],
        compiler_params=_params(2),
        name="out_proj_residual",
    )(x1, x2, w3, w3, h, gate)


def _ffn_up_kernel(x_ref, wg_ref, wu_ref, o_ref, wgb_ref, wub_ref):
    @pl.when(pl.program_id(1) == 0)
    def _():
        wgb_ref[...] = wg_ref[...].astype(BF16)
        wub_ref[...] = wu_ref[...].astype(BF16)

    x = x_ref[...]
    g = jnp.dot(x, wgb_ref[...], preferred_element_type=F32)
    u = jnp.dot(x, wub_ref[...], preferred_element_type=F32)
    o_ref[...] = (_silu(g) * u).astype(o_ref.dtype)


def _ffn_up(x, w_gate, w_up, layer, tf=512):
    t, k = x.shape
    f = w_gate.shape[2]
    return pl.pallas_call(
        _ffn_up_kernel,
        grid=(f // tf, t // ROW_TILE),
        in_specs=[pl.BlockSpec((ROW_TILE, k), lambda j, i: (i, 0)),
                  pl.BlockSpec((None, k, tf), lambda j, i: (layer, 0, j)),
                  pl.BlockSpec((None, k, tf), lambda j, i: (layer, 0, j))],
        out_specs=pl.BlockSpec((ROW_TILE, tf), lambda j, i: (i, j)),
        out_shape=jax.ShapeDtypeStruct((t, f), BF16),
        scratch_shapes=[pltpu.VMEM((k, tf), BF16), pltpu.VMEM((k, tf), BF16)],
        compiler_params=_params(2),
        name="ffn_up",
    )(x, w_gate, w_up)


def _mm_resid_kernel(x_ref, w_ref, h_ref, g_ref, o_ref, wb_ref):
    @pl.when(pl.program_id(1) == 0)
    def _():
        wb_ref[...] = w_ref[...].astype(BF16)

    y = jnp.dot(x_ref[...], wb_ref[...], preferred_element_type=F32)
    o_ref[...] = h_ref[...] + g_ref[...] * y


def _ffn_down_residual(x, w3, layer, h, gate, tiles_per_seq, tn=512):
    t, k = x.shape
    n = w3.shape[2]
    seg = _seg_index(tiles_per_seq)
    return pl.pallas_call(
        _mm_resid_kernel,
        grid=(n // tn, t // ROW_TILE),
        in_specs=[pl.BlockSpec((ROW_TILE, k), lambda j, i: (i, 0)),
                  pl.BlockSpec((None, k, tn), lambda j, i: (layer, 0, j)),
                  pl.BlockSpec((ROW_TILE, tn), lambda j, i: (i, j)),
                  pl.BlockSpec((None, 1, tn), lambda j, i: (seg(i), 0, j))],
        out_specs=pl.BlockSpec((ROW_TILE, tn), lambda j, i: (i, j)),
        out_shape=jax.ShapeDtypeStruct((t, n), F32),
        scratch_shapes=[pltpu.VMEM((k, tn), BF16)],
        compiler_params=_params(2),
        name="ffn_down_residual",
    )(x, w3, h, gate)


def _rope(x, cs):
    return x * cs[:, :HEAD_DIM] + pltpu.roll(x, HEAD_DIM // 2, axis=1) * cs[:, HEAD_DIM:]


def _attend(q_all, k_parts, v_parts, sink_ref, o_ref, mask_fn):
    group = Q_HEADS // KV_HEADS
    scale = HEAD_DIM ** -0.5
    for g in range(KV_HEADS):
        qg = jnp.concatenate([q_all[:, (g * group + r) * HEAD_DIM:(g * group + r + 1) * HEAD_DIM]
                              for r in range(group)], axis=0).astype(BF16)
        kg = jnp.concatenate([kp[:, g * HEAD_DIM:(g + 1) * HEAD_DIM] for kp in k_parts], axis=0).astype(BF16)
        vg = jnp.concatenate([vp[:, g * HEAD_DIM:(g + 1) * HEAD_DIM] for vp in v_parts], axis=0).astype(BF16)
        s = lax.dot_general(qg, kg, (((1,), (1,)), ((), ())), preferred_element_type=F32) * scale
        if mask_fn is not None:
            s = jnp.where(mask_fn(s.shape), s, NEG)
        sink = sink_ref[g]
        m = jnp.maximum(jnp.max(s, axis=-1, keepdims=True), sink)
        p = jnp.exp(s - m)
        denom = jnp.sum(p, axis=-1, keepdims=True) + jnp.exp(sink - m)
        o = jnp.dot(p.astype(BF16), vg, preferred_element_type=F32) / denom
        for r in range(group):
            hq = g * group + r
            o_ref[:, hq * HEAD_DIM:(hq + 1) * HEAD_DIM] = o[r * ATT_BLOCK:(r + 1) * ATT_BLOCK].astype(o_ref.dtype)


def _win_attn_kernel(q_ref, kp_ref, kc_ref, kn_ref, kx_ref, vp_ref, vc_ref, vn_ref, vx_ref,
                     csp_ref, csc_ref, csn_ref, sink_ref, o_ref, *, n_blocks):
    n = pl.program_id(1)
    csc = csc_ref[...]
    q_all = jnp.concatenate([_rope(q_ref[:, h * HEAD_DIM:(h + 1) * HEAD_DIM], csc) for h in range(Q_HEADS)], axis=1)

    def rope_kv(k_ref, cs):
        return jnp.concatenate([_rope(k_ref[:, g * HEAD_DIM:(g + 1) * HEAD_DIM], cs) for g in range(KV_HEADS)], axis=1)

    k_parts = [rope_kv(kp_ref, csp_ref[...]), rope_kv(kc_ref, csc), rope_kv(kn_ref, csn_ref[...]), kx_ref[...]]
    v_parts = [vp_ref[...], vc_ref[...], vn_ref[...], vx_ref[...]]

    def mask_fn(shape):
        row = lax.broadcasted_iota(jnp.int32, shape, 0) & (ATT_BLOCK - 1)
        col = lax.broadcasted_iota(jnp.int32, shape, 1)
        band = (col >= row) & (col <= row + 2 * ATT_BLOCK)
        ok_prev = (col >= ATT_BLOCK) | (n > 0)
        ok_next = (col < 2 * ATT_BLOCK) | (n < n_blocks - 1)
        return (band & ok_prev & ok_next) | (col >= 3 * ATT_BLOCK)

    _attend(q_all, k_parts, v_parts, sink_ref, o_ref, mask_fn)


def _ctx_attn_kernel(q_ref, kx_ref, vx_ref, sink_ref, latent_rows_ref, o_ref):
    del latent_rows_ref
    _attend(q_ref[...], [kx_ref[...]], [vx_ref[...]], sink_ref, o_ref, None)


def _sink_rows(sink):
    group = Q_HEADS // KV_HEADS
    return jnp.repeat(sink.astype(F32).reshape(KV_HEADS, group), ATT_BLOCK, axis=1).reshape(
        KV_HEADS, group * ATT_BLOCK, 1)


def _window_attention(z, cs, sink, batch, ctx_len, seq):
    t = z.shape[0]
    nb = seq // ATT_BLOCK
    cb = ctx_len // ATT_BLOCK
    base = batch * cb
    qw = Q_HEADS * HEAD_DIM
    kw = KV_HEADS * HEAD_DIM
    kcol = qw // kw
    vcol = kcol + 1
    sink_rows = _sink_rows(sink)

    def lat(b, n):
        return base + b * nb + n

    def prev(n):
        return jnp.maximum(n - 1, 0)

    def nxt(n):
        return jnp.minimum(n + 1, nb - 1)

    o_lat = pl.pallas_call(
        functools.partial(_win_attn_kernel, n_blocks=nb),
        grid=(batch, nb),
        in_specs=[pl.BlockSpec((ATT_BLOCK, qw), lambda b, n: (lat(b, n), 0)),
                  pl.BlockSpec((ATT_BLOCK, kw), lambda b, n: (lat(b, prev(n)), kcol)),
                  pl.BlockSpec((ATT_BLOCK, kw), lambda b, n: (lat(b, n), kcol)),
                  pl.BlockSpec((ATT_BLOCK, kw), lambda b, n: (lat(b, nxt(n)), kcol)),
                  pl.BlockSpec((ctx_len, kw), lambda b, n: (b, kcol)),
                  pl.BlockSpec((ATT_BLOCK, kw), lambda b, n: (lat(b, prev(n)), vcol)),
                  pl.BlockSpec((ATT_BLOCK, kw), lambda b, n: (lat(b, n), vcol)),
                  pl.BlockSpec((ATT_BLOCK, kw), lambda b, n: (lat(b, nxt(n)), vcol)),
                  pl.BlockSpec((ctx_len, kw), lambda b, n: (b, vcol)),
                  pl.BlockSpec((ATT_BLOCK, 2 * HEAD_DIM), lambda b, n: (prev(n), 0)),
                  pl.BlockSpec((ATT_BLOCK, 2 * HEAD_DIM), lambda b, n: (n, 0)),
                  pl.BlockSpec((ATT_BLOCK, 2 * HEAD_DIM), lambda b, n: (nxt(n), 0)),
                  pl.BlockSpec(sink_rows.shape, lambda b, n: (0, 0, 0))],
        out_specs=pl.BlockSpec((ATT_BLOCK, qw), lambda b, n: (lat(b, n), 0)),
        out_shape=jax.ShapeDtypeStruct((t, qw), BF16),
        compiler_params=_params(2),
        name="window_attention",
    )(z, z, z, z, z, z, z, z, z, cs, cs, cs, sink_rows)

    return pl.pallas_call(
        _ctx_attn_kernel,
        grid=(batch, cb),
        in_specs=[pl.BlockSpec((ATT_BLOCK, qw), lambda b, n: (b * cb + n, 0)),
                  pl.BlockSpec((ctx_len, kw), lambda b, n: (b, kcol)),
                  pl.BlockSpec((ctx_len, kw), lambda b, n: (b, vcol)),
                  pl.BlockSpec(sink_rows.shape, lambda b, n: (0, 0, 0)),
                  pl.BlockSpec(memory_space=pl.ANY)],
        out_specs=pl.BlockSpec((ATT_BLOCK, qw), lambda b, n: (b * cb + n, 0)),
        out_shape=jax.ShapeDtypeStruct((t, qw), BF16),
        input_output_aliases={4: 0},
        compiler_params=_params(2),
        name="context_attention",
    )(z, z, z, sink_rows, o_lat)


def _seq_edges(t, batch, ctx_tiles, lat_tiles):
    n_ctx = batch * ctx_tiles
    u = t - n_ctx
    is_ctx = t < n_ctx
    first = jnp.where(is_ctx, lax.rem(t, ctx_tiles) == 0, lax.rem(u, lat_tiles) == 0)
    last = jnp.where(is_ctx, lax.rem(t, ctx_tiles) == ctx_tiles - 1, lax.rem(u, lat_tiles) == lat_tiles - 1)
    return first, last


def _fill_ext(ext_ref, cur, prev, nxt, first, last):
    ext_ref[HALO:HALO + CONV_TILE, :] = cur
    ext_ref[0:HALO, :] = jnp.where(first, 0.0, prev)
    ext_ref[HALO + CONV_TILE:, :] = jnp.where(last, 0.0, nxt)


def _dwconv_rows(ext_ref, w_ref, taps, r0, n_rows):
    pad = taps // 2
    acc = None
    for k in range(taps):
        start = HALO + r0 + k - pad
        term = w_ref[k:k + 1, :] * ext_ref[start:start + n_rows, :]
        acc = term if acc is None else acc + term
    return acc


CONV_ROWS = 32


def _short_conv_kernel(bg_ref, cg_ref, xb_ref, cgp_ref, xbp_ref, cgn_ref, xbn_ref, w_ref, o_ref, ext_ref,
                       *, batch, ctx_tiles, lat_tiles):
    first, last = _seq_edges(pl.program_id(0), batch, ctx_tiles, lat_tiles)
    _fill_ext(ext_ref, cg_ref[...] * xb_ref[...], cgp_ref[...] * xbp_ref[...], cgn_ref[...] * xbn_ref[...],
              first, last)
    for r0 in range(0, CONV_TILE, CONV_ROWS):
        conv = _dwconv_rows(ext_ref, w_ref, CONV_SHORT, r0, CONV_ROWS)
        o_ref[r0:r0 + CONV_ROWS, :] = (bg_ref[r0:r0 + CONV_ROWS, :] * conv).astype(o_ref.dtype)


def _halo_specs(width, col, halo_per_tile, n_halo_blocks):
    prev = pl.BlockSpec((HALO, width), lambda t, c: (jnp.maximum(t * halo_per_tile - 1, 0), col(c)))
    nxt = pl.BlockSpec((HALO, width), lambda t, c: (jnp.minimum((t + 1) * halo_per_tile, n_halo_blocks - 1), col(c)))
    return prev, nxt


def _short_conv(z, conv_w3, layer, batch, ctx_len, seq):
    t = z.shape[0]
    cw = 512
    n_c = 1024 // cw
    bg0, cg0, xb0 = 1536 // cw, 2560 // cw, 3584 // cw
    hpt = CONV_TILE // HALO
    nh = t // HALO
    cgp, cgn = _halo_specs(cw, lambda c: cg0 + c, hpt, nh)
    xbp, xbn = _halo_specs(cw, lambda c: xb0 + c, hpt, nh)
    kern = functools.partial(_short_conv_kernel, batch=batch, ctx_tiles=ctx_len // CONV_TILE,
                             lat_tiles=seq // CONV_TILE)
    return pl.pallas_call(
        kern,
        grid=(t // CONV_TILE, n_c),
        in_specs=[pl.BlockSpec((CONV_TILE, cw), lambda t_, c: (t_, bg0 + c)),
                  pl.BlockSpec((CONV_TILE, cw), lambda t_, c: (t_, cg0 + c)),
                  pl.BlockSpec((CONV_TILE, cw), lambda t_, c: (t_, xb0 + c)),
                  cgp, xbp, cgn, xbn,
                  pl.BlockSpec((None, CONV_SHORT, cw), lambda t_, c: (layer, 0, c))],
        out_specs=pl.BlockSpec((CONV_TILE, cw), lambda t_, c: (t_, c)),
        out_shape=jax.ShapeDtypeStruct((t, 1024), BF16),
        scratch_shapes=[pltpu.VMEM((CONV_TILE + 2 * HALO, cw), F32)],
        compiler_params=_params(2),
        name="short_conv",
    )(z, z, z, z, z, z, z, conv_w3)


def _conformer_kernel(a_ref, g_ref, ap_ref, gp_ref, an_ref, gn_ref, w_ref, b_ref, lw_ref, lb_ref, o_ref, ext_ref,
                      *, batch, ctx_tiles, lat_tiles):
    first, last = _seq_edges(pl.program_id(0), batch, ctx_tiles, lat_tiles)
    _fill_ext(ext_ref, a_ref[...] * _sigmoid(g_ref[...]), ap_ref[...] * _sigmoid(gp_ref[...]),
              an_ref[...] * _sigmoid(gn_ref[...]), first, last)
    for r0 in range(0, CONV_TILE, CONV_ROWS):
        u = _dwconv_rows(ext_ref, w_ref, CONV_LONG, r0, CONV_ROWS) + b_ref[...]
        mu = jnp.mean(u, axis=-1, keepdims=True)
        uc = u - mu
        var = jnp.mean(uc * uc, axis=-1, keepdims=True)
        y = uc * lax.rsqrt(var + NORM_EPS) * lw_ref[...] + lb_ref[...]
        o_ref[r0:r0 + CONV_ROWS, :] = _silu(y).astype(o_ref.dtype)


def _conformer_conv(z, a_col, g_col, dw_w3, dw_b, ln_w, ln_b, layer, batch, ctx_len, seq):
    t = z.shape[0]
    cw = 1024
    hpt = CONV_TILE // HALO
    nh = t // HALO
    ap, an = _halo_specs(cw, lambda c: a_col, hpt, nh)
    gp, gn = _halo_specs(cw, lambda c: g_col, hpt, nh)
    kern = functools.partial(_conformer_kernel, batch=batch, ctx_tiles=ctx_len // CONV_TILE,
                             lat_tiles=seq // CONV_TILE)
    vec = pl.BlockSpec((None, 1, cw), lambda t_, c: (layer, 0, 0))
    n_layers = dw_b.shape[0]
    return pl.pallas_call(
        kern,
        grid=(t // CONV_TILE, 1),
        in_specs=[pl.BlockSpec((CONV_TILE, cw), lambda t_, c: (t_, a_col)),
                  pl.BlockSpec((CONV_TILE, cw), lambda t_, c: (t_, g_col)),
                  ap, gp, an, gn,
                  pl.BlockSpec((None, CONV_LONG, cw), lambda t_, c: (layer, 0, 0)),
                  vec, vec, vec],
        out_specs=pl.BlockSpec((CONV_TILE, cw), lambda t_, c: (t_, 0)),
        out_shape=jax.ShapeDtypeStruct((t, cw), BF16),
        scratch_shapes=[pltpu.VMEM((CONV_TILE + 2 * HALO, cw), F32)],
        compiler_params=_params(2),
        name="conformer_conv",
    )(z, z, z, z, z, z, dw_w3, dw_b.reshape(n_layers, 1, cw), ln_w.reshape(n_layers, 1, cw),
      ln_b.reshape(n_layers, 1, cw))


def _log_sigmoid(x):
    return jnp.minimum(x, 0.0) - jnp.log(1.0 + jnp.exp(-jnp.abs(x)))


def _mlstm_kernel(q_ref, k_ref, v_ref, gc_ref, gr_ref, bc_ref, br_ref, o_ref, ct_ref, n_ref, m_ref):
    d = pl.program_id(1)

    @pl.when(pl.program_id(2) == 0)
    def _():
        ct_ref[...] = jnp.zeros_like(ct_ref)
        n_ref[...] = jnp.zeros_like(n_ref)
        m_ref[...] = jnp.zeros_like(m_ref)

    ln = M_CHUNK
    row = lax.broadcasted_iota(jnp.int32, (ln, ln), 0)
    col = lax.broadcasted_iota(jnp.int32, (ln, ln), 1)
    sgn = jnp.where(d == 0, 1, -1)
    mask = sgn * (row - col) >= 0
    mask_t = sgn * (col - row) >= 0
    gc = gc_ref[...] + bc_ref[...]
    gr = gr_ref[...] + br_ref[...]
    for h in range(M_HEADS):
        ig_c = gc[:, h:h + 1]
        lf_c = _log_sigmoid(gc[:, M_HEADS + h:M_HEADS + h + 1])
        ig_r = gr[h:h + 1, :]
        lf_r = _log_sigmoid(gr[M_HEADS + h:M_HEADS + h + 1, :])
        q = q_ref[:, h * M_QK:(h + 1) * M_QK] * (M_QK ** -0.5)
        k = k_ref[:, h * M_QK:(h + 1) * M_QK]
        v = v_ref[:, h * M_V:(h + 1) * M_V]
        qb = q.astype(BF16)
        kb = k.astype(BF16)
        cum_c = jnp.sum(jnp.where(mask, lf_r, 0.0), axis=1, keepdims=True)
        cum_r = jnp.sum(jnp.where(mask_t, lf_c, 0.0), axis=0, keepdims=True)
        total = jnp.sum(lf_r, axis=1, keepdims=True)
        m_s = m_ref[h]
        dlog = jnp.where(mask, cum_c - cum_r + ig_r, NEG)
        inter = cum_c + m_s
        m_t = jnp.maximum(inter, jnp.max(dlog, axis=1, keepdims=True))
        w_inter = jnp.exp(inter - m_t)
        s = lax.dot_general(qb, kb, (((1,), (1,)), ((), ())), preferred_element_type=F32) * jnp.exp(dlog - m_t)
        ct = ct_ref[h]
        n_v = n_ref[h]
        num = (w_inter * jnp.dot(qb, ct.astype(BF16), preferred_element_type=F32)
               + jnp.dot(s.astype(BF16), v.astype(BF16), preferred_element_type=F32))
        den = w_inter * jnp.sum(q * n_v, axis=1, keepdims=True) + jnp.sum(s, axis=1, keepdims=True)
        o_ref[:, h * M_V:(h + 1) * M_V] = num / jnp.maximum(jnp.abs(den), jnp.exp(-m_t))
        gl = total - cum_c + ig_c
        m_new = jnp.maximum(total + m_s, jnp.max(gl, axis=0, keepdims=True))
        wg = jnp.exp(gl - m_new)
        decay = jnp.exp(total + m_s - m_new)
        ct_ref[h] = decay * ct + jnp.dot(k.T.astype(BF16), (wg * v).astype(BF16), preferred_element_type=F32)
        n_ref[h] = decay * n_v + jnp.sum(wg * k, axis=0, keepdims=True)
        m_ref[h] = m_new


def _mlstm(zq, gates, gate_b, batch, ctx_len, seq):
    t = zq.shape[0]
    cb = ctx_len // M_CHUNK
    nc = seq // M_CHUNK
    base = batch * cb
    steps = cb + nc
    half = 2 * M_HEADS
    g_dir = gates.reshape(t, 2, half).transpose(1, 0, 2)
    g_dir_t = g_dir.transpose(0, 2, 1)
    b_dir = gate_b.astype(F32).reshape(2, 1, half)
    b_dir_t = b_dir.transpose(0, 2, 1)

    def rb(b, d, i):
        in_ctx = i < cb
        fwd = jnp.where(in_ctx, b * cb + i, base + b * nc + (i - cb))
        bwd = jnp.where(in_ctx, b * cb + (cb - 1 - i), base + b * nc + (nc - 1 - (i - cb)))
        return jnp.where(d == 0, fwd, bwd)

    qk_w = M_HEADS * M_QK
    v_w = M_HEADS * M_V
    return pl.pallas_call(
        _mlstm_kernel,
        grid=(batch, 2, steps),
        in_specs=[pl.BlockSpec((M_CHUNK, qk_w), lambda b, d, i: (rb(b, d, i), 0)),
                  pl.BlockSpec((M_CHUNK, qk_w), lambda b, d, i: (rb(b, d, i), 1)),
                  pl.BlockSpec((M_CHUNK, v_w), lambda b, d, i: (rb(b, d, i), 1)),
                  pl.BlockSpec((None, M_CHUNK, half), lambda b, d, i: (d, rb(b, d, i), 0)),
                  pl.BlockSpec((None, half, M_CHUNK), lambda b, d, i: (d, 0, rb(b, d, i))),
                  pl.BlockSpec((None, 1, half), lambda b, d, i: (d, 0, 0)),
                  pl.BlockSpec((None, half, 1), lambda b, d, i: (d, 0, 0))],
        out_specs=pl.BlockSpec((None, M_CHUNK, v_w), lambda b, d, i: (d, rb(b, d, i), 0)),
        out_shape=jax.ShapeDtypeStruct((2, t, v_w), F32),
        scratch_shapes=[pltpu.VMEM((M_HEADS, M_QK, M_V), F32),
                        pltpu.VMEM((M_HEADS, 1, M_QK), F32),
                        pltpu.VMEM((M_HEADS, 1, 1), F32)],
        compiler_params=_params(3),
        name="mlstm",
    )(zq, zq, zq, g_dir, g_dir_t, b_dir, b_dir_t)


def _head_out_kernel(hf_ref, hb_ref, o_ref, w_ref, out_ref):
    for h in range(M_HEADS):
        sl = slice(h * M_V, (h + 1) * M_V)
        x = hf_ref[:, sl] + hb_ref[:, sl]
        mu = jnp.mean(x, axis=-1, keepdims=True)
        xc = x - mu
        var = jnp.mean(xc * xc, axis=-1, keepdims=True)
        y = xc * lax.rsqrt(var + NORM_EPS) * w_ref[:, sl]
        out_ref[:, sl] = (y * _sigmoid(o_ref[:, sl])).astype(out_ref.dtype)


def _mlstm_head_out(h_dir, z_rest, mnorm_w):
    _, t, w = h_dir.shape
    return pl.pallas_call(
        _head_out_kernel,
        grid=(t // ROW_TILE,),
        in_specs=[pl.BlockSpec((None, ROW_TILE, w), lambda i: (0, i, 0)),
                  pl.BlockSpec((None, ROW_TILE, w), lambda i: (1, i, 0)),
                  pl.BlockSpec((ROW_TILE, w), lambda i: (i, 0)),
                  pl.BlockSpec((1, w), lambda i: (0, 0))],
        out_specs=pl.BlockSpec((ROW_TILE, w), lambda i: (i, 0)),
        out_shape=jax.ShapeDtypeStruct((t, w), BF16),
        compiler_params=_params(1),
        name="mlstm_head_out",
    )(h_dir, h_dir, z_rest, mnorm_w.reshape(1, w))


def _fresh_weights(be_ref, rb):
    return (rb == 0) | (be_ref[rb] != be_ref[jnp.maximum(rb - 1, 0)])


def _moe_up_kernel(be_ref, nv_ref, x_ref, wg_ref, wu_ref, o_ref, wgb_ref, wub_ref):
    rb = pl.program_id(1)
    valid = rb < nv_ref[0]

    @pl.when(valid & _fresh_weights(be_ref, rb))
    def _():
        wgb_ref[...] = wg_ref[...].astype(BF16)
        wub_ref[...] = wu_ref[...].astype(BF16)

    @pl.when(valid)
    def _():
        x = x_ref[...]
        g = jnp.dot(x, wgb_ref[...], preferred_element_type=F32)
        u = jnp.dot(x, wub_ref[...], preferred_element_type=F32)
        o_ref[...] = (_silu(g) * u).astype(o_ref.dtype)

    @pl.when(jnp.logical_not(valid))
    def _():
        o_ref[...] = jnp.zeros_like(o_ref)


def _moe_down_kernel(be_ref, nv_ref, x_ref, w_ref, rw_ref, o_ref, wb_ref):
    rb = pl.program_id(1)
    valid = rb < nv_ref[0]

    @pl.when(valid & _fresh_weights(be_ref, rb))
    def _():
        wb_ref[...] = w_ref[...].astype(BF16)

    @pl.when(valid)
    def _():
        o_ref[...] = rw_ref[...] * jnp.dot(x_ref[...], wb_ref[...], preferred_element_type=F32)

    @pl.when(jnp.logical_not(valid))
    def _():
        o_ref[...] = jnp.zeros_like(o_ref)


def _moe_experts(x_sorted, row_w, block_expert, n_valid, w_gate, w_up, w_down, layer, tf=512, tn=512):
    p, d = x_sorted.shape
    f = w_gate.shape[3]
    nblk = p // ROW_TILE

    def row(rb, nv):
        return jnp.minimum(rb, nv[0] - 1)

    g = pl.pallas_call(
        _moe_up_kernel,
        grid_spec=pltpu.PrefetchScalarGridSpec(
            num_scalar_prefetch=2,
            grid=(f // tf, nblk),
            in_specs=[pl.BlockSpec((ROW_TILE, d), lambda j, rb, be, nv: (row(rb, nv), 0)),
                      pl.BlockSpec((None, None, d, tf), lambda j, rb, be, nv: (layer, be[rb], 0, j)),
                      pl.BlockSpec((None, None, d, tf), lambda j, rb, be, nv: (layer, be[rb], 0, j))],
            out_specs=pl.BlockSpec((ROW_TILE, tf), lambda j, rb, be, nv: (rb, j)),
            scratch_shapes=[pltpu.VMEM((d, tf), BF16), pltpu.VMEM((d, tf), BF16)]),
        out_shape=jax.ShapeDtypeStruct((p, f), BF16),
        compiler_params=_params(2),
        name="moe_up",
    )(block_expert, n_valid, x_sorted, w_gate, w_up)
    return pl.pallas_call(
        _moe_down_kernel,
        grid_spec=pltpu.PrefetchScalarGridSpec(
            num_scalar_prefetch=2,
            grid=(d // tn, nblk),
            in_specs=[pl.BlockSpec((ROW_TILE, f), lambda j, rb, be, nv: (row(rb, nv), 0)),
                      pl.BlockSpec((None, None, f, tn), lambda j, rb, be, nv: (layer, be[rb], 0, j)),
                      pl.BlockSpec((ROW_TILE, 1), lambda j, rb, be, nv: (row(rb, nv), 0))],
            out_specs=pl.BlockSpec((ROW_TILE, tn), lambda j, rb, be, nv: (rb, j)),
            scratch_shapes=[pltpu.VMEM((f, tn), BF16)]),
        out_shape=jax.ShapeDtypeStruct((p, d), F32),
        compiler_params=_params(2),
        name="moe_down",
    )(block_expert, n_valid, g, w_down, row_w)


def _routing_plan(route):
    t = route.shape[0]
    tm = ROW_TILE
    n_assign = TOP_K * t
    nblk = (n_assign + N_EXPERTS * (tm - 1)) // tm
    e = route[:, :TOP_K].astype(jnp.int32).reshape(-1)
    wts = route[:, TOP_K:2 * TOP_K].reshape(-1)
    onehot = (e[:, None] == jnp.arange(N_EXPERTS, dtype=jnp.int32)[None, :]).astype(jnp.int32)
    csum = jnp.cumsum(onehot, axis=0)
    rank = jnp.sum((csum - onehot) * onehot, axis=1)
    counts = csum[-1]
    padded = ((counts + tm - 1) // tm) * tm
    ends = jnp.cumsum(padded)
    starts = ends - padded
    dest = starts[e] + rank
    src_tok = jnp.zeros((nblk * tm,), jnp.int32).at[dest].set(jnp.arange(n_assign, dtype=jnp.int32) // TOP_K)
    row_w = jnp.zeros((nblk * tm,), F32).at[dest].set(wts)
    n_valid = (ends[-1] // tm).astype(jnp.int32)
    blk_start = jnp.arange(nblk, dtype=jnp.int32) * tm
    be = jnp.sum((ends[None, :] <= blk_start[:, None]).astype(jnp.int32), axis=1)
    be = jnp.minimum(be, N_EXPERTS - 1)
    last_valid = be[jnp.maximum(n_valid - 1, 0)]
    be = jnp.where(jnp.arange(nblk) < n_valid, be, last_valid).astype(jnp.int32)
    return src_tok, row_w.reshape(-1, 1), be, n_valid.reshape(1), dest.reshape(t, TOP_K)


def _combine_kernel(h_ref, y0_ref, y1_ref, g_ref, o_ref):
    o_ref[...] = h_ref[...] + g_ref[...] * (y0_ref[...] + y1_ref[...])


def _combine_residual(h, y0, y1, gate, tiles_per_seq):
    t, d = h.shape
    seg = _seg_index(tiles_per_seq)
    tile = pl.BlockSpec((ROW_TILE, d), lambda i: (i, 0))
    return pl.pallas_call(
        _combine_kernel,
        grid=(t // ROW_TILE,),
        in_specs=[tile, tile, tile, pl.BlockSpec((None, 1, d), lambda i: (seg(i), 0, 0))],
        out_specs=tile,
        out_shape=jax.ShapeDtypeStruct((t, d), F32),
        compiler_params=_params(1),
        name="moe_combine_residual",
    )(h, y0, y1, gate)


def _rope_table(seq):
    rows = seq // GRID_W
    row = jnp.repeat(jnp.arange(rows, dtype=F32), GRID_W)
    col = jnp.tile(jnp.arange(GRID_W, dtype=F32), rows)
    n_freq = HEAD_DIM // 4
    inv_freq = ROPE_BASE ** (-jnp.arange(n_freq, dtype=F32) / n_freq)
    ang = jnp.concatenate([row[:, None] * inv_freq, col[:, None] * inv_freq], axis=-1)
    cos, sin = jnp.cos(ang), jnp.sin(ang)
    return jnp.concatenate([cos, cos, -sin, sin], axis=-1)


def kernel(x, c, ctx, c_ctx, ada_w, ada_b, norm1_w, norm2_w, ev_w_in, ev_sink, ev_conv_w, ev_w_out, ffn_w_gate, ffn_w_up, ffn_w_down, od_w_in, od_gate_b, od_mnorm_w, od_dw_w, od_dw_b, od_ln_w, od_ln_b, od_w_out, moe_router_w, moe_router_b, moe_w_gate, moe_w_up, moe_w_down, final_w):
    batch, seq, d = x.shape
    ctx_len = ctx.shape[1]
    depth = ada_w.shape[0]
    assert batch * ctx_len == ROW_TILE and seq % ROW_TILE == 0 and ctx_len % CONV_TILE == 0
    assert 1 + batch <= 8
    tps = seq // ROW_TILE
    n_ctx_rows = batch * ctx_len

    cs = _rope_table(seq)
    cvec = jnp.zeros((8, d), F32).at[0].set(c_ctx).at[1:1 + batch].set(c)
    mod_table = _ada_table(cvec, ada_w, ada_b)
    h = jnp.concatenate([ctx.reshape(n_ctx_rows, d), x.reshape(batch * seq, d)], axis=0)

    for layer in range(depth):
        j = layer // 2
        mods = mod_table[layer, :1 + batch].reshape(1 + batch, 6, 1, d)
        mod = [mods[:, i] for i in range(6)]
        a1 = _norm_modulate(h, norm1_w[layer], mod[0], mod[1], tps)
        if layer % 2 == 0:
            z = _matmul(a1, ev_w_in, j, ev_w_in.shape[2])
            o_a = _window_attention(z, cs, ev_sink[j], batch, ctx_len, seq)
            o_b = _short_conv(z, ev_conv_w, j, batch, ctx_len, seq)
            h = _out_proj_residual(o_a, o_b, ev_w_out, j, h, mod[2], tps)
            a2 = _norm_modulate(h, norm2_w[layer], mod[3], mod[4], tps)
            g = _ffn_up(a2, ffn_w_gate, ffn_w_up, j)
            h = _ffn_down_residual(g, ffn_w_down, j, h, mod[5], tps)
        else:
            qkv_w = M_HEADS * (2 * M_QK + M_V)
            n_gate = 4 * M_HEADS
            zq = _matmul(a1, od_w_in, j, qkv_w)
            w_gate_cols = jnp.pad(od_w_in[j, :, qkv_w:qkv_w + n_gate], ((0, 0), (0, 128 - n_gate)))[None]
            gates = _matmul(a1, w_gate_cols, 0, 128, tn=128)[:, :n_gate]
            w_rest = od_w_in[j, :, qkv_w + n_gate:][None]
            z_rest = _matmul(a1, w_rest, 0, w_rest.shape[2])
            h_dir = _mlstm(zq, gates, od_gate_b[j], batch, ctx_len, seq)
            m_out = _mlstm_head_out(h_dir, z_rest, od_mnorm_w[j])
            u_out = _conformer_conv(z_rest, 1, 2, od_dw_w, od_dw_b, od_ln_w, od_ln_b, j, batch, ctx_len, seq)
            h = _out_proj_residual(m_out, u_out, od_w_out, j, h, mod[2], tps)
            a2, route = _norm_modulate_route(h, norm2_w[layer], mod[3], mod[4], moe_router_w[j],
                                             moe_router_b[j], tps)
            src_tok, row_w, block_expert, n_valid, pos = _routing_plan(route)
            x_sorted = jnp.take(a2, src_tok, axis=0)
            y = _moe_experts(x_sorted, row_w, block_expert, n_valid, moe_w_gate, moe_w_up, moe_w_down, j)
            h = _combine_residual(h, jnp.take(y, pos[:, 0], axis=0), jnp.take(y, pos[:, 1], axis=0), mod[5], tps)

    zero_mod = jnp.zeros((1 + batch, 1, d), F32)
    out = _norm_modulate(h, final_w, zero_mod, zero_mod, tps, out_dtype=F32,
                         row_offset_tiles=n_ctx_rows // ROW_TILE, n_rows=batch * seq)
    return out.reshape(batch, seq, d)
```

```python
import functools

import jax
import jax.numpy as jnp
from jax import lax
from jax.experimental import pallas as pl
from jax.experimental.pallas import tpu as pltpu

F32 = jnp.float32
BF16 = jnp.bfloat16

NORM_EPS = 1e-6
ROPE_BASE = 10000.0
GRID_W = 64
HEAD_DIM = 128
Q_HEADS = 8
KV_HEADS = 2
ATT_BLOCK = 128
M_HEADS = 4
M_QK = 128
M_V = 256
M_CHUNK = 128
CONV_SHORT = 3
CONV_LONG = 31
N_EXPERTS = 8
TOP_K = 2

ROW_TILE = 512
CONV_TILE = 256
HALO = 16
NEG = -1e30
V7X_VMEM_LIMIT = 56 * 1024 * 1024


def _params(n_axes):
    return pltpu.CompilerParams(dimension_semantics=("arbitrary",) * n_axes, vmem_limit_bytes=V7X_VMEM_LIMIT)


def _sigmoid(x):
    return 1.0 / (1.0 + jnp.exp(-x))


def _silu(x):
    return x * _sigmoid(x)


def _ada_kernel(c_ref, w_ref, b_ref, o_ref):
    s = _silu(c_ref[...]).astype(BF16)
    o_ref[...] = jnp.dot(s, w_ref[...].astype(BF16), preferred_element_type=F32) + b_ref[...]


def _ada_table(cvec, ada_w, ada_b, tn=1024):
    depth, d, n = ada_w.shape
    rows = cvec.shape[0]
    return pl.pallas_call(
        _ada_kernel,
        grid=(depth, n // tn),
        in_specs=[pl.BlockSpec((rows, d), lambda l, j: (0, 0)),
                  pl.BlockSpec((None, d, tn), lambda l, j: (l, 0, j)),
                  pl.BlockSpec((None, 1, tn), lambda l, j: (l, 0, j))],
        out_specs=pl.BlockSpec((None, rows, tn), lambda l, j: (l, 0, j)),
        out_shape=jax.ShapeDtypeStruct((depth, rows, n), F32),
        compiler_params=_params(2),
        name="ada_table",
    )(cvec, ada_w, ada_b.reshape(depth, 1, n))


def _normmod(x, w, shift, scale):
    ms = jnp.mean(x * x, axis=-1, keepdims=True)
    y = x * lax.rsqrt(ms + NORM_EPS) * w
    return y * (1.0 + scale) + shift


def _normmod_kernel(h_ref, w_ref, sh_ref, sc_ref, o_ref):
    o_ref[...] = _normmod(h_ref[...], w_ref[...], sh_ref[...], sc_ref[...]).astype(o_ref.dtype)


def _top2_route(a, rw_ref, rb_ref):
    logits = jnp.dot(a, rw_ref[...], precision=lax.Precision.HIGHEST, preferred_element_type=F32) + rb_ref[...]
    lane = lax.broadcasted_iota(jnp.int32, logits.shape, 1)
    logits = jnp.where(lane < N_EXPERTS, logits, NEG)
    big = jnp.int32(1 << 20)
    m1 = jnp.max(logits, axis=-1, keepdims=True)
    i1 = jnp.min(jnp.where(logits == m1, lane, big), axis=-1, keepdims=True)
    rest = jnp.where(lane == i1, NEG, logits)
    m2 = jnp.max(rest, axis=-1, keepdims=True)
    i2 = jnp.min(jnp.where(rest == m2, lane, big), axis=-1, keepdims=True)
    e2 = jnp.exp(m2 - m1)
    w1 = 1.0 / (1.0 + e2)
    w2 = e2 / (1.0 + e2)
    r = jnp.where(lane == 0, i1.astype(F32), 0.0)
    r = jnp.where(lane == 1, i2.astype(F32), r)
    r = jnp.where(lane == 2, w1, r)
    r = jnp.where(lane == 3, w2, r)
    return r


def _seg_index(tiles_per_seq):
    return lambda i: (i + tiles_per_seq - 1) // tiles_per_seq


def _norm_modulate(h, w, shift, scale, tiles_per_seq, out_dtype=BF16, row_offset_tiles=0, n_rows=None):
    t, d = h.shape
    n_rows = t if n_rows is None else n_rows
    seg = _seg_index(tiles_per_seq)
    off = row_offset_tiles
    return pl.pallas_call(
        _normmod_kernel,
        grid=(n_rows // ROW_TILE,),
        in_specs=[pl.BlockSpec((ROW_TILE, d), lambda i: (i + off, 0)),
                  pl.BlockSpec((1, d), lambda i: (0, 0)),
                  pl.BlockSpec((None, 1, d), lambda i: (seg(i + off), 0, 0)),
                  pl.BlockSpec((None, 1, d), lambda i: (seg(i + off), 0, 0))],
        out_specs=pl.BlockSpec((ROW_TILE, d), lambda i: (i, 0)),
        out_shape=jax.ShapeDtypeStruct((n_rows, d), out_dtype),
        compiler_params=_params(1),
        name="norm_modulate",
    )(h, w.reshape(1, d), shift, scale)


def _mm_kernel(x_ref, w_ref, o_ref, wb_ref):
    @pl.when(pl.program_id(1) == 0)
    def _():
        wb_ref[...] = w_ref[...].astype(BF16)

    o_ref[...] = jnp.dot(x_ref[...], wb_ref[...], preferred_element_type=F32).astype(o_ref.dtype)


def _matmul(x, w3, layer, n_cols, tn=512, out_dtype=F32):
    t, k = x.shape
    return pl.pallas_call(
        _mm_kernel,
        grid=(n_cols // tn, t // ROW_TILE),
        in_specs=[pl.BlockSpec((ROW_TILE, k), lambda j, i: (i, 0)),
                  pl.BlockSpec((None, k, tn), lambda j, i: (layer, 0, j))],
        out_specs=pl.BlockSpec((ROW_TILE, tn), lambda j, i: (i, j)),
        out_shape=jax.ShapeDtypeStruct((t, n_cols), out_dtype),
        scratch_shapes=[pltpu.VMEM((k, tn), BF16)],
        compiler_params=_params(2),
        name="matmul",
    )(x, w3)


def _out_proj_kernel(*refs, route):
    if route:
        x1_ref, x2_ref, w1_ref, w2_ref, h_ref, g_ref, nw_ref, sh_ref, sc_ref, rw_ref, rb_ref, ho_ref, a_ref, r_ref = refs
    else:
        x1_ref, x2_ref, w1_ref, w2_ref, h_ref, g_ref, nw_ref, sh_ref, sc_ref, ho_ref, a_ref = refs
    y = jnp.dot(x1_ref[...], w1_ref[...], preferred_element_type=F32)
    y = y + jnp.dot(x2_ref[...], w2_ref[...], preferred_element_type=F32)
    hn = h_ref[...] + g_ref[...] * y
    ho_ref[...] = hn
    a = _normmod(hn, nw_ref[...], sh_ref[...], sc_ref[...])
    a_ref[...] = a.astype(a_ref.dtype)
    if route:
        r_ref[...] = _top2_route(a, rw_ref, rb_ref)


def _out_proj_norm(x1, x2, w3b, layer, h, gate, norm_w, shift, scale, tiles_per_seq, router=None):
    t, k1 = x1.shape
    k2 = x2.shape[1]
    assert k1 == k2
    d = w3b.shape[2]
    seg = _seg_index(tiles_per_seq)
    row = lambda w: pl.BlockSpec((ROW_TILE, w), lambda i: (i, 0))
    modv = pl.BlockSpec((None, 1, d), lambda i: (seg(i), 0, 0))
    in_specs = [row(k1), row(k2),
                pl.BlockSpec((None, k1, d), lambda i: (layer, 0, 0), pipeline_mode=pl.Buffered(1)),
                pl.BlockSpec((None, k2, d), lambda i: (layer, 1, 0), pipeline_mode=pl.Buffered(1)),
                row(d), modv, pl.BlockSpec((1, d), lambda i: (0, 0)), modv, modv]
    args = [x1, x2, w3b, w3b, h, gate, norm_w.reshape(1, d), shift, scale]
    out_specs = [row(d), row(d)]
    out_shape = [jax.ShapeDtypeStruct((t, d), F32), jax.ShapeDtypeStruct((t, d), BF16)]
    if router is not None:
        router_w, router_b = router
        in_specs += [pl.BlockSpec((d, 128), lambda i: (0, 0)), pl.BlockSpec((1, 128), lambda i: (0, 0))]
        args += [jnp.pad(router_w, ((0, 0), (0, 128 - N_EXPERTS))),
                 jnp.pad(router_b, (0, 128 - N_EXPERTS)).reshape(1, 128)]
        out_specs.append(row(128))
        out_shape.append(jax.ShapeDtypeStruct((t, 128), F32))
    return pl.pallas_call(
        functools.partial(_out_proj_kernel, route=router is not None),
        grid=(t // ROW_TILE,),
        in_specs=in_specs,
        out_specs=out_specs,
        out_shape=out_shape,
        compiler_params=_params(1),
        name="out_proj_norm",
    )(*args)


def _ffn_up_kernel(x_ref, wg_ref, wu_ref, o_ref, wgb_ref, wub_ref):
    @pl.when(pl.program_id(1) == 0)
    def _():
        wgb_ref[...] = wg_ref[...].astype(BF16)
        wub_ref[...] = wu_ref[...].astype(BF16)

    x = x_ref[...]
    g = jnp.dot(x, wgb_ref[...], preferred_element_type=F32)
    u = jnp.dot(x, wub_ref[...], preferred_element_type=F32)
    o_ref[...] = (_silu(g) * u).astype(o_ref.dtype)


def _ffn_up(x, w_gate, w_up, layer, tf=512):
    t, k = x.shape
    f = w_gate.shape[2]
    return pl.pallas_call(
        _ffn_up_kernel,
        grid=(f // tf, t // ROW_TILE),
        in_specs=[pl.BlockSpec((ROW_TILE, k), lambda j, i: (i, 0)),
                  pl.BlockSpec((None, k, tf), lambda j, i: (layer, 0, j)),
                  pl.BlockSpec((None, k, tf), lambda j, i: (layer, 0, j))],
        out_specs=pl.BlockSpec((ROW_TILE, tf), lambda j, i: (i, j)),
        out_shape=jax.ShapeDtypeStruct((t, f), BF16),
        scratch_shapes=[pltpu.VMEM((k, tf), BF16), pltpu.VMEM((k, tf), BF16)],
        compiler_params=_params(2),
        name="ffn_up",
    )(x, w_gate, w_up)


def _mm_resid_kernel(x_ref, w_ref, h_ref, g_ref, o_ref, wb_ref):
    @pl.when(pl.program_id(1) == 0)
    def _():
        wb_ref[...] = w_ref[...].astype(BF16)

    y = jnp.dot(x_ref[...], wb_ref[...], preferred_element_type=F32)
    o_ref[...] = h_ref[...] + g_ref[...] * y


def _ffn_down_residual(x, w3, layer, h, gate, tiles_per_seq, tn=512):
    t, k = x.shape
    n = w3.shape[2]
    seg = _seg_index(tiles_per_seq)
    return pl.pallas_call(
        _mm_resid_kernel,
        grid=(n // tn, t // ROW_TILE),
        in_specs=[pl.BlockSpec((ROW_TILE, k), lambda j, i: (i, 0)),
                  pl.BlockSpec((None, k, tn), lambda j, i: (layer, 0, j)),
                  pl.BlockSpec((ROW_TILE, tn), lambda j, i: (i, j)),
                  pl.BlockSpec((None, 1, tn), lambda j, i: (seg(i), 0, j))],
        out_specs=pl.BlockSpec((ROW_TILE, tn), lambda j, i: (i, j)),
        out_shape=jax.ShapeDtypeStruct((t, n), F32),
        scratch_shapes=[pltpu.VMEM((k, tn), BF16)],
        compiler_params=_params(2),
        name="ffn_down_residual",
    )(x, w3, h, gate)


def _rope(x, cs):
    return x * cs[:, :HEAD_DIM] + pltpu.roll(x, HEAD_DIM // 2, axis=1) * cs[:, HEAD_DIM:]


def _attend(q_all, k_parts, v_parts, sink_ref, o_ref, mask_fn):
    group = Q_HEADS // KV_HEADS
    scale = HEAD_DIM ** -0.5
    for g in range(KV_HEADS):
        qg = jnp.concatenate([q_all[:, (g * group + r) * HEAD_DIM:(g * group + r + 1) * HEAD_DIM]
                              for r in range(group)], axis=0).astype(BF16)
        kg = jnp.concatenate([kp[:, g * HEAD_DIM:(g + 1) * HEAD_DIM] for kp in k_parts], axis=0).astype(BF16)
        vg = jnp.concatenate([vp[:, g * HEAD_DIM:(g + 1) * HEAD_DIM] for vp in v_parts], axis=0).astype(BF16)
        s = lax.dot_general(qg, kg, (((1,), (1,)), ((), ())), preferred_element_type=F32) * scale
        if mask_fn is not None:
            s = jnp.where(mask_fn(s.shape), s, NEG)
        sink = sink_ref[g]
        m = jnp.maximum(jnp.max(s, axis=-1, keepdims=True), sink)
        p = jnp.exp(s - m)
        denom = jnp.sum(p, axis=-1, keepdims=True) + jnp.exp(sink - m)
        o = jnp.dot(p.astype(BF16), vg, preferred_element_type=F32) / denom
        for r in range(group):
            hq = g * group + r
            o_ref[:, hq * HEAD_DIM:(hq + 1) * HEAD_DIM] = o[r * ATT_BLOCK:(r + 1) * ATT_BLOCK].astype(o_ref.dtype)


def _win_attn_kernel(q_ref, kp_ref, kc_ref, kn_ref, kx_ref, vp_ref, vc_ref, vn_ref, vx_ref,
                     csp_ref, csc_ref, csn_ref, sink_ref, o_ref, *, n_blocks):
    n = pl.program_id(1)
    csc = csc_ref[...]
    q_all = jnp.concatenate([_rope(q_ref[:, h * HEAD_DIM:(h + 1) * HEAD_DIM], csc) for h in range(Q_HEADS)], axis=1)

    def rope_kv(k_ref, cs):
        return jnp.concatenate([_rope(k_ref[:, g * HEAD_DIM:(g + 1) * HEAD_DIM], cs) for g in range(KV_HEADS)], axis=1)

    k_parts = [rope_kv(kp_ref, csp_ref[...]), rope_kv(kc_ref, csc), rope_kv(kn_ref, csn_ref[...]), kx_ref[...]]
    v_parts = [vp_ref[...], vc_ref[...], vn_ref[...], vx_ref[...]]

    def mask_fn(shape):
        row = lax.broadcasted_iota(jnp.int32, shape, 0) & (ATT_BLOCK - 1)
        col = lax.broadcasted_iota(jnp.int32, shape, 1)
        band = (col >= row) & (col <= row + 2 * ATT_BLOCK)
        ok_prev = (col >= ATT_BLOCK) | (n > 0)
        ok_next = (col < 2 * ATT_BLOCK) | (n < n_blocks - 1)
        return (band & ok_prev & ok_next) | (col >= 3 * ATT_BLOCK)

    _attend(q_all, k_parts, v_parts, sink_ref, o_ref, mask_fn)


def _ctx_attn_kernel(q_ref, kx_ref, vx_ref, sink_ref, latent_rows_ref, o_ref):
    del latent_rows_ref
    _attend(q_ref[...], [kx_ref[...]], [vx_ref[...]], sink_ref, o_ref, None)


def _sink_rows(sink):
    group = Q_HEADS // KV_HEADS
    return jnp.repeat(sink.astype(F32).reshape(KV_HEADS, group), ATT_BLOCK, axis=1).reshape(
        KV_HEADS, group * ATT_BLOCK, 1)


def _window_attention(z, cs, sink, batch, ctx_len, seq):
    t = z.shape[0]
    nb = seq // ATT_BLOCK
    cb = ctx_len // ATT_BLOCK
    base = batch * cb
    qw = Q_HEADS * HEAD_DIM
    kw = KV_HEADS * HEAD_DIM
    kcol = qw // kw
    vcol = kcol + 1
    sink_rows = _sink_rows(sink)

    def lat(b, n):
        return base + b * nb + n

    def prev(n):
        return jnp.maximum(n - 1, 0)

    def nxt(n):
        return jnp.minimum(n + 1, nb - 1)

    o_lat = pl.pallas_call(
        functools.partial(_win_attn_kernel, n_blocks=nb),
        grid=(batch, nb),
        in_specs=[pl.BlockSpec((ATT_BLOCK, qw), lambda b, n: (lat(b, n), 0)),
                  pl.BlockSpec((ATT_BLOCK, kw), lambda b, n: (lat(b, prev(n)), kcol)),
                  pl.BlockSpec((ATT_BLOCK, kw), lambda b, n: (lat(b, n), kcol)),
                  pl.BlockSpec((ATT_BLOCK, kw), lambda b, n: (lat(b, nxt(n)), kcol)),
                  pl.BlockSpec((ctx_len, kw), lambda b, n: (b, kcol)),
                  pl.BlockSpec((ATT_BLOCK, kw), lambda b, n: (lat(b, prev(n)), vcol)),
                  pl.BlockSpec((ATT_BLOCK, kw), lambda b, n: (lat(b, n), vcol)),
                  pl.BlockSpec((ATT_BLOCK, kw), lambda b, n: (lat(b, nxt(n)), vcol)),
                  pl.BlockSpec((ctx_len, kw), lambda b, n: (b, vcol)),
                  pl.BlockSpec((ATT_BLOCK, 2 * HEAD_DIM), lambda b, n: (prev(n), 0)),
                  pl.BlockSpec((ATT_BLOCK, 2 * HEAD_DIM), lambda b, n: (n, 0)),
                  pl.BlockSpec((ATT_BLOCK, 2 * HEAD_DIM), lambda b, n: (nxt(n), 0)),
                  pl.BlockSpec(sink_rows.shape, lambda b, n: (0, 0, 0))],
        out_specs=pl.BlockSpec((ATT_BLOCK, qw), lambda b, n: (lat(b, n), 0)),
        out_shape=jax.ShapeDtypeStruct((t, qw), BF16),
        compiler_params=_params(2),
        name="window_attention",
    )(z, z, z, z, z, z, z, z, z, cs, cs, cs, sink_rows)

    return pl.pallas_call(
        _ctx_attn_kernel,
        grid=(batch, cb),
        in_specs=[pl.BlockSpec((ATT_BLOCK, qw), lambda b, n: (b * cb + n, 0)),
                  pl.BlockSpec((ctx_len, kw), lambda b, n: (b, kcol)),
                  pl.BlockSpec((ctx_len, kw), lambda b, n: (b, vcol)),
                  pl.BlockSpec(sink_rows.shape, lambda b, n: (0, 0, 0)),
                  pl.BlockSpec(memory_space=pl.ANY)],
        out_specs=pl.BlockSpec((ATT_BLOCK, qw), lambda b, n: (b * cb + n, 0)),
        out_shape=jax.ShapeDtypeStruct((t, qw), BF16),
        input_output_aliases={4: 0},
        compiler_params=_params(2),
        name="context_attention",
    )(z, z, z, sink_rows, o_lat)


def _seq_edges(t, batch, ctx_tiles, lat_tiles):
    n_ctx = batch * ctx_tiles
    u = t - n_ctx
    is_ctx = t < n_ctx
    first = jnp.where(is_ctx, lax.rem(t, ctx_tiles) == 0, lax.rem(u, lat_tiles) == 0)
    last = jnp.where(is_ctx, lax.rem(t, ctx_tiles) == ctx_tiles - 1, lax.rem(u, lat_tiles) == lat_tiles - 1)
    return first, last


def _fill_ext(ext_ref, cur, prev, nxt, first, last):
    ext_ref[HALO:HALO + CONV_TILE, :] = cur
    ext_ref[0:HALO, :] = jnp.where(first, 0.0, prev)
    ext_ref[HALO + CONV_TILE:, :] = jnp.where(last, 0.0, nxt)


def _dwconv_rows(ext_ref, w_ref, taps, r0, n_rows):
    pad = taps // 2
    acc = None
    for k in range(taps):
        start = HALO + r0 + k - pad
        term = w_ref[k:k + 1, :] * ext_ref[start:start + n_rows, :]
        acc = term if acc is None else acc + term
    return acc


CONV_ROWS = 32


def _short_conv_kernel(bg_ref, cg_ref, xb_ref, cgp_ref, xbp_ref, cgn_ref, xbn_ref, w_ref, o_ref, ext_ref,
                       *, batch, ctx_tiles, lat_tiles):
    first, last = _seq_edges(pl.program_id(0), batch, ctx_tiles, lat_tiles)
    _fill_ext(ext_ref, cg_ref[...] * xb_ref[...], cgp_ref[...] * xbp_ref[...], cgn_ref[...] * xbn_ref[...],
              first, last)
    for r0 in range(0, CONV_TILE, CONV_ROWS):
        conv = _dwconv_rows(ext_ref, w_ref, CONV_SHORT, r0, CONV_ROWS)
        o_ref[r0:r0 + CONV_ROWS, :] = (bg_ref[r0:r0 + CONV_ROWS, :] * conv).astype(o_ref.dtype)


def _halo_specs(width, col, halo_per_tile, n_halo_blocks):
    prev = pl.BlockSpec((HALO, width), lambda t, c: (jnp.maximum(t * halo_per_tile - 1, 0), col(c)))
    nxt = pl.BlockSpec((HALO, width), lambda t, c: (jnp.minimum((t + 1) * halo_per_tile, n_halo_blocks - 1), col(c)))
    return prev, nxt


def _short_conv(z, conv_w3, layer, batch, ctx_len, seq):
    t = z.shape[0]
    cw = 512
    n_c = 1024 // cw
    bg0, cg0, xb0 = 1536 // cw, 2560 // cw, 3584 // cw
    hpt = CONV_TILE // HALO
    nh = t // HALO
    cgp, cgn = _halo_specs(cw, lambda c: cg0 + c, hpt, nh)
    xbp, xbn = _halo_specs(cw, lambda c: xb0 + c, hpt, nh)
    kern = functools.partial(_short_conv_kernel, batch=batch, ctx_tiles=ctx_len // CONV_TILE,
                             lat_tiles=seq // CONV_TILE)
    return pl.pallas_call(
        kern,
        grid=(t // CONV_TILE, n_c),
        in_specs=[pl.BlockSpec((CONV_TILE, cw), lambda t_, c: (t_, bg0 + c)),
                  pl.BlockSpec((CONV_TILE, cw), lambda t_, c: (t_, cg0 + c)),
                  pl.BlockSpec((CONV_TILE, cw), lambda t_, c: (t_, xb0 + c)),
                  cgp, xbp, cgn, xbn,
                  pl.BlockSpec((None, CONV_SHORT, cw), lambda t_, c: (layer, 0, c))],
        out_specs=pl.BlockSpec((CONV_TILE, cw), lambda t_, c: (t_, c)),
        out_shape=jax.ShapeDtypeStruct((t, 1024), BF16),
        scratch_shapes=[pltpu.VMEM((CONV_TILE + 2 * HALO, cw), F32)],
        compiler_params=_params(2),
        name="short_conv",
    )(z, z, z, z, z, z, z, conv_w3)


SUBLANES = 8
SHIFT_ROWS = CONV_TILE + 2 * HALO - SUBLANES


def _dwconv_rows_aligned(ext_ref, sh_ref, w8_ref, taps, r0, n_rows):
    pad = taps // 2
    width = ext_ref.shape[1]
    acc = None
    for k in range(taps):
        start = HALO + r0 + k - pad
        b = start % SUBLANES
        a8 = start - b
        src = ext_ref[a8:a8 + n_rows, :] if b == 0 else sh_ref[b - 1, a8:a8 + n_rows, :]
        term = (src.reshape(n_rows // SUBLANES, SUBLANES, width) * w8_ref[k][None]).reshape(n_rows, width)
        acc = term if acc is None else acc + term
    return acc


def _conformer_kernel(a_ref, g_ref, ap_ref, gp_ref, an_ref, gn_ref, w8_ref, b_ref, lw_ref, lb_ref, o_ref,
                      ext_ref, sh_ref, *, batch, ctx_tiles, lat_tiles):
    first, last = _seq_edges(pl.program_id(0), batch, ctx_tiles, lat_tiles)
    _fill_ext(ext_ref, a_ref[...] * _sigmoid(g_ref[...]), ap_ref[...] * _sigmoid(gp_ref[...]),
              an_ref[...] * _sigmoid(gn_ref[...]), first, last)
    for b in range(1, SUBLANES):
        sh_ref[b - 1] = ext_ref[b:b + SHIFT_ROWS, :]
    for r0 in range(0, CONV_TILE, CONV_ROWS):
        u = _dwconv_rows_aligned(ext_ref, sh_ref, w8_ref, CONV_LONG, r0, CONV_ROWS) + b_ref[...]
        mu = jnp.mean(u, axis=-1, keepdims=True)
        uc = u - mu
        var = jnp.mean(uc * uc, axis=-1, keepdims=True)
        y = uc * lax.rsqrt(var + NORM_EPS) * lw_ref[...] + lb_ref[...]
        o_ref[r0:r0 + CONV_ROWS, :] = _silu(y).astype(o_ref.dtype)


def _conformer_conv(z, a_col, g_col, dw_w3, dw_b, ln_w, ln_b, layer, batch, ctx_len, seq):
    t = z.shape[0]
    cw = 1024
    hpt = CONV_TILE // HALO
    nh = t // HALO
    ap, an = _halo_specs(cw, lambda c: a_col, hpt, nh)
    gp, gn = _halo_specs(cw, lambda c: g_col, hpt, nh)
    kern = functools.partial(_conformer_kernel, batch=batch, ctx_tiles=ctx_len // CONV_TILE,
                             lat_tiles=seq // CONV_TILE)
    vec = pl.BlockSpec((None, 1, cw), lambda t_, c: (layer, 0, 0))
    n_layers = dw_b.shape[0]
    return pl.pallas_call(
        kern,
        grid=(t // CONV_TILE, 1),
        in_specs=[pl.BlockSpec((CONV_TILE, cw), lambda t_, c: (t_, a_col)),
                  pl.BlockSpec((CONV_TILE, cw), lambda t_, c: (t_, g_col)),
                  ap, gp, an, gn,
                  pl.BlockSpec((None, CONV_LONG, SUBLANES, cw), lambda t_, c: (layer, 0, 0, 0)),
                  vec, vec, vec],
        out_specs=pl.BlockSpec((CONV_TILE, cw), lambda t_, c: (t_, 0)),
        out_shape=jax.ShapeDtypeStruct((t, cw), BF16),
        scratch_shapes=[pltpu.VMEM((CONV_TILE + 2 * HALO, cw), F32),
                        pltpu.VMEM((SUBLANES - 1, SHIFT_ROWS, cw), F32)],
        compiler_params=_params(2),
        name="conformer_conv",
    )(z, z, z, z, z, z, jnp.broadcast_to(dw_w3[:, :, None, :], (n_layers, CONV_LONG, SUBLANES, cw)),
      dw_b.reshape(n_layers, 1, cw), ln_w.reshape(n_layers, 1, cw), ln_b.reshape(n_layers, 1, cw))


def _log_sigmoid(x):
    return jnp.minimum(x, 0.0) - jnp.log(1.0 + jnp.exp(-jnp.abs(x)))


def _mlstm_kernel(qf_ref, kf_ref, vf_ref, gcf_ref, grf_ref, qb_ref, kb_ref, vb_ref, gcb_ref, grb_ref,
                  bc_ref, br_ref, of_ref, ob_ref, ct_ref, n_ref, m_ref):
    @pl.when(pl.program_id(1) == 0)
    def _():
        ct_ref[...] = jnp.zeros_like(ct_ref)
        n_ref[...] = jnp.zeros_like(n_ref)
        m_ref[...] = jnp.zeros_like(m_ref)

    ln = M_CHUNK
    row = lax.broadcasted_iota(jnp.int32, (ln, ln), 0)
    col = lax.broadcasted_iota(jnp.int32, (ln, ln), 1)
    lower = col <= row
    upper = col >= row
    _mlstm_direction(qf_ref, kf_ref, vf_ref, gcf_ref[...] + bc_ref[0], grf_ref[...] + br_ref[0], lower, upper,
                     of_ref, ct_ref, n_ref, m_ref, 0)
    _mlstm_direction(qb_ref, kb_ref, vb_ref, gcb_ref[...] + bc_ref[1], grb_ref[...] + br_ref[1], upper, lower,
                     ob_ref, ct_ref, n_ref, m_ref, M_HEADS)


def _mlstm_direction(q_ref, k_ref, v_ref, gc, gr, mask, mask_t, o_ref, ct_ref, n_ref, m_ref, state0):
    for hd in range(M_HEADS):
        h = hd
        st = state0 + hd
        ig_c = gc[:, h:h + 1]
        lf_c = _log_sigmoid(gc[:, M_HEADS + h:M_HEADS + h + 1])
        ig_r = gr[h:h + 1, :]
        lf_r = _log_sigmoid(gr[M_HEADS + h:M_HEADS + h + 1, :])
        q = q_ref[:, h * M_QK:(h + 1) * M_QK] * (M_QK ** -0.5)
        k = k_ref[:, h * M_QK:(h + 1) * M_QK]
        v = v_ref[:, h * M_V:(h + 1) * M_V]
        qb = q.astype(BF16)
        kb = k.astype(BF16)
        cum_c = jnp.sum(jnp.where(mask, lf_r, 0.0), axis=1, keepdims=True)
        cum_r = jnp.sum(jnp.where(mask_t, lf_c, 0.0), axis=0, keepdims=True)
        total = jnp.sum(lf_r, axis=1, keepdims=True)
        m_s = m_ref[st]
        dlog = jnp.where(mask, cum_c - cum_r + ig_r, NEG)
        inter = cum_c + m_s
        m_t = jnp.maximum(inter, jnp.max(dlog, axis=1, keepdims=True))
        w_inter = jnp.exp(inter - m_t)
        s = lax.dot_general(qb, kb, (((1,), (1,)), ((), ())), preferred_element_type=F32) * jnp.exp(dlog - m_t)
        ct = ct_ref[st]
        n_v = n_ref[st]
        num = (w_inter * jnp.dot(qb, ct.astype(BF16), preferred_element_type=F32)
               + jnp.dot(s.astype(BF16), v.astype(BF16), preferred_element_type=F32))
        den = w_inter * jnp.sum(q * n_v, axis=1, keepdims=True) + jnp.sum(s, axis=1, keepdims=True)
        o_ref[:, h * M_V:(h + 1) * M_V] = num / jnp.maximum(jnp.abs(den), jnp.exp(-m_t))
        gl = total - cum_c + ig_c
        m_new = jnp.maximum(total + m_s, jnp.max(gl, axis=0, keepdims=True))
        wg = jnp.exp(gl - m_new)
        decay = jnp.exp(total + m_s - m_new)
        ct_ref[st] = decay * ct + jnp.dot(k.T.astype(BF16), (wg * v).astype(BF16), preferred_element_type=F32)
        n_ref[st] = decay * n_v + jnp.sum(wg * k, axis=0, keepdims=True)
        m_ref[st] = m_new


def _mlstm(zq, gates, gate_b, batch, ctx_len, seq):
    t = zq.shape[0]
    cb = ctx_len // M_CHUNK
    nc = seq // M_CHUNK
    base = batch * cb
    steps = cb + nc
    half = 2 * M_HEADS
    g_dir = gates.reshape(t, 2, half).transpose(1, 0, 2)
    g_dir_t = g_dir.transpose(0, 2, 1)
    b_dir = gate_b.astype(F32).reshape(2, 1, half)
    b_dir_t = b_dir.transpose(0, 2, 1)

    def rb_f(b, i):
        return jnp.where(i < cb, b * cb + i, base + b * nc + (i - cb))

    def rb_b(b, i):
        return jnp.where(i < cb, b * cb + (cb - 1 - i), base + b * nc + (nc - 1 - (i - cb)))

    qk_w = M_HEADS * M_QK
    v_w = M_HEADS * M_V

    def chunk_specs(d, rb):
        return [pl.BlockSpec((M_CHUNK, qk_w), lambda b, i: (rb(b, i), 0)),
                pl.BlockSpec((M_CHUNK, qk_w), lambda b, i: (rb(b, i), 1)),
                pl.BlockSpec((M_CHUNK, v_w), lambda b, i: (rb(b, i), 1)),
                pl.BlockSpec((None, M_CHUNK, half), lambda b, i: (d, rb(b, i), 0)),
                pl.BlockSpec((None, half, M_CHUNK), lambda b, i: (d, 0, rb(b, i)))]

    return pl.pallas_call(
        _mlstm_kernel,
        grid=(batch, steps),
        in_specs=chunk_specs(0, rb_f) + chunk_specs(1, rb_b) + [
            pl.BlockSpec((2, 1, half), lambda b, i: (0, 0, 0)),
            pl.BlockSpec((2, half, 1), lambda b, i: (0, 0, 0))],
        out_specs=[pl.BlockSpec((M_CHUNK, v_w), lambda b, i: (rb_f(b, i), 0)),
                   pl.BlockSpec((M_CHUNK, v_w), lambda b, i: (rb_b(b, i), 0))],
        out_shape=[jax.ShapeDtypeStruct((t, v_w), F32), jax.ShapeDtypeStruct((t, v_w), F32)],
        scratch_shapes=[pltpu.VMEM((2 * M_HEADS, M_QK, M_V), F32),
                        pltpu.VMEM((2 * M_HEADS, 1, M_QK), F32),
                        pltpu.VMEM((2 * M_HEADS, 1, 1), F32)],
        compiler_params=_params(2),
        name="mlstm",
    )(zq, zq, zq, g_dir, g_dir_t, zq, zq, zq, g_dir, g_dir_t, b_dir, b_dir_t)


def _head_out_kernel(hf_ref, hb_ref, o_ref, w_ref, out_ref):
    for h in range(M_HEADS):
        sl = slice(h * M_V, (h + 1) * M_V)
        x = hf_ref[:, sl] + hb_ref[:, sl]
        mu = jnp.mean(x, axis=-1, keepdims=True)
        xc = x - mu
        var = jnp.mean(xc * xc, axis=-1, keepdims=True)
        y = xc * lax.rsqrt(var + NORM_EPS) * w_ref[:, sl]
        out_ref[:, sl] = (y * _sigmoid(o_ref[:, sl])).astype(out_ref.dtype)


def _mlstm_head_out(h_f, h_b, z_rest, mnorm_w):
    t, w = h_f.shape
    tile = pl.BlockSpec((ROW_TILE, w), lambda i: (i, 0))
    return pl.pallas_call(
        _head_out_kernel,
        grid=(t // ROW_TILE,),
        in_specs=[tile, tile, tile, pl.BlockSpec((1, w), lambda i: (0, 0))],
        out_specs=tile,
        out_shape=jax.ShapeDtypeStruct((t, w), BF16),
        compiler_params=_params(1),
        name="mlstm_head_out",
    )(h_f, h_b, z_rest, mnorm_w.reshape(1, w))


def _fresh_weights(be_ref, rb):
    return (rb == 0) | (be_ref[rb] != be_ref[jnp.maximum(rb - 1, 0)])


def _moe_up_kernel(be_ref, nv_ref, x_ref, wg_ref, wu_ref, o_ref, wgb_ref, wub_ref):
    rb = pl.program_id(1)
    valid = rb < nv_ref[0]

    @pl.when(valid & _fresh_weights(be_ref, rb))
    def _():
        wgb_ref[...] = wg_ref[...].astype(BF16)
        wub_ref[...] = wu_ref[...].astype(BF16)

    @pl.when(valid)
    def _():
        x = x_ref[...]
        g = jnp.dot(x, wgb_ref[...], preferred_element_type=F32)
        u = jnp.dot(x, wub_ref[...], preferred_element_type=F32)
        o_ref[...] = (_silu(g) * u).astype(o_ref.dtype)

    @pl.when(jnp.logical_not(valid))
    def _():
        o_ref[...] = jnp.zeros_like(o_ref)


def _moe_down_kernel(be_ref, nv_ref, x_ref, w_ref, o_ref, wb_ref):
    rb = pl.program_id(1)
    valid = rb < nv_ref[0]

    @pl.when(valid & _fresh_weights(be_ref, rb))
    def _():
        wb_ref[...] = w_ref[...].astype(BF16)

    @pl.when(valid)
    def _():
        o_ref[...] = jnp.dot(x_ref[...], wb_ref[...], preferred_element_type=F32).astype(o_ref.dtype)

    @pl.when(jnp.logical_not(valid))
    def _():
        o_ref[...] = jnp.zeros_like(o_ref)


def _moe_experts(x_sorted, block_expert, n_valid, w_gate, w_up, w_down, layer, tf=512, tn=512):
    p, d = x_sorted.shape
    f = w_gate.shape[3]
    nblk = p // ROW_TILE

    def row(rb, nv):
        return jnp.minimum(rb, nv[0] - 1)

    g = pl.pallas_call(
        _moe_up_kernel,
        grid_spec=pltpu.PrefetchScalarGridSpec(
            num_scalar_prefetch=2,
            grid=(f // tf, nblk),
            in_specs=[pl.BlockSpec((ROW_TILE, d), lambda j, rb, be, nv: (row(rb, nv), 0)),
                      pl.BlockSpec((None, None, d, tf), lambda j, rb, be, nv: (layer, be[rb], 0, j)),
                      pl.BlockSpec((None, None, d, tf), lambda j, rb, be, nv: (layer, be[rb], 0, j))],
            out_specs=pl.BlockSpec((ROW_TILE, tf), lambda j, rb, be, nv: (rb, j)),
            scratch_shapes=[pltpu.VMEM((d, tf), BF16), pltpu.VMEM((d, tf), BF16)]),
        out_shape=jax.ShapeDtypeStruct((p, f), BF16),
        compiler_params=_params(2),
        name="moe_up",
    )(block_expert, n_valid, x_sorted, w_gate, w_up)
    return pl.pallas_call(
        _moe_down_kernel,
        grid_spec=pltpu.PrefetchScalarGridSpec(
            num_scalar_prefetch=2,
            grid=(d // tn, nblk),
            in_specs=[pl.BlockSpec((ROW_TILE, f), lambda j, rb, be, nv: (row(rb, nv), 0)),
                      pl.BlockSpec((None, None, f, tn), lambda j, rb, be, nv: (layer, be[rb], 0, j))],
            out_specs=pl.BlockSpec((ROW_TILE, tn), lambda j, rb, be, nv: (rb, j)),
            scratch_shapes=[pltpu.VMEM((f, tn), BF16)]),
        out_shape=jax.ShapeDtypeStruct((p, d), BF16),
        compiler_params=_params(2),
        name="moe_down",
    )(block_expert, n_valid, g, w_down)


def _routing_plan(route):
    t = route.shape[0]
    tm = ROW_TILE
    n_assign = TOP_K * t
    nblk = (n_assign + N_EXPERTS * (tm - 1)) // tm
    e = route[:, :TOP_K].astype(jnp.int32).reshape(-1)
    onehot = (e[:, None] == jnp.arange(N_EXPERTS, dtype=jnp.int32)[None, :]).astype(jnp.int32)
    csum = jnp.cumsum(onehot, axis=0)
    rank = jnp.sum((csum - onehot) * onehot, axis=1)
    counts = csum[-1]
    padded = ((counts + tm - 1) // tm) * tm
    ends = jnp.cumsum(padded)
    starts = ends - padded
    dest = starts[e] + rank
    src_tok = jnp.zeros((nblk * tm,), jnp.int32).at[dest].set(
        jnp.arange(n_assign, dtype=jnp.int32) // TOP_K, unique_indices=True, mode="promise_in_bounds")
    n_valid = (ends[-1] // tm).astype(jnp.int32)
    blk_start = jnp.arange(nblk, dtype=jnp.int32) * tm
    be = jnp.sum((ends[None, :] <= blk_start[:, None]).astype(jnp.int32), axis=1)
    be = jnp.minimum(be, N_EXPERTS - 1)
    last_valid = be[jnp.maximum(n_valid - 1, 0)]
    be = jnp.where(jnp.arange(nblk) < n_valid, be, last_valid).astype(jnp.int32)
    return src_tok, be, n_valid.reshape(1), dest.reshape(t, TOP_K)


def _combine_kernel(h_ref, y0_ref, y1_ref, r_ref, g_ref, nw_ref, sh_ref, sc_ref, *out_refs):
    r = r_ref[...]
    moe = r[:, TOP_K:TOP_K + 1] * y0_ref[...].astype(F32) + r[:, TOP_K + 1:TOP_K + 2] * y1_ref[...].astype(F32)
    hn = h_ref[...] + g_ref[...] * moe
    if len(out_refs) == 2:
        out_refs[0][...] = hn
    out_refs[-1][...] = _normmod(hn, nw_ref[...], sh_ref[...], sc_ref[...]).astype(out_refs[-1].dtype)


def _combine_norm(h, y0, y1, route, gate, norm_w, shift, scale, tiles_per_seq, row_offset_tiles, emit_h, a_dtype):
    d = h.shape[1]
    n_rows = y0.shape[0]
    off = row_offset_tiles
    seg = _seg_index(tiles_per_seq)
    shifted = lambda w: pl.BlockSpec((ROW_TILE, w), lambda i: (i + off, 0))
    tile = pl.BlockSpec((ROW_TILE, d), lambda i: (i, 0))
    modv = pl.BlockSpec((None, 1, d), lambda i: (seg(i + off), 0, 0))
    out_specs = [tile, tile] if emit_h else [tile]
    out_shape = [jax.ShapeDtypeStruct((n_rows, d), a_dtype)]
    if emit_h:
        out_shape.insert(0, jax.ShapeDtypeStruct((n_rows, d), F32))
    return pl.pallas_call(
        _combine_kernel,
        grid=(n_rows // ROW_TILE,),
        in_specs=[shifted(d), tile, tile, shifted(128), modv, pl.BlockSpec((1, d), lambda i: (0, 0)), modv, modv],
        out_specs=out_specs,
        out_shape=out_shape,
        compiler_params=_params(1),
        name="moe_combine_norm",
    )(h, y0, y1, route, gate, norm_w.reshape(1, d), shift, scale)


def _rope_table(seq):
    rows = seq // GRID_W
    row = jnp.repeat(jnp.arange(rows, dtype=F32), GRID_W)
    col = jnp.tile(jnp.arange(GRID_W, dtype=F32), rows)
    n_freq = HEAD_DIM // 4
    inv_freq = ROPE_BASE ** (-jnp.arange(n_freq, dtype=F32) / n_freq)
    ang = jnp.concatenate([row[:, None] * inv_freq, col[:, None] * inv_freq], axis=-1)
    cos, sin = jnp.cos(ang), jnp.sin(ang)
    return jnp.concatenate([cos, cos, -sin, sin], axis=-1)


def kernel(x, c, ctx, c_ctx, ada_w, ada_b, norm1_w, norm2_w, ev_w_in, ev_sink, ev_conv_w, ev_w_out, ffn_w_gate, ffn_w_up, ffn_w_down, od_w_in, od_gate_b, od_mnorm_w, od_dw_w, od_dw_b, od_ln_w, od_ln_b, od_w_out, moe_router_w, moe_router_b, moe_w_gate, moe_w_up, moe_w_down, final_w):
    batch, seq, d = x.shape
    ctx_len = ctx.shape[1]
    depth = ada_w.shape[0]
    assert batch * ctx_len == ROW_TILE and seq % ROW_TILE == 0 and ctx_len % CONV_TILE == 0
    assert 1 + batch <= 8
    tps = seq // ROW_TILE
    n_ctx_rows = batch * ctx_len

    cs = _rope_table(seq)
    cvec = jnp.zeros((8, d), F32).at[0].set(c_ctx).at[1:1 + batch].set(c)
    mod_table = _ada_table(cvec, ada_w, ada_b)
    h = jnp.concatenate([ctx.reshape(n_ctx_rows, d), x.reshape(batch * seq, d)], axis=0)

    def mods_of(layer):
        mods = mod_table[layer, :1 + batch].reshape(1 + batch, 6, 1, d)
        return [mods[:, i] for i in range(6)]

    zero_mod = jnp.zeros((1 + batch, 1, d), F32)
    ctx_tiles = n_ctx_rows // ROW_TILE
    ev_w_out_b = ev_w_out.astype(BF16)
    od_w_out_b = od_w_out.astype(BF16)
    def take(rows, idx):
        return rows.at[idx].get(mode="promise_in_bounds")

    out = None
    mod = mods_of(0)
    a1 = _norm_modulate(h, norm1_w[0], mod[0], mod[1], tps)
    for layer in range(depth):
        j = layer // 2
        last = layer == depth - 1
        next_mod = None if last else mods_of(layer + 1)
        if layer % 2 == 0:
            z = _matmul(a1, ev_w_in, j, ev_w_in.shape[2], tn=1536)
            o_a = _window_attention(z, cs, ev_sink[j], batch, ctx_len, seq)
            o_b = _short_conv(z, ev_conv_w, j, batch, ctx_len, seq)
            h, a2 = _out_proj_norm(o_a, o_b, ev_w_out_b, j, h, mod[2], norm2_w[layer], mod[3], mod[4], tps)
            g = _ffn_up(a2, ffn_w_gate, ffn_w_up, j)
            h = _ffn_down_residual(g, ffn_w_down, j, h, mod[5], tps)
            if last:
                out = _norm_modulate(h, final_w, zero_mod, zero_mod, tps, out_dtype=F32,
                                     row_offset_tiles=ctx_tiles, n_rows=batch * seq)
            else:
                a1 = _norm_modulate(h, norm1_w[layer + 1], next_mod[0], next_mod[1], tps)
        else:
            qkv_w = M_HEADS * (2 * M_QK + M_V)
            n_gate = 4 * M_HEADS
            zq = _matmul(a1, od_w_in, j, qkv_w, tn=1024)
            w_gate_cols = jnp.pad(od_w_in[j, :, qkv_w:qkv_w + n_gate], ((0, 0), (0, 128 - n_gate)))[None]
            gates = _matmul(a1, w_gate_cols, 0, 128, tn=128)[:, :n_gate]
            w_rest = od_w_in[j, :, qkv_w + n_gate:][None]
            z_rest = _matmul(a1, w_rest, 0, w_rest.shape[2], tn=1536)
            h_f, h_b = _mlstm(zq, gates, od_gate_b[j], batch, ctx_len, seq)
            m_out = _mlstm_head_out(h_f, h_b, z_rest, od_mnorm_w[j])
            u_out = _conformer_conv(z_rest, 1, 2, od_dw_w, od_dw_b, od_ln_w, od_ln_b, j, batch, ctx_len, seq)
            h, a2, route = _out_proj_norm(m_out, u_out, od_w_out_b, j, h, mod[2], norm2_w[layer], mod[3], mod[4],
                                          tps, router=(moe_router_w[j], moe_router_b[j]))
            row0 = n_ctx_rows if last else 0
            src_tok, block_expert, n_valid, pos = _routing_plan(route[row0:])
            y = _moe_experts(take(a2, src_tok + row0), block_expert, n_valid, moe_w_gate, moe_w_up, moe_w_down, j)
            y0, y1 = take(y, pos[:, 0]), take(y, pos[:, 1])
            if last:
                out = _combine_norm(h, y0, y1, route, mod[5], final_w, zero_mod, zero_mod, tps, ctx_tiles,
                                    emit_h=False, a_dtype=F32)[0]
            else:
                h, a1 = _combine_norm(h, y0, y1, route, mod[5], norm1_w[layer + 1], next_mod[0], next_mod[1], tps,
                                      0, emit_h=True, a_dtype=BF16)
        mod = next_mod
    return out.reshape(batch, seq, d)
```

```python
import functools

import jax
import jax.numpy as jnp
from jax import lax
from jax.experimental import pallas as pl
from jax.experimental.pallas import tpu as pltpu

F32 = jnp.float32
BF16 = jnp.bfloat16

NORM_EPS = 1e-6
ROPE_BASE = 10000.0
GRID_W = 64
HEAD_DIM = 128
Q_HEADS = 8
KV_HEADS = 2
ATT_BLOCK = 128
M_HEADS = 4
M_QK = 128
M_V = 256
M_CHUNK = 128
CONV_SHORT = 3
CONV_LONG = 31
N_EXPERTS = 8
TOP_K = 2

ROW_TILE = 512
CONV_TILE = 256
HALO = 16
NEG = -1e30
V7X_VMEM_LIMIT = 56 * 1024 * 1024


def _params(n_axes):
    return pltpu.CompilerParams(dimension_semantics=("arbitrary",) * n_axes, vmem_limit_bytes=V7X_VMEM_LIMIT)


def _sigmoid(x):
    return 1.0 / (1.0 + jnp.exp(-x))


def _silu(x):
    return x * _sigmoid(x)


def _ada_kernel(c_ref, w_ref, b_ref, o_ref):
    s = _silu(c_ref[...]).astype(BF16)
    o_ref[...] = jnp.dot(s, w_ref[...].astype(BF16), preferred_element_type=F32) + b_ref[...]


def _ada_table(cvec, ada_w, ada_b, tn=1024):
    depth, d, n = ada_w.shape
    rows = cvec.shape[0]
    return pl.pallas_call(
        _ada_kernel,
        grid=(depth, n // tn),
        in_specs=[pl.BlockSpec((rows, d), lambda l, j: (0, 0)),
                  pl.BlockSpec((None, d, tn), lambda l, j: (l, 0, j)),
                  pl.BlockSpec((None, 1, tn), lambda l, j: (l, 0, j))],
        out_specs=pl.BlockSpec((None, rows, tn), lambda l, j: (l, 0, j)),
        out_shape=jax.ShapeDtypeStruct((depth, rows, n), F32),
        compiler_params=_params(2),
        name="ada_table",
    )(cvec, ada_w, ada_b.reshape(depth, 1, n))


def _normmod(x, w, shift, scale):
    ms = jnp.mean(x * x, axis=-1, keepdims=True)
    y = x * lax.rsqrt(ms + NORM_EPS) * w
    return y * (1.0 + scale) + shift


def _normmod_kernel(h_ref, w_ref, sh_ref, sc_ref, o_ref):
    o_ref[...] = _normmod(h_ref[...], w_ref[...], sh_ref[...], sc_ref[...]).astype(o_ref.dtype)


def _top2_route(a, rwh_ref, rwl_ref, rb_ref):
    a_hi = a.astype(BF16)
    a_lo = (a - a_hi.astype(F32)).astype(BF16)
    w_hi = rwh_ref[...]
    logits = (jnp.dot(a_hi, w_hi, preferred_element_type=F32)
              + (jnp.dot(a_lo, w_hi, preferred_element_type=F32)
                 + jnp.dot(a_hi, rwl_ref[...], preferred_element_type=F32))) + rb_ref[...]
    lane = lax.broadcasted_iota(jnp.int32, logits.shape, 1)
    logits = jnp.where(lane < N_EXPERTS, logits, NEG)
    big = jnp.int32(1 << 20)
    m1 = jnp.max(logits, axis=-1, keepdims=True)
    i1 = jnp.min(jnp.where(logits == m1, lane, big), axis=-1, keepdims=True)
    rest = jnp.where(lane == i1, NEG, logits)
    m2 = jnp.max(rest, axis=-1, keepdims=True)
    i2 = jnp.min(jnp.where(rest == m2, lane, big), axis=-1, keepdims=True)
    e2 = jnp.exp(m2 - m1)
    w1 = 1.0 / (1.0 + e2)
    w2 = e2 / (1.0 + e2)
    r = jnp.where(lane == 0, i1.astype(F32), 0.0)
    r = jnp.where(lane == 1, i2.astype(F32), r)
    r = jnp.where(lane == 2, w1, r)
    r = jnp.where(lane == 3, w2, r)
    return r


def _seg_index(tiles_per_seq):
    return lambda i: (i + tiles_per_seq - 1) // tiles_per_seq


def _norm_modulate(h, w, shift, scale, tiles_per_seq, out_dtype=BF16, row_offset_tiles=0, n_rows=None):
    t, d = h.shape
    n_rows = t if n_rows is None else n_rows
    seg = _seg_index(tiles_per_seq)
    off = row_offset_tiles
    return pl.pallas_call(
        _normmod_kernel,
        grid=(n_rows // ROW_TILE,),
        in_specs=[pl.BlockSpec((ROW_TILE, d), lambda i: (i + off, 0)),
                  pl.BlockSpec((1, d), lambda i: (0, 0)),
                  pl.BlockSpec((None, 1, d), lambda i: (seg(i + off), 0, 0)),
                  pl.BlockSpec((None, 1, d), lambda i: (seg(i + off), 0, 0))],
        out_specs=pl.BlockSpec((ROW_TILE, d), lambda i: (i, 0)),
        out_shape=jax.ShapeDtypeStruct((n_rows, d), out_dtype),
        compiler_params=_params(1),
        name="norm_modulate",
    )(h, w.reshape(1, d), shift, scale)


def _mm_kernel(x_ref, w_ref, o_ref, wb_ref):
    @pl.when(pl.program_id(1) == 0)
    def _():
        wb_ref[...] = w_ref[...].astype(BF16)

    o_ref[...] = jnp.dot(x_ref[...], wb_ref[...], preferred_element_type=F32).astype(o_ref.dtype)


def _matmul(x, w3, layer, n_cols, tn=512, out_dtype=F32):
    t, k = x.shape
    return pl.pallas_call(
        _mm_kernel,
        grid=(n_cols // tn, t // ROW_TILE),
        in_specs=[pl.BlockSpec((ROW_TILE, k), lambda j, i: (i, 0)),
                  pl.BlockSpec((None, k, tn), lambda j, i: (layer, 0, j))],
        out_specs=pl.BlockSpec((ROW_TILE, tn), lambda j, i: (i, j)),
        out_shape=jax.ShapeDtypeStruct((t, n_cols), out_dtype),
        scratch_shapes=[pltpu.VMEM((k, tn), BF16)],
        compiler_params=_params(2),
        name="matmul",
    )(x, w3)


def _out_proj_kernel(*refs, route):
    if route:
        (x1_ref, x2_ref, w1_ref, w2_ref, h_ref, g_ref, nw_ref, sh_ref, sc_ref, rwh_ref, rwl_ref, rb_ref,
         ho_ref, a_ref, r_ref) = refs
    else:
        x1_ref, x2_ref, w1_ref, w2_ref, h_ref, g_ref, nw_ref, sh_ref, sc_ref, ho_ref, a_ref = refs
    y = jnp.dot(x1_ref[...], w1_ref[...], preferred_element_type=F32)
    y = y + jnp.dot(x2_ref[...], w2_ref[...], preferred_element_type=F32)
    hn = h_ref[...] + g_ref[...] * y
    ho_ref[...] = hn
    a = _normmod(hn, nw_ref[...], sh_ref[...], sc_ref[...])
    a_ref[...] = a.astype(a_ref.dtype)
    if route:
        r_ref[...] = _top2_route(a, rwh_ref, rwl_ref, rb_ref)


def _out_proj_norm(x1, x2, w3b, layer, h, gate, norm_w, shift, scale, tiles_per_seq, router=None):
    t, k1 = x1.shape
    k2 = x2.shape[1]
    assert k1 == k2
    d = w3b.shape[2]
    seg = _seg_index(tiles_per_seq)
    row = lambda w: pl.BlockSpec((ROW_TILE, w), lambda i: (i, 0))
    modv = pl.BlockSpec((None, 1, d), lambda i: (seg(i), 0, 0))
    in_specs = [row(k1), row(k2),
                pl.BlockSpec((None, k1, d), lambda i: (layer, 0, 0), pipeline_mode=pl.Buffered(1)),
                pl.BlockSpec((None, k2, d), lambda i: (layer, 1, 0), pipeline_mode=pl.Buffered(1)),
                row(d), modv, pl.BlockSpec((1, d), lambda i: (0, 0)), modv, modv]
    args = [x1, x2, w3b, w3b, h, gate, norm_w.reshape(1, d), shift, scale]
    out_specs = [row(d), row(d)]
    out_shape = [jax.ShapeDtypeStruct((t, d), F32), jax.ShapeDtypeStruct((t, d), BF16)]
    if router is not None:
        router_w, router_b = router
        rw = jnp.pad(router_w.astype(F32), ((0, 0), (0, 128 - N_EXPERTS)))
        rw_hi = rw.astype(BF16)
        rw_lo = (rw - rw_hi.astype(F32)).astype(BF16)
        in_specs += [pl.BlockSpec((d, 128), lambda i: (0, 0)), pl.BlockSpec((d, 128), lambda i: (0, 0)),
                     pl.BlockSpec((1, 128), lambda i: (0, 0))]
        args += [rw_hi, rw_lo, jnp.pad(router_b, (0, 128 - N_EXPERTS)).reshape(1, 128)]
        out_specs.append(row(128))
        out_shape.append(jax.ShapeDtypeStruct((t, 128), F32))
    return pl.pallas_call(
        functools.partial(_out_proj_kernel, route=router is not None),
        grid=(t // ROW_TILE,),
        in_specs=in_specs,
        out_specs=out_specs,
        out_shape=out_shape,
        compiler_params=_params(1),
        name="out_proj_norm",
    )(*args)


def _ffn_up_kernel(x_ref, wg_ref, wu_ref, o_ref, wgb_ref, wub_ref):
    @pl.when(pl.program_id(1) == 0)
    def _():
        wgb_ref[...] = wg_ref[...].astype(BF16)
        wub_ref[...] = wu_ref[...].astype(BF16)

    x = x_ref[...]
    g = jnp.dot(x, wgb_ref[...], preferred_element_type=F32)
    u = jnp.dot(x, wub_ref[...], preferred_element_type=F32)
    o_ref[...] = (_silu(g) * u).astype(o_ref.dtype)


def _ffn_up(x, w_gate, w_up, layer, tf=512):
    t, k = x.shape
    f = w_gate.shape[2]
    return pl.pallas_call(
        _ffn_up_kernel,
        grid=(f // tf, t // ROW_TILE),
        in_specs=[pl.BlockSpec((ROW_TILE, k), lambda j, i: (i, 0)),
                  pl.BlockSpec((None, k, tf), lambda j, i: (layer, 0, j)),
                  pl.BlockSpec((None, k, tf), lambda j, i: (layer, 0, j))],
        out_specs=pl.BlockSpec((ROW_TILE, tf), lambda j, i: (i, j)),
        out_shape=jax.ShapeDtypeStruct((t, f), BF16),
        scratch_shapes=[pltpu.VMEM((k, tf), BF16), pltpu.VMEM((k, tf), BF16)],
        compiler_params=_params(2),
        name="ffn_up",
    )(x, w_gate, w_up)


def _mm_resid_kernel(x_ref, w_ref, h_ref, g_ref, o_ref, wb_ref):
    @pl.when(pl.program_id(1) == 0)
    def _():
        wb_ref[...] = w_ref[...].astype(BF16)

    y = jnp.dot(x_ref[...], wb_ref[...], preferred_element_type=F32)
    o_ref[...] = h_ref[...] + g_ref[...] * y


def _ffn_down_residual(x, w3, layer, h, gate, tiles_per_seq, tn=512):
    t, k = x.shape
    n = w3.shape[2]
    seg = _seg_index(tiles_per_seq)
    return pl.pallas_call(
        _mm_resid_kernel,
        grid=(n // tn, t // ROW_TILE),
        in_specs=[pl.BlockSpec((ROW_TILE, k), lambda j, i: (i, 0)),
                  pl.BlockSpec((None, k, tn), lambda j, i: (layer, 0, j)),
                  pl.BlockSpec((ROW_TILE, tn), lambda j, i: (i, j)),
                  pl.BlockSpec((None, 1, tn), lambda j, i: (seg(i), 0, j))],
        out_specs=pl.BlockSpec((ROW_TILE, tn), lambda j, i: (i, j)),
        out_shape=jax.ShapeDtypeStruct((t, n), F32),
        scratch_shapes=[pltpu.VMEM((k, tn), BF16)],
        compiler_params=_params(2),
        name="ffn_down_residual",
    )(x, w3, h, gate)


def _rope(x, cs):
    return x * cs[:, :HEAD_DIM] + pltpu.roll(x, HEAD_DIM // 2, axis=1) * cs[:, HEAD_DIM:]


def _attend(q_all, k_parts, v_parts, sink_ref, o_ref, mask_fn):
    group = Q_HEADS // KV_HEADS
    scale = HEAD_DIM ** -0.5
    for g in range(KV_HEADS):
        qg = jnp.concatenate([q_all[:, (g * group + r) * HEAD_DIM:(g * group + r + 1) * HEAD_DIM]
                              for r in range(group)], axis=0).astype(BF16)
        kg = jnp.concatenate([kp[:, g * HEAD_DIM:(g + 1) * HEAD_DIM] for kp in k_parts], axis=0).astype(BF16)
        vg = jnp.concatenate([vp[:, g * HEAD_DIM:(g + 1) * HEAD_DIM] for vp in v_parts], axis=0).astype(BF16)
        s = lax.dot_general(qg, kg, (((1,), (1,)), ((), ())), preferred_element_type=F32) * scale
        if mask_fn is not None:
            s = jnp.where(mask_fn(s.shape), s, NEG)
        sink = sink_ref[g]
        m = jnp.maximum(jnp.max(s, axis=-1, keepdims=True), sink)
        p = jnp.exp(s - m)
        denom = jnp.sum(p, axis=-1, keepdims=True) + jnp.exp(sink - m)
        o = jnp.dot(p.astype(BF16), vg, preferred_element_type=F32) / denom
        for r in range(group):
            hq = g * group + r
            o_ref[:, hq * HEAD_DIM:(hq + 1) * HEAD_DIM] = o[r * ATT_BLOCK:(r + 1) * ATT_BLOCK].astype(o_ref.dtype)


def _attn_kernel(q_ref, kp_ref, kc_ref, kn_ref, kx_ref, vp_ref, vc_ref, vn_ref, vx_ref,
                 csp_ref, csc_ref, csn_ref, sink_ref, o_ref, *, n_blocks, ctx_blocks):
    step = pl.program_id(1)

    @pl.when(step < ctx_blocks)
    def _():
        _attend(q_ref[...], [kx_ref[...]], [vx_ref[...]], sink_ref, o_ref, None)

    @pl.when(step >= ctx_blocks)
    def _():
        _win_attn_body(q_ref, kp_ref, kc_ref, kn_ref, kx_ref, vp_ref, vc_ref, vn_ref, vx_ref,
                       csp_ref, csc_ref, csn_ref, sink_ref, o_ref, step - ctx_blocks, n_blocks)


def _win_attn_body(q_ref, kp_ref, kc_ref, kn_ref, kx_ref, vp_ref, vc_ref, vn_ref, vx_ref,
                   csp_ref, csc_ref, csn_ref, sink_ref, o_ref, n, n_blocks):
    csc = csc_ref[...]
    q_all = jnp.concatenate([_rope(q_ref[:, h * HEAD_DIM:(h + 1) * HEAD_DIM], csc) for h in range(Q_HEADS)], axis=1)

    def rope_kv(k_ref, cs):
        return jnp.concatenate([_rope(k_ref[:, g * HEAD_DIM:(g + 1) * HEAD_DIM], cs) for g in range(KV_HEADS)], axis=1)

    k_parts = [rope_kv(kp_ref, csp_ref[...]), rope_kv(kc_ref, csc), rope_kv(kn_ref, csn_ref[...]), kx_ref[...]]
    v_parts = [vp_ref[...], vc_ref[...], vn_ref[...], vx_ref[...]]

    def mask_fn(shape):
        row = lax.broadcasted_iota(jnp.int32, shape, 0) & (ATT_BLOCK - 1)
        col = lax.broadcasted_iota(jnp.int32, shape, 1)
        band = (col >= row) & (col <= row + 2 * ATT_BLOCK)
        ok_prev = (col >= ATT_BLOCK) | (n > 0)
        ok_next = (col < 2 * ATT_BLOCK) | (n < n_blocks - 1)
        return (band & ok_prev & ok_next) | (col >= 3 * ATT_BLOCK)

    _attend(q_all, k_parts, v_parts, sink_ref, o_ref, mask_fn)


def _sink_rows(sink):
    group = Q_HEADS // KV_HEADS
    return jnp.repeat(sink.astype(F32).reshape(KV_HEADS, group), ATT_BLOCK, axis=1).reshape(
        KV_HEADS, group * ATT_BLOCK, 1)


def _window_attention(z, cs, sink, batch, ctx_len, seq):
    t = z.shape[0]
    nb = seq // ATT_BLOCK
    cb = ctx_len // ATT_BLOCK
    base = batch * cb
    qw = Q_HEADS * HEAD_DIM
    kw = KV_HEADS * HEAD_DIM
    kcol = qw // kw
    vcol = kcol + 1
    sink_rows = _sink_rows(sink)

    def lat(s):
        return jnp.clip(s - cb, 0, nb - 1)

    def q_block(b, s):
        return jnp.where(s < cb, b * cb + s, base + b * nb + lat(s))

    def kv(shift, col):
        return pl.BlockSpec((ATT_BLOCK, kw), lambda b, s: (base + b * nb + lat(s + shift), col))

    def rot(shift):
        return pl.BlockSpec((ATT_BLOCK, 2 * HEAD_DIM), lambda b, s: (lat(s + shift), 0))

    return pl.pallas_call(
        functools.partial(_attn_kernel, n_blocks=nb, ctx_blocks=cb),
        grid=(batch, cb + nb),
        in_specs=[pl.BlockSpec((ATT_BLOCK, qw), lambda b, s: (q_block(b, s), 0)),
                  kv(-1, kcol), kv(0, kcol), kv(1, kcol),
                  pl.BlockSpec((ctx_len, kw), lambda b, s: (b, kcol)),
                  kv(-1, vcol), kv(0, vcol), kv(1, vcol),
                  pl.BlockSpec((ctx_len, kw), lambda b, s: (b, vcol)),
                  rot(-1), rot(0), rot(1),
                  pl.BlockSpec(sink_rows.shape, lambda b, s: (0, 0, 0))],
        out_specs=pl.BlockSpec((ATT_BLOCK, qw), lambda b, s: (q_block(b, s), 0)),
        out_shape=jax.ShapeDtypeStruct((t, qw), BF16),
        compiler_params=_params(2),
        name="window_attention",
    )(z, z, z, z, z, z, z, z, z, cs, cs, cs, sink_rows)


def _seq_edges(t, batch, ctx_tiles, lat_tiles):
    n_ctx = batch * ctx_tiles
    u = t - n_ctx
    is_ctx = t < n_ctx
    first = jnp.where(is_ctx, lax.rem(t, ctx_tiles) == 0, lax.rem(u, lat_tiles) == 0)
    last = jnp.where(is_ctx, lax.rem(t, ctx_tiles) == ctx_tiles - 1, lax.rem(u, lat_tiles) == lat_tiles - 1)
    return first, last


def _fill_ext(ext_ref, cur, prev, nxt, first, last):
    ext_ref[HALO:HALO + CONV_TILE, :] = cur
    ext_ref[0:HALO, :] = jnp.where(first, 0.0, prev)
    ext_ref[HALO + CONV_TILE:, :] = jnp.where(last, 0.0, nxt)


def _dwconv_rows(ext_ref, w_ref, taps, r0, n_rows):
    pad = taps // 2
    acc = None
    for k in range(taps):
        start = HALO + r0 + k - pad
        term = w_ref[k:k + 1, :] * ext_ref[start:start + n_rows, :]
        acc = term if acc is None else acc + term
    return acc


CONV_ROWS = 32


def _short_conv_kernel(bg_ref, cg_ref, xb_ref, cgp_ref, xbp_ref, cgn_ref, xbn_ref, w_ref, o_ref, ext_ref,
                       *, batch, ctx_tiles, lat_tiles):
    first, last = _seq_edges(pl.program_id(0), batch, ctx_tiles, lat_tiles)
    _fill_ext(ext_ref, cg_ref[...] * xb_ref[...], cgp_ref[...] * xbp_ref[...], cgn_ref[...] * xbn_ref[...],
              first, last)
    for r0 in range(0, CONV_TILE, CONV_ROWS):
        conv = _dwconv_rows(ext_ref, w_ref, CONV_SHORT, r0, CONV_ROWS)
        o_ref[r0:r0 + CONV_ROWS, :] = (bg_ref[r0:r0 + CONV_ROWS, :] * conv).astype(o_ref.dtype)


def _halo_specs(width, col, halo_per_tile, n_halo_blocks):
    prev = pl.BlockSpec((HALO, width), lambda t, c: (jnp.maximum(t * halo_per_tile - 1, 0), col(c)))
    nxt = pl.BlockSpec((HALO, width), lambda t, c: (jnp.minimum((t + 1) * halo_per_tile, n_halo_blocks - 1), col(c)))
    return prev, nxt


def _short_conv(z, conv_w3, layer, batch, ctx_len, seq):
    t = z.shape[0]
    cw = 512
    n_c = 1024 // cw
    bg0, cg0, xb0 = 1536 // cw, 2560 // cw, 3584 // cw
    hpt = CONV_TILE // HALO
    nh = t // HALO
    cgp, cgn = _halo_specs(cw, lambda c: cg0 + c, hpt, nh)
    xbp, xbn = _halo_specs(cw, lambda c: xb0 + c, hpt, nh)
    kern = functools.partial(_short_conv_kernel, batch=batch, ctx_tiles=ctx_len // CONV_TILE,
                             lat_tiles=seq // CONV_TILE)
    return pl.pallas_call(
        kern,
        grid=(t // CONV_TILE, n_c),
        in_specs=[pl.BlockSpec((CONV_TILE, cw), lambda t_, c: (t_, bg0 + c)),
                  pl.BlockSpec((CONV_TILE, cw), lambda t_, c: (t_, cg0 + c)),
                  pl.BlockSpec((CONV_TILE, cw), lambda t_, c: (t_, xb0 + c)),
                  cgp, xbp, cgn, xbn,
                  pl.BlockSpec((None, CONV_SHORT, cw), lambda t_, c: (layer, 0, c))],
        out_specs=pl.BlockSpec((CONV_TILE, cw), lambda t_, c: (t_, c)),
        out_shape=jax.ShapeDtypeStruct((t, 1024), BF16),
        scratch_shapes=[pltpu.VMEM((CONV_TILE + 2 * HALO, cw), F32)],
        compiler_params=_params(2),
        name="short_conv",
    )(z, z, z, z, z, z, z, conv_w3)


SUBLANES = 8
SHIFT_ROWS = CONV_TILE + 2 * HALO - SUBLANES


def _dwconv_rows_aligned(ext_ref, sh_ref, w8_ref, taps, r0, n_rows):
    pad = taps // 2
    width = ext_ref.shape[1]
    acc = None
    for k in range(taps):
        start = HALO + r0 + k - pad
        b = start % SUBLANES
        a8 = start - b
        src = ext_ref[a8:a8 + n_rows, :] if b == 0 else sh_ref[b - 1, a8:a8 + n_rows, :]
        term = (src.reshape(n_rows // SUBLANES, SUBLANES, width) * w8_ref[k][None]).reshape(n_rows, width)
        acc = term if acc is None else acc + term
    return acc


def _conformer_kernel(a_ref, g_ref, ap_ref, gp_ref, an_ref, gn_ref, w8_ref, b_ref, lw_ref, lb_ref, o_ref,
                      ext_ref, sh_ref, *, batch, ctx_tiles, lat_tiles):
    first, last = _seq_edges(pl.program_id(0), batch, ctx_tiles, lat_tiles)
    _fill_ext(ext_ref, a_ref[...] * _sigmoid(g_ref[...]), ap_ref[...] * _sigmoid(gp_ref[...]),
              an_ref[...] * _sigmoid(gn_ref[...]), first, last)
    for b in range(1, SUBLANES):
        sh_ref[b - 1] = ext_ref[b:b + SHIFT_ROWS, :]
    for r0 in range(0, CONV_TILE, CONV_ROWS):
        u = _dwconv_rows_aligned(ext_ref, sh_ref, w8_ref, CONV_LONG, r0, CONV_ROWS) + b_ref[...]
        mu = jnp.mean(u, axis=-1, keepdims=True)
        uc = u - mu
        var = jnp.mean(uc * uc, axis=-1, keepdims=True)
        y = uc * lax.rsqrt(var + NORM_EPS) * lw_ref[...] + lb_ref[...]
        o_ref[r0:r0 + CONV_ROWS, :] = _silu(y).astype(o_ref.dtype)


def _conformer_conv(z, a_col, g_col, dw_w3, dw_b, ln_w, ln_b, layer, batch, ctx_len, seq):
    t = z.shape[0]
    cw = 1024
    hpt = CONV_TILE // HALO
    nh = t // HALO
    ap, an = _halo_specs(cw, lambda c: a_col, hpt, nh)
    gp, gn = _halo_specs(cw, lambda c: g_col, hpt, nh)
    kern = functools.partial(_conformer_kernel, batch=batch, ctx_tiles=ctx_len // CONV_TILE,
                             lat_tiles=seq // CONV_TILE)
    vec = pl.BlockSpec((None, 1, cw), lambda t_, c: (layer, 0, 0))
    n_layers = dw_b.shape[0]
    return pl.pallas_call(
        kern,
        grid=(t // CONV_TILE, 1),
        in_specs=[pl.BlockSpec((CONV_TILE, cw), lambda t_, c: (t_, a_col)),
                  pl.BlockSpec((CONV_TILE, cw), lambda t_, c: (t_, g_col)),
                  ap, gp, an, gn,
                  pl.BlockSpec((None, CONV_LONG, SUBLANES, cw), lambda t_, c: (layer, 0, 0, 0)),
                  vec, vec, vec],
        out_specs=pl.BlockSpec((CONV_TILE, cw), lambda t_, c: (t_, 0)),
        out_shape=jax.ShapeDtypeStruct((t, cw), BF16),
        scratch_shapes=[pltpu.VMEM((CONV_TILE + 2 * HALO, cw), F32),
                        pltpu.VMEM((SUBLANES - 1, SHIFT_ROWS, cw), F32)],
        compiler_params=_params(2),
        name="conformer_conv",
    )(z, z, z, z, z, z, jnp.broadcast_to(dw_w3[:, :, None, :], (n_layers, CONV_LONG, SUBLANES, cw)),
      dw_b.reshape(n_layers, 1, cw), ln_w.reshape(n_layers, 1, cw), ln_b.reshape(n_layers, 1, cw))


def _log_sigmoid(x):
    return jnp.minimum(x, 0.0) - jnp.log(1.0 + jnp.exp(-jnp.abs(x)))


def _mlstm_kernel(qf_ref, kf_ref, vf_ref, gcf_ref, grf_ref, qb_ref, kb_ref, vb_ref, gcb_ref, grb_ref,
                  bc_ref, br_ref, of_ref, ob_ref, *state_refs):
    n_chain = 2 * M_HEADS
    ct_ref, n_ref, m_ref = state_refs[:n_chain], state_refs[n_chain:2 * n_chain], state_refs[2 * n_chain:]

    @pl.when(pl.program_id(1) == 0)
    def _():
        for ref in state_refs:
            ref[...] = jnp.zeros_like(ref)

    ln = M_CHUNK
    row = lax.broadcasted_iota(jnp.int32, (ln, ln), 0)
    col = lax.broadcasted_iota(jnp.int32, (ln, ln), 1)
    lower = col <= row
    upper = col >= row
    _mlstm_direction(qf_ref, kf_ref, vf_ref, gcf_ref[...] + bc_ref[0], grf_ref[...] + br_ref[0], lower, upper,
                     of_ref, ct_ref, n_ref, m_ref, 0)
    _mlstm_direction(qb_ref, kb_ref, vb_ref, gcb_ref[...] + bc_ref[1], grb_ref[...] + br_ref[1], upper, lower,
                     ob_ref, ct_ref, n_ref, m_ref, M_HEADS)


def _mlstm_direction(q_ref, k_ref, v_ref, gc, gr, mask, mask_t, o_ref, ct_ref, n_ref, m_ref, state0):
    for h in range(M_HEADS):
        st = state0 + h
        ig_c = gc[:, h:h + 1]
        lf_c = _log_sigmoid(gc[:, M_HEADS + h:M_HEADS + h + 1])
        ig_r = gr[h:h + 1, :]
        lf_r = _log_sigmoid(gr[M_HEADS + h:M_HEADS + h + 1, :])
        q = q_ref[:, h * M_QK:(h + 1) * M_QK] * (M_QK ** -0.5)
        k = k_ref[:, h * M_QK:(h + 1) * M_QK]
        v = v_ref[:, h * M_V:(h + 1) * M_V]
        qb = q.astype(BF16)
        kb = k.astype(BF16)
        cum_c = jnp.sum(jnp.where(mask, lf_r, 0.0), axis=1, keepdims=True)
        cum_r = jnp.sum(jnp.where(mask_t, lf_c, 0.0), axis=0, keepdims=True)
        total = jnp.sum(lf_r, axis=1, keepdims=True)
        m_s = m_ref[st][...]
        dlog = jnp.where(mask, cum_c - cum_r + ig_r, NEG)
        inter = cum_c + m_s
        m_t = jnp.maximum(inter, jnp.max(dlog, axis=1, keepdims=True))
        w_inter = jnp.exp(inter - m_t)
        s = lax.dot_general(qb, kb, (((1,), (1,)), ((), ())), preferred_element_type=F32) * jnp.exp(dlog - m_t)
        ct = ct_ref[st][...]
        n_v = n_ref[st][...]
        num = (w_inter * jnp.dot(qb, ct.astype(BF16), preferred_element_type=F32)
               + jnp.dot(s.astype(BF16), v.astype(BF16), preferred_element_type=F32))
        den = w_inter * jnp.sum(q * n_v, axis=1, keepdims=True) + jnp.sum(s, axis=1, keepdims=True)
        o_ref[:, h * M_V:(h + 1) * M_V] = num / jnp.maximum(jnp.abs(den), jnp.exp(-m_t))
        gl = total - cum_c + ig_c
        m_new = jnp.maximum(total + m_s, jnp.max(gl, axis=0, keepdims=True))
        wg = jnp.exp(gl - m_new)
        decay = jnp.exp(total + m_s - m_new)
        ct_ref[st][...] = decay * ct + jnp.dot(k.T.astype(BF16), (wg * v).astype(BF16), preferred_element_type=F32)
        n_ref[st][...] = decay * n_v + jnp.sum(wg * k, axis=0, keepdims=True)
        m_ref[st][...] = m_new


def _mlstm(zq, gates, gate_b, batch, ctx_len, seq):
    t = zq.shape[0]
    cb = ctx_len // M_CHUNK
    nc = seq // M_CHUNK
    base = batch * cb
    steps = cb + nc
    half = 2 * M_HEADS
    g_dir = gates.reshape(t, 2, half).transpose(1, 0, 2)
    g_dir_t = g_dir.transpose(0, 2, 1)
    b_dir = gate_b.astype(F32).reshape(2, 1, half)
    b_dir_t = b_dir.transpose(0, 2, 1)

    def rb_f(b, i):
        return jnp.where(i < cb, b * cb + i, base + b * nc + (i - cb))

    def rb_b(b, i):
        return jnp.where(i < cb, b * cb + (cb - 1 - i), base + b * nc + (nc - 1 - (i - cb)))

    qk_w = M_HEADS * M_QK
    v_w = M_HEADS * M_V

    def chunk_specs(d, rb):
        return [pl.BlockSpec((M_CHUNK, qk_w), lambda b, i: (rb(b, i), 0)),
                pl.BlockSpec((M_CHUNK, qk_w), lambda b, i: (rb(b, i), 1)),
                pl.BlockSpec((M_CHUNK, v_w), lambda b, i: (rb(b, i), 1)),
                pl.BlockSpec((None, M_CHUNK, half), lambda b, i: (d, rb(b, i), 0)),
                pl.BlockSpec((None, half, M_CHUNK), lambda b, i: (d, 0, rb(b, i)))]

    return pl.pallas_call(
        _mlstm_kernel,
        grid=(batch, steps),
        in_specs=chunk_specs(0, rb_f) + chunk_specs(1, rb_b) + [
            pl.BlockSpec((2, 1, half), lambda b, i: (0, 0, 0)),
            pl.BlockSpec((2, half, 1), lambda b, i: (0, 0, 0))],
        out_specs=[pl.BlockSpec((M_CHUNK, v_w), lambda b, i: (rb_f(b, i), 0)),
                   pl.BlockSpec((M_CHUNK, v_w), lambda b, i: (rb_b(b, i), 0))],
        out_shape=[jax.ShapeDtypeStruct((t, v_w), F32), jax.ShapeDtypeStruct((t, v_w), F32)],
        scratch_shapes=([pltpu.VMEM((M_QK, M_V), F32)] * (2 * M_HEADS)
                        + [pltpu.VMEM((1, M_QK), F32)] * (2 * M_HEADS)
                        + [pltpu.VMEM((1, 1), F32)] * (2 * M_HEADS)),
        compiler_params=_params(2),
        name="mlstm",
    )(zq, zq, zq, g_dir, g_dir_t, zq, zq, zq, g_dir, g_dir_t, b_dir, b_dir_t)


def _head_out_kernel(hf_ref, hb_ref, o_ref, w_ref, out_ref):
    for h in range(M_HEADS):
        sl = slice(h * M_V, (h + 1) * M_V)
        x = hf_ref[:, sl] + hb_ref[:, sl]
        mu = jnp.mean(x, axis=-1, keepdims=True)
        xc = x - mu
        var = jnp.mean(xc * xc, axis=-1, keepdims=True)
        y = xc * lax.rsqrt(var + NORM_EPS) * w_ref[:, sl]
        out_ref[:, sl] = (y * _sigmoid(o_ref[:, sl])).astype(out_ref.dtype)


def _mlstm_head_out(h_f, h_b, z_rest, mnorm_w):
    t, w = h_f.shape
    tile = pl.BlockSpec((ROW_TILE, w), lambda i: (i, 0))
    return pl.pallas_call(
        _head_out_kernel,
        grid=(t // ROW_TILE,),
        in_specs=[tile, tile, tile, pl.BlockSpec((1, w), lambda i: (0, 0))],
        out_specs=tile,
        out_shape=jax.ShapeDtypeStruct((t, w), BF16),
        compiler_params=_params(1),
        name="mlstm_head_out",
    )(h_f, h_b, z_rest, mnorm_w.reshape(1, w))


def _fresh_weights(be_ref, rb):
    return (rb == 0) | (be_ref[rb] != be_ref[jnp.maximum(rb - 1, 0)])


def _moe_up_kernel(be_ref, nv_ref, x_ref, wg_ref, wu_ref, o_ref, wgb_ref, wub_ref):
    rb = pl.program_id(1)
    valid = rb < nv_ref[0]

    @pl.when(valid & _fresh_weights(be_ref, rb))
    def _():
        wgb_ref[...] = wg_ref[...].astype(BF16)
        wub_ref[...] = wu_ref[...].astype(BF16)

    @pl.when(valid)
    def _():
        x = x_ref[...]
        g = jnp.dot(x, wgb_ref[...], preferred_element_type=F32)
        u = jnp.dot(x, wub_ref[...], preferred_element_type=F32)
        o_ref[...] = (_silu(g) * u).astype(o_ref.dtype)

    @pl.when(jnp.logical_not(valid))
    def _():
        o_ref[...] = jnp.zeros_like(o_ref)


def _moe_down_kernel(be_ref, nv_ref, x_ref, w_ref, o_ref, wb_ref):
    rb = pl.program_id(1)
    valid = rb < nv_ref[0]

    @pl.when(valid & _fresh_weights(be_ref, rb))
    def _():
        wb_ref[...] = w_ref[...].astype(BF16)

    @pl.when(valid)
    def _():
        o_ref[...] = jnp.dot(x_ref[...], wb_ref[...], preferred_element_type=F32).astype(o_ref.dtype)

    @pl.when(jnp.logical_not(valid))
    def _():
        o_ref[...] = jnp.zeros_like(o_ref)


def _moe_experts(x_sorted, block_expert, n_valid, w_gate, w_up, w_down, layer, tf=512, tn=512):
    p, d = x_sorted.shape
    f = w_gate.shape[3]
    nblk = p // ROW_TILE

    def row(rb, nv):
        return jnp.minimum(rb, nv[0] - 1)

    g = pl.pallas_call(
        _moe_up_kernel,
        grid_spec=pltpu.PrefetchScalarGridSpec(
            num_scalar_prefetch=2,
            grid=(f // tf, nblk),
            in_specs=[pl.BlockSpec((ROW_TILE, d), lambda j, rb, be, nv: (row(rb, nv), 0)),
                      pl.BlockSpec((None, None, d, tf), lambda j, rb, be, nv: (layer, be[rb], 0, j)),
                      pl.BlockSpec((None, None, d, tf), lambda j, rb, be, nv: (layer, be[rb], 0, j))],
            out_specs=pl.BlockSpec((ROW_TILE, tf), lambda j, rb, be, nv: (rb, j)),
            scratch_shapes=[pltpu.VMEM((d, tf), BF16), pltpu.VMEM((d, tf), BF16)]),
        out_shape=jax.ShapeDtypeStruct((p, f), BF16),
        compiler_params=_params(2),
        name="moe_up",
    )(block_expert, n_valid, x_sorted, w_gate, w_up)
    return pl.pallas_call(
        _moe_down_kernel,
        grid_spec=pltpu.PrefetchScalarGridSpec(
            num_scalar_prefetch=2,
            grid=(d // tn, nblk),
            in_specs=[pl.BlockSpec((ROW_TILE, f), lambda j, rb, be, nv: (row(rb, nv), 0)),
                      pl.BlockSpec((None, None, f, tn), lambda j, rb, be, nv: (layer, be[rb], 0, j))],
            out_specs=pl.BlockSpec((ROW_TILE, tn), lambda j, rb, be, nv: (rb, j)),
            scratch_shapes=[pltpu.VMEM((f, tn), BF16)]),
        out_shape=jax.ShapeDtypeStruct((p, d), BF16),
        compiler_params=_params(2),
        name="moe_down",
    )(block_expert, n_valid, g, w_down)


def _routing_plan(route):
    t = route.shape[0]
    tm = ROW_TILE
    n_assign = TOP_K * t
    nblk = (n_assign + N_EXPERTS * (tm - 1)) // tm
    e = route[:, :TOP_K].astype(jnp.int32).reshape(-1)
    onehot = (e[:, None] == jnp.arange(N_EXPERTS, dtype=jnp.int32)[None, :]).astype(jnp.int32)
    csum = jnp.cumsum(onehot, axis=0)
    counts = csum[-1]
    padded = ((counts + tm - 1) // tm) * tm
    ends = jnp.cumsum(padded)
    starts = ends - padded
    dest = jnp.sum((csum - onehot + starts[None, :]) * onehot, axis=1)
    src_tok = jnp.zeros((nblk * tm,), jnp.int32).at[dest].set(
        jnp.arange(n_assign, dtype=jnp.int32) // TOP_K, unique_indices=True, mode="promise_in_bounds")
    n_valid = (ends[-1] // tm).astype(jnp.int32)
    blk_start = jnp.arange(nblk, dtype=jnp.int32) * tm
    be = jnp.sum((ends[None, :] <= blk_start[:, None]).astype(jnp.int32), axis=1)
    be = jnp.minimum(be, N_EXPERTS - 1).astype(jnp.int32)
    return src_tok, be, n_valid.reshape(1), dest.reshape(t, TOP_K)


def _combine_kernel(h_ref, y0_ref, y1_ref, r_ref, g_ref, nw_ref, sh_ref, sc_ref, *out_refs):
    r = r_ref[...]
    moe = r[:, TOP_K:TOP_K + 1] * y0_ref[...].astype(F32) + r[:, TOP_K + 1:TOP_K + 2] * y1_ref[...].astype(F32)
    hn = h_ref[...] + g_ref[...] * moe
    if len(out_refs) == 2:
        out_refs[0][...] = hn
    out_refs[-1][...] = _normmod(hn, nw_ref[...], sh_ref[...], sc_ref[...]).astype(out_refs[-1].dtype)


def _combine_norm(h, y0, y1, route, gate, norm_w, shift, scale, tiles_per_seq, row_offset_tiles, emit_h, a_dtype):
    d = h.shape[1]
    n_rows = y0.shape[0]
    off = row_offset_tiles
    seg = _seg_index(tiles_per_seq)
    shifted = lambda w: pl.BlockSpec((ROW_TILE, w), lambda i: (i + off, 0))
    tile = pl.BlockSpec((ROW_TILE, d), lambda i: (i, 0))
    modv = pl.BlockSpec((None, 1, d), lambda i: (seg(i + off), 0, 0))
    out_specs = [tile, tile] if emit_h else [tile]
    out_shape = [jax.ShapeDtypeStruct((n_rows, d), a_dtype)]
    if emit_h:
        out_shape.insert(0, jax.ShapeDtypeStruct((n_rows, d), F32))
    return pl.pallas_call(
        _combine_kernel,
        grid=(n_rows // ROW_TILE,),
        in_specs=[shifted(d), tile, tile, shifted(128), modv, pl.BlockSpec((1, d), lambda i: (0, 0)), modv, modv],
        out_specs=out_specs,
        out_shape=out_shape,
        compiler_params=_params(1),
        name="moe_combine_norm",
    )(h, y0, y1, route, gate, norm_w.reshape(1, d), shift, scale)


def _rope_table(seq):
    rows = seq // GRID_W
    row = jnp.repeat(jnp.arange(rows, dtype=F32), GRID_W)
    col = jnp.tile(jnp.arange(GRID_W, dtype=F32), rows)
    n_freq = HEAD_DIM // 4
    inv_freq = ROPE_BASE ** (-jnp.arange(n_freq, dtype=F32) / n_freq)
    ang = jnp.concatenate([row[:, None] * inv_freq, col[:, None] * inv_freq], axis=-1)
    cos, sin = jnp.cos(ang), jnp.sin(ang)
    return jnp.concatenate([cos, cos, -sin, sin], axis=-1)


def kernel(x, c, ctx, c_ctx, ada_w, ada_b, norm1_w, norm2_w, ev_w_in, ev_sink, ev_conv_w, ev_w_out, ffn_w_gate, ffn_w_up, ffn_w_down, od_w_in, od_gate_b, od_mnorm_w, od_dw_w, od_dw_b, od_ln_w, od_ln_b, od_w_out, moe_router_w, moe_router_b, moe_w_gate, moe_w_up, moe_w_down, final_w):
    batch, seq, d = x.shape
    ctx_len = ctx.shape[1]
    depth = ada_w.shape[0]
    assert batch * ctx_len == ROW_TILE and seq % ROW_TILE == 0 and ctx_len % CONV_TILE == 0
    assert 1 + batch <= 8
    tps = seq // ROW_TILE
    n_ctx_rows = batch * ctx_len

    cs = _rope_table(seq)
    cvec = jnp.zeros((8, d), F32).at[0].set(c_ctx).at[1:1 + batch].set(c)
    mod_table = _ada_table(cvec, ada_w, ada_b)
    h = jnp.concatenate([ctx.reshape(n_ctx_rows, d), x.reshape(batch * seq, d)], axis=0)

    def mods_of(layer):
        mods = mod_table[layer, :1 + batch].reshape(1 + batch, 6, 1, d)
        return [mods[:, i] for i in range(6)]

    zero_mod = jnp.zeros((1 + batch, 1, d), F32)
    ctx_tiles = n_ctx_rows // ROW_TILE
    ev_w_out_b = ev_w_out.astype(BF16)
    od_w_out_b = od_w_out.astype(BF16)
    def take(rows, idx):
        return rows.at[idx].get(mode="promise_in_bounds")

    out = None
    mod = mods_of(0)
    a1 = _norm_modulate(h, norm1_w[0], mod[0], mod[1], tps)
    for layer in range(depth):
        j = layer // 2
        last = layer == depth - 1
        next_mod = None if last else mods_of(layer + 1)
        if layer % 2 == 0:
            z = _matmul(a1, ev_w_in, j, ev_w_in.shape[2], tn=1536)
            o_a = _window_attention(z, cs, ev_sink[j], batch, ctx_len, seq)
            o_b = _short_conv(z, ev_conv_w, j, batch, ctx_len, seq)
            h, a2 = _out_proj_norm(o_a, o_b, ev_w_out_b, j, h, mod[2], norm2_w[layer], mod[3], mod[4], tps)
            g = _ffn_up(a2, ffn_w_gate, ffn_w_up, j)
            h = _ffn_down_residual(g, ffn_w_down, j, h, mod[5], tps)
            if last:
                out = _norm_modulate(h, final_w, zero_mod, zero_mod, tps, out_dtype=F32,
                                     row_offset_tiles=ctx_tiles, n_rows=batch * seq)
            else:
                a1 = _norm_modulate(h, norm1_w[layer + 1], next_mod[0], next_mod[1], tps)
        else:
            qkv_w = M_HEADS * (2 * M_QK + M_V)
            n_gate = 4 * M_HEADS
            zq = _matmul(a1, od_w_in, j, qkv_w, tn=1024)
            w_gate_cols = jnp.pad(od_w_in[j, :, qkv_w:qkv_w + n_gate], ((0, 0), (0, 128 - n_gate)))[None]
            gates = _matmul(a1, w_gate_cols, 0, 128, tn=128)[:, :n_gate]
            w_rest = od_w_in[j, :, qkv_w + n_gate:][None]
            z_rest = _matmul(a1, w_rest, 0, w_rest.shape[2], tn=1536)
            h_f, h_b = _mlstm(zq, gates, od_gate_b[j], batch, ctx_len, seq)
            m_out = _mlstm_head_out(h_f, h_b, z_rest, od_mnorm_w[j])
            u_out = _conformer_conv(z_rest, 1, 2, od_dw_w, od_dw_b, od_ln_w, od_ln_b, j, batch, ctx_len, seq)
            h, a2, route = _out_proj_norm(m_out, u_out, od_w_out_b, j, h, mod[2], norm2_w[layer], mod[3], mod[4],
                                          tps, router=(moe_router_w[j], moe_router_b[j]))
            row0 = n_ctx_rows if last else 0
            src_tok, block_expert, n_valid, pos = _routing_plan(route[row0:])
            x_sorted = jnp.take(a2, src_tok + row0, axis=0, mode="clip")
            y = _moe_experts(x_sorted, block_expert, n_valid, moe_w_gate, moe_w_up, moe_w_down, j)
            y0, y1 = take(y, pos[:, 0]), take(y, pos[:, 1])
            if last:
                out = _combine_norm(h, y0, y1, route, mod[5], final_w, zero_mod, zero_mod, tps, ctx_tiles,
                                    emit_h=False, a_dtype=F32)[0]
            else:
                h, a1 = _combine_norm(h, y0, y1, route, mod[5], norm1_w[layer + 1], next_mod[0], next_mod[1], tps,
                                      0, emit_h=True, a_dtype=BF16)
        mod = next_mod
    return out.reshape(batch, seq, d)
```

```python
import functools

import jax
import jax.numpy as jnp
from jax import lax
from jax.experimental import pallas as pl
from jax.experimental.pallas import tpu as pltpu

F32 = jnp.float32
BF16 = jnp.bfloat16

NORM_EPS = 1e-6
ROPE_BASE = 10000.0
GRID_W = 64
HEAD_DIM = 128
Q_HEADS = 8
KV_HEADS = 2
ATT_BLOCK = 128
M_HEADS = 4
M_QK = 128
M_V = 256
M_CHUNK = 128
CONV_SHORT = 3
CONV_LONG = 31
N_EXPERTS = 8
TOP_K = 2

ROW_TILE = 512
CONV_TILE = 256
HALO = 16
NEG = -1e30
V7X_VMEM_LIMIT = 56 * 1024 * 1024


def _params(n_axes):
    return pltpu.CompilerParams(dimension_semantics=("arbitrary",) * n_axes, vmem_limit_bytes=V7X_VMEM_LIMIT)


def _sigmoid(x):
    return 1.0 / (1.0 + jnp.exp(-x))


def _silu(x):
    return x * _sigmoid(x)


def _ada_kernel(c_ref, w_ref, b_ref, o_ref):
    s = _silu(c_ref[...]).astype(BF16)
    o_ref[...] = jnp.dot(s, w_ref[...].astype(BF16), preferred_element_type=F32) + b_ref[...]


def _ada_table(cvec, ada_w, ada_b, tn=1024):
    depth, d, n = ada_w.shape
    rows = cvec.shape[0]
    return pl.pallas_call(
        _ada_kernel,
        grid=(depth, n // tn),
        in_specs=[pl.BlockSpec((rows, d), lambda l, j: (0, 0)),
                  pl.BlockSpec((None, d, tn), lambda l, j: (l, 0, j)),
                  pl.BlockSpec((None, 1, tn), lambda l, j: (l, 0, j))],
        out_specs=pl.BlockSpec((None, rows, tn), lambda l, j: (l, 0, j)),
        out_shape=jax.ShapeDtypeStruct((depth, rows, n), F32),
        compiler_params=_params(2),
        name="ada_table",
    )(cvec, ada_w, ada_b.reshape(depth, 1, n))


def _normmod(x, w, shift, scale):
    ms = jnp.mean(x * x, axis=-1, keepdims=True)
    y = x * lax.rsqrt(ms + NORM_EPS) * w
    return y * (1.0 + scale) + shift


def _normmod_kernel(h_ref, w_ref, sh_ref, sc_ref, o_ref):
    o_ref[...] = _normmod(h_ref[...], w_ref[...], sh_ref[...], sc_ref[...]).astype(o_ref.dtype)


def _top2_route(a, rwh_ref, rwl_ref, rb_ref):
    a_hi = a.astype(BF16)
    a_lo = (a - a_hi.astype(F32)).astype(BF16)
    w_hi = rwh_ref[...]
    logits = (jnp.dot(a_hi, w_hi, preferred_element_type=F32)
              + (jnp.dot(a_lo, w_hi, preferred_element_type=F32)
                 + jnp.dot(a_hi, rwl_ref[...], preferred_element_type=F32))) + rb_ref[...]
    lane = lax.broadcasted_iota(jnp.int32, logits.shape, 1)
    logits = jnp.where(lane < N_EXPERTS, logits, NEG)
    big = jnp.int32(1 << 20)
    m1 = jnp.max(logits, axis=-1, keepdims=True)
    i1 = jnp.min(jnp.where(logits == m1, lane, big), axis=-1, keepdims=True)
    rest = jnp.where(lane == i1, NEG, logits)
    m2 = jnp.max(rest, axis=-1, keepdims=True)
    i2 = jnp.min(jnp.where(rest == m2, lane, big), axis=-1, keepdims=True)
    e2 = jnp.exp(m2 - m1)
    w1 = 1.0 / (1.0 + e2)
    w2 = e2 / (1.0 + e2)
    r = jnp.where(lane == 0, i1.astype(F32), 0.0)
    r = jnp.where(lane == 1, i2.astype(F32), r)
    r = jnp.where(lane == 2, w1, r)
    r = jnp.where(lane == 3, w2, r)
    return r


def _seg_index(tiles_per_seq):
    return lambda i: (i + tiles_per_seq - 1) // tiles_per_seq


def _norm_modulate(h, w, shift, scale, tiles_per_seq, out_dtype=BF16, row_offset_tiles=0, n_rows=None):
    t, d = h.shape
    n_rows = t if n_rows is None else n_rows
    seg = _seg_index(tiles_per_seq)
    off = row_offset_tiles
    return pl.pallas_call(
        _normmod_kernel,
        grid=(n_rows // ROW_TILE,),
        in_specs=[pl.BlockSpec((ROW_TILE, d), lambda i: (i + off, 0)),
                  pl.BlockSpec((1, d), lambda i: (0, 0)),
                  pl.BlockSpec((None, 1, d), lambda i: (seg(i + off), 0, 0)),
                  pl.BlockSpec((None, 1, d), lambda i: (seg(i + off), 0, 0))],
        out_specs=pl.BlockSpec((ROW_TILE, d), lambda i: (i, 0)),
        out_shape=jax.ShapeDtypeStruct((n_rows, d), out_dtype),
        compiler_params=_params(1),
        name="norm_modulate",
    )(h, w.reshape(1, d), shift, scale)


def _mm_kernel(x_ref, w_ref, o_ref, wb_ref):
    @pl.when(pl.program_id(1) == 0)
    def _():
        wb_ref[...] = w_ref[...].astype(BF16)

    o_ref[...] = jnp.dot(x_ref[...], wb_ref[...], preferred_element_type=F32).astype(o_ref.dtype)


def _matmul(x, w3, layer, n_cols, tn=512, out_dtype=F32):
    t, k = x.shape
    return pl.pallas_call(
        _mm_kernel,
        grid=(n_cols // tn, t // ROW_TILE),
        in_specs=[pl.BlockSpec((ROW_TILE, k), lambda j, i: (i, 0)),
                  pl.BlockSpec((None, k, tn), lambda j, i: (layer, 0, j))],
        out_specs=pl.BlockSpec((ROW_TILE, tn), lambda j, i: (i, j)),
        out_shape=jax.ShapeDtypeStruct((t, n_cols), out_dtype),
        scratch_shapes=[pltpu.VMEM((k, tn), BF16)],
        compiler_params=_params(2),
        name="matmul",
    )(x, w3)


def _mlstm_head_out_rows(hf_ref, hb_ref, op_ref, mw_ref):
    parts = []
    for hd in range(M_HEADS):
        sl = slice(hd * M_V, (hd + 1) * M_V)
        x = hf_ref[:, sl] + hb_ref[:, sl]
        mu = jnp.mean(x, axis=-1, keepdims=True)
        xc = x - mu
        var = jnp.mean(xc * xc, axis=-1, keepdims=True)
        y = xc * lax.rsqrt(var + NORM_EPS) * mw_ref[:, sl]
        parts.append((y * _sigmoid(op_ref[:, sl])).astype(BF16))
    return jnp.concatenate(parts, axis=1)


def _out_proj_kernel(*refs, route, head_out):
    refs = list(refs)
    if head_out:
        x1 = _mlstm_head_out_rows(*refs[:4])
        refs = refs[4:]
    else:
        x1 = refs.pop(0)[...]
    if route:
        (x2_ref, w1_ref, w2_ref, h_ref, g_ref, nw_ref, sh_ref, sc_ref, rwh_ref, rwl_ref, rb_ref,
         ho_ref, a_ref, r_ref) = refs
    else:
        x2_ref, w1_ref, w2_ref, h_ref, g_ref, nw_ref, sh_ref, sc_ref, ho_ref, a_ref = refs
    y = jnp.dot(x1, w1_ref[...], preferred_element_type=F32)
    y = y + jnp.dot(x2_ref[...], w2_ref[...], preferred_element_type=F32)
    hn = h_ref[...] + g_ref[...] * y
    ho_ref[...] = hn
    a = _normmod(hn, nw_ref[...], sh_ref[...], sc_ref[...])
    a_ref[...] = a.astype(a_ref.dtype)
    if route:
        r_ref[...] = _top2_route(a, rwh_ref, rwl_ref, rb_ref)


def _out_proj_norm(x1, x2, w3b, layer, h, gate, norm_w, shift, scale, tiles_per_seq, router=None):
    t, k2 = x2.shape
    k1 = k2
    d = w3b.shape[2]
    seg = _seg_index(tiles_per_seq)
    row = lambda w: pl.BlockSpec((ROW_TILE, w), lambda i: (i, 0))
    modv = pl.BlockSpec((None, 1, d), lambda i: (seg(i), 0, 0))
    head_out = isinstance(x1, tuple)
    if head_out:
        h_f, h_b, z_rest, mnorm_w = x1
        assert h_f.shape == (t, k1)
        x1_specs = [row(k1), row(k1), row(k1), pl.BlockSpec((1, k1), lambda i: (0, 0))]
        x1_args = [h_f, h_b, z_rest, mnorm_w.reshape(1, k1)]
    else:
        assert x1.shape == (t, k1)
        x1_specs, x1_args = [row(k1)], [x1]
    in_specs = x1_specs + [
        row(k2),
        pl.BlockSpec((None, k1, d), lambda i: (layer, 0, 0), pipeline_mode=pl.Buffered(1)),
        pl.BlockSpec((None, k2, d), lambda i: (layer, 1, 0), pipeline_mode=pl.Buffered(1)),
        row(d), modv, pl.BlockSpec((1, d), lambda i: (0, 0)), modv, modv]
    args = x1_args + [x2, w3b, w3b, h, gate, norm_w.reshape(1, d), shift, scale]
    out_specs = [row(d), row(d)]
    out_shape = [jax.ShapeDtypeStruct((t, d), F32), jax.ShapeDtypeStruct((t, d), BF16)]
    if router is not None:
        router_w, router_b = router
        rw = jnp.pad(router_w.astype(F32), ((0, 0), (0, 128 - N_EXPERTS)))
        rw_hi = rw.astype(BF16)
        rw_lo = (rw - rw_hi.astype(F32)).astype(BF16)
        in_specs += [pl.BlockSpec((d, 128), lambda i: (0, 0)), pl.BlockSpec((d, 128), lambda i: (0, 0)),
                     pl.BlockSpec((1, 128), lambda i: (0, 0))]
        args += [rw_hi, rw_lo, jnp.pad(router_b, (0, 128 - N_EXPERTS)).reshape(1, 128)]
        out_specs.append(row(128))
        out_shape.append(jax.ShapeDtypeStruct((t, 128), F32))
    return pl.pallas_call(
        functools.partial(_out_proj_kernel, route=router is not None, head_out=head_out),
        grid=(t // ROW_TILE,),
        in_specs=in_specs,
        out_specs=out_specs,
        out_shape=out_shape,
        compiler_params=_params(1),
        name="out_proj_norm",
    )(*args)


def _ffn_up_kernel(x_ref, wg_ref, wu_ref, o_ref, wgb_ref, wub_ref):
    @pl.when(pl.program_id(1) == 0)
    def _():
        wgb_ref[...] = wg_ref[...].astype(BF16)
        wub_ref[...] = wu_ref[...].astype(BF16)

    x = x_ref[...]
    g = jnp.dot(x, wgb_ref[...], preferred_element_type=F32)
    u = jnp.dot(x, wub_ref[...], preferred_element_type=F32)
    o_ref[...] = (_silu(g) * u).astype(o_ref.dtype)


def _ffn_up(x, w_gate, w_up, layer, tf=512):
    t, k = x.shape
    f = w_gate.shape[2]
    return pl.pallas_call(
        _ffn_up_kernel,
        grid=(f // tf, t // ROW_TILE),
        in_specs=[pl.BlockSpec((ROW_TILE, k), lambda j, i: (i, 0)),
                  pl.BlockSpec((None, k, tf), lambda j, i: (layer, 0, j)),
                  pl.BlockSpec((None, k, tf), lambda j, i: (layer, 0, j))],
        out_specs=pl.BlockSpec((ROW_TILE, tf), lambda j, i: (i, j)),
        out_shape=jax.ShapeDtypeStruct((t, f), BF16),
        scratch_shapes=[pltpu.VMEM((k, tf), BF16), pltpu.VMEM((k, tf), BF16)],
        compiler_params=_params(2),
        name="ffn_up",
    )(x, w_gate, w_up)


def _mm_resid_kernel(x_ref, w_ref, h_ref, g_ref, o_ref, wb_ref):
    @pl.when(pl.program_id(1) == 0)
    def _():
        wb_ref[...] = w_ref[...].astype(BF16)

    y = jnp.dot(x_ref[...], wb_ref[...], preferred_element_type=F32)
    o_ref[...] = h_ref[...] + g_ref[...] * y


def _ffn_down_residual(x, w3, layer, h, gate, tiles_per_seq, tn=512):
    t, k = x.shape
    n = w3.shape[2]
    seg = _seg_index(tiles_per_seq)
    return pl.pallas_call(
        _mm_resid_kernel,
        grid=(n // tn, t // ROW_TILE),
        in_specs=[pl.BlockSpec((ROW_TILE, k), lambda j, i: (i, 0)),
                  pl.BlockSpec((None, k, tn), lambda j, i: (layer, 0, j)),
                  pl.BlockSpec((ROW_TILE, tn), lambda j, i: (i, j)),
                  pl.BlockSpec((None, 1, tn), lambda j, i: (seg(i), 0, j))],
        out_specs=pl.BlockSpec((ROW_TILE, tn), lambda j, i: (i, j)),
        out_shape=jax.ShapeDtypeStruct((t, n), F32),
        scratch_shapes=[pltpu.VMEM((k, tn), BF16)],
        compiler_params=_params(2),
        name="ffn_down_residual",
    )(x, w3, h, gate)


def _rope(x, cs):
    x = x.astype(F32)
    return x * cs[:, :HEAD_DIM] + pltpu.roll(x, HEAD_DIM // 2, axis=1) * cs[:, HEAD_DIM:]


def _attend(q_all, k_parts, v_parts, sink_ref, o_ref, mask_fn):
    group = Q_HEADS // KV_HEADS
    scale = HEAD_DIM ** -0.5
    for g in range(KV_HEADS):
        qg = jnp.concatenate([q_all[:, (g * group + r) * HEAD_DIM:(g * group + r + 1) * HEAD_DIM]
                              for r in range(group)], axis=0).astype(BF16)
        kg = jnp.concatenate([kp[:, g * HEAD_DIM:(g + 1) * HEAD_DIM] for kp in k_parts], axis=0).astype(BF16)
        vg = jnp.concatenate([vp[:, g * HEAD_DIM:(g + 1) * HEAD_DIM] for vp in v_parts], axis=0).astype(BF16)
        s = lax.dot_general(qg, kg, (((1,), (1,)), ((), ())), preferred_element_type=F32) * scale
        if mask_fn is not None:
            s = jnp.where(mask_fn(s.shape), s, NEG)
        sink = sink_ref[g]
        m = jnp.maximum(jnp.max(s, axis=-1, keepdims=True), sink)
        p = jnp.exp(s - m)
        denom = jnp.sum(p, axis=-1, keepdims=True) + jnp.exp(sink - m)
        o = jnp.dot(p.astype(BF16), vg, preferred_element_type=F32) / denom
        for r in range(group):
            hq = g * group + r
            o_ref[:, hq * HEAD_DIM:(hq + 1) * HEAD_DIM] = o[r * ATT_BLOCK:(r + 1) * ATT_BLOCK].astype(o_ref.dtype)


def _attn_kernel(q_ref, kp_ref, kc_ref, kn_ref, kx_ref, vp_ref, vc_ref, vn_ref, vx_ref,
                 csp_ref, csc_ref, csn_ref, sink_ref, o_ref, *, n_blocks, ctx_blocks):
    step = pl.program_id(1)

    @pl.when(step < ctx_blocks)
    def _():
        _attend(q_ref[...], [kx_ref[...]], [vx_ref[...]], sink_ref, o_ref, None)

    @pl.when(step >= ctx_blocks)
    def _():
        _win_attn_body(q_ref, kp_ref, kc_ref, kn_ref, kx_ref, vp_ref, vc_ref, vn_ref, vx_ref,
                       csp_ref, csc_ref, csn_ref, sink_ref, o_ref, step - ctx_blocks, n_blocks)


def _win_attn_body(q_ref, kp_ref, kc_ref, kn_ref, kx_ref, vp_ref, vc_ref, vn_ref, vx_ref,
                   csp_ref, csc_ref, csn_ref, sink_ref, o_ref, n, n_blocks):
    csc = csc_ref[...]
    q_all = jnp.concatenate([_rope(q_ref[:, h * HEAD_DIM:(h + 1) * HEAD_DIM], csc) for h in range(Q_HEADS)], axis=1)

    def rope_kv(k_ref, cs):
        return jnp.concatenate([_rope(k_ref[:, g * HEAD_DIM:(g + 1) * HEAD_DIM], cs) for g in range(KV_HEADS)], axis=1)

    k_parts = [rope_kv(kp_ref, csp_ref[...]), rope_kv(kc_ref, csc), rope_kv(kn_ref, csn_ref[...]), kx_ref[...]]
    v_parts = [vp_ref[...], vc_ref[...], vn_ref[...], vx_ref[...]]

    def mask_fn(shape):
        row = lax.broadcasted_iota(jnp.int32, shape, 0) & (ATT_BLOCK - 1)
        col = lax.broadcasted_iota(jnp.int32, shape, 1)
        band = (col >= row) & (col <= row + 2 * ATT_BLOCK)
        ok_prev = (col >= ATT_BLOCK) | (n > 0)
        ok_next = (col < 2 * ATT_BLOCK) | (n < n_blocks - 1)
        return (band & ok_prev & ok_next) | (col >= 3 * ATT_BLOCK)

    _attend(q_all, k_parts, v_parts, sink_ref, o_ref, mask_fn)


def _sink_rows(sink):
    group = Q_HEADS // KV_HEADS
    return jnp.repeat(sink.astype(F32).reshape(KV_HEADS, group), ATT_BLOCK, axis=1).reshape(
        KV_HEADS, group * ATT_BLOCK, 1)


def _window_attention(z, cs, sink, batch, ctx_len, seq):
    t = z.shape[0]
    nb = seq // ATT_BLOCK
    cb = ctx_len // ATT_BLOCK
    base = batch * cb
    qw = Q_HEADS * HEAD_DIM
    kw = KV_HEADS * HEAD_DIM
    kcol = qw // kw
    vcol = kcol + 1
    sink_rows = _sink_rows(sink)

    def lat(s):
        return jnp.clip(s - cb, 0, nb - 1)

    def q_block(b, s):
        return jnp.where(s < cb, b * cb + s, base + b * nb + lat(s))

    def kv(shift, col):
        return pl.BlockSpec((ATT_BLOCK, kw), lambda b, s: (base + b * nb + lat(s + shift), col))

    def rot(shift):
        return pl.BlockSpec((ATT_BLOCK, 2 * HEAD_DIM), lambda b, s: (lat(s + shift), 0))

    return pl.pallas_call(
        functools.partial(_attn_kernel, n_blocks=nb, ctx_blocks=cb),
        grid=(batch, cb + nb),
        in_specs=[pl.BlockSpec((ATT_BLOCK, qw), lambda b, s: (q_block(b, s), 0)),
                  kv(-1, kcol), kv(0, kcol), kv(1, kcol),
                  pl.BlockSpec((ctx_len, kw), lambda b, s: (b, kcol)),
                  kv(-1, vcol), kv(0, vcol), kv(1, vcol),
                  pl.BlockSpec((ctx_len, kw), lambda b, s: (b, vcol)),
                  rot(-1), rot(0), rot(1),
                  pl.BlockSpec(sink_rows.shape, lambda b, s: (0, 0, 0))],
        out_specs=pl.BlockSpec((ATT_BLOCK, qw), lambda b, s: (q_block(b, s), 0)),
        out_shape=jax.ShapeDtypeStruct((t, qw), BF16),
        compiler_params=_params(2),
        name="window_attention",
    )(z, z, z, z, z, z, z, z, z, cs, cs, cs, sink_rows)


def _seq_edges(t, batch, ctx_tiles, lat_tiles):
    n_ctx = batch * ctx_tiles
    u = t - n_ctx
    is_ctx = t < n_ctx
    first = jnp.where(is_ctx, lax.rem(t, ctx_tiles) == 0, lax.rem(u, lat_tiles) == 0)
    last = jnp.where(is_ctx, lax.rem(t, ctx_tiles) == ctx_tiles - 1, lax.rem(u, lat_tiles) == lat_tiles - 1)
    return first, last


def _fill_ext(ext_ref, cur, prev, nxt, first, last):
    ext_ref[HALO:HALO + CONV_TILE, :] = cur
    ext_ref[0:HALO, :] = jnp.where(first, 0.0, prev)
    ext_ref[HALO + CONV_TILE:, :] = jnp.where(last, 0.0, nxt)


def _dwconv_rows(ext_ref, w_ref, taps, r0, n_rows):
    pad = taps // 2
    acc = None
    for k in range(taps):
        start = HALO + r0 + k - pad
        term = w_ref[k:k + 1, :] * ext_ref[start:start + n_rows, :]
        acc = term if acc is None else acc + term
    return acc


CONV_ROWS = 32


def _short_conv_kernel(bg_ref, cg_ref, xb_ref, cgp_ref, xbp_ref, cgn_ref, xbn_ref, w_ref, o_ref, ext_ref,
                       *, batch, ctx_tiles, lat_tiles):
    first, last = _seq_edges(pl.program_id(0), batch, ctx_tiles, lat_tiles)
    def prod(a_ref, b_ref):
        return a_ref[...].astype(F32) * b_ref[...].astype(F32)

    _fill_ext(ext_ref, prod(cg_ref, xb_ref), prod(cgp_ref, xbp_ref), prod(cgn_ref, xbn_ref), first, last)
    for r0 in range(0, CONV_TILE, CONV_ROWS):
        conv = _dwconv_rows(ext_ref, w_ref, CONV_SHORT, r0, CONV_ROWS)
        o_ref[r0:r0 + CONV_ROWS, :] = (bg_ref[r0:r0 + CONV_ROWS, :].astype(F32) * conv).astype(o_ref.dtype)


def _halo_specs(width, col, halo_per_tile, n_halo_blocks):
    prev = pl.BlockSpec((HALO, width), lambda t, c: (jnp.maximum(t * halo_per_tile - 1, 0), col(c)))
    nxt = pl.BlockSpec((HALO, width), lambda t, c: (jnp.minimum((t + 1) * halo_per_tile, n_halo_blocks - 1), col(c)))
    return prev, nxt


def _short_conv(z, conv_w3, layer, batch, ctx_len, seq):
    t = z.shape[0]
    cw = 512
    n_c = 1024 // cw
    bg0, cg0, xb0 = 1536 // cw, 2560 // cw, 3584 // cw
    hpt = CONV_TILE // HALO
    nh = t // HALO
    cgp, cgn = _halo_specs(cw, lambda c: cg0 + c, hpt, nh)
    xbp, xbn = _halo_specs(cw, lambda c: xb0 + c, hpt, nh)
    kern = functools.partial(_short_conv_kernel, batch=batch, ctx_tiles=ctx_len // CONV_TILE,
                             lat_tiles=seq // CONV_TILE)
    return pl.pallas_call(
        kern,
        grid=(t // CONV_TILE, n_c),
        in_specs=[pl.BlockSpec((CONV_TILE, cw), lambda t_, c: (t_, bg0 + c)),
                  pl.BlockSpec((CONV_TILE, cw), lambda t_, c: (t_, cg0 + c)),
                  pl.BlockSpec((CONV_TILE, cw), lambda t_, c: (t_, xb0 + c)),
                  cgp, xbp, cgn, xbn,
                  pl.BlockSpec((None, CONV_SHORT, cw), lambda t_, c: (layer, 0, c))],
        out_specs=pl.BlockSpec((CONV_TILE, cw), lambda t_, c: (t_, c)),
        out_shape=jax.ShapeDtypeStruct((t, 1024), BF16),
        scratch_shapes=[pltpu.VMEM((CONV_TILE + 2 * HALO, cw), F32)],
        compiler_params=_params(2),
        name="short_conv",
    )(z, z, z, z, z, z, z, conv_w3)


SUBLANES = 8
SHIFT_ROWS = CONV_TILE + 2 * HALO - SUBLANES


def _dwconv_rows_aligned(ext_ref, sh_ref, w8_ref, taps, r0, n_rows):
    pad = taps // 2
    width = ext_ref.shape[1]
    acc = None
    for k in range(taps):
        start = HALO + r0 + k - pad
        b = start % SUBLANES
        a8 = start - b
        src = ext_ref[a8:a8 + n_rows, :] if b == 0 else sh_ref[b - 1, a8:a8 + n_rows, :]
        term = (src.reshape(n_rows // SUBLANES, SUBLANES, width) * w8_ref[k][None]).reshape(n_rows, width)
        acc = term if acc is None else acc + term
    return acc


def _conformer_kernel(a_ref, g_ref, ap_ref, gp_ref, an_ref, gn_ref, w8_ref, b_ref, lw_ref, lb_ref, o_ref,
                      ext_ref, sh_ref, *, batch, ctx_tiles, lat_tiles):
    first, last = _seq_edges(pl.program_id(0), batch, ctx_tiles, lat_tiles)
    _fill_ext(ext_ref, a_ref[...] * _sigmoid(g_ref[...]), ap_ref[...] * _sigmoid(gp_ref[...]),
              an_ref[...] * _sigmoid(gn_ref[...]), first, last)
    for b in range(1, SUBLANES):
        sh_ref[b - 1] = ext_ref[b:b + SHIFT_ROWS, :]
    for r0 in range(0, CONV_TILE, CONV_ROWS):
        u = _dwconv_rows_aligned(ext_ref, sh_ref, w8_ref, CONV_LONG, r0, CONV_ROWS) + b_ref[...]
        mu = jnp.mean(u, axis=-1, keepdims=True)
        uc = u - mu
        var = jnp.mean(uc * uc, axis=-1, keepdims=True)
        y = uc * lax.rsqrt(var + NORM_EPS) * lw_ref[...] + lb_ref[...]
        o_ref[r0:r0 + CONV_ROWS, :] = _silu(y).astype(o_ref.dtype)


def _conformer_conv(z, a_col, g_col, dw_w3, dw_b, ln_w, ln_b, layer, batch, ctx_len, seq):
    t = z.shape[0]
    cw = 1024
    hpt = CONV_TILE // HALO
    nh = t // HALO
    ap, an = _halo_specs(cw, lambda c: a_col, hpt, nh)
    gp, gn = _halo_specs(cw, lambda c: g_col, hpt, nh)
    kern = functools.partial(_conformer_kernel, batch=batch, ctx_tiles=ctx_len // CONV_TILE,
                             lat_tiles=seq // CONV_TILE)
    vec = pl.BlockSpec((None, 1, cw), lambda t_, c: (layer, 0, 0))
    n_layers = dw_b.shape[0]
    return pl.pallas_call(
        kern,
        grid=(t // CONV_TILE, 1),
        in_specs=[pl.BlockSpec((CONV_TILE, cw), lambda t_, c: (t_, a_col)),
                  pl.BlockSpec((CONV_TILE, cw), lambda t_, c: (t_, g_col)),
                  ap, gp, an, gn,
                  pl.BlockSpec((None, CONV_LONG, SUBLANES, cw), lambda t_, c: (layer, 0, 0, 0)),
                  vec, vec, vec],
        out_specs=pl.BlockSpec((CONV_TILE, cw), lambda t_, c: (t_, 0)),
        out_shape=jax.ShapeDtypeStruct((t, cw), BF16),
        scratch_shapes=[pltpu.VMEM((CONV_TILE + 2 * HALO, cw), F32),
                        pltpu.VMEM((SUBLANES - 1, SHIFT_ROWS, cw), F32)],
        compiler_params=_params(2),
        name="conformer_conv",
    )(z, z, z, z, z, z, jnp.broadcast_to(dw_w3[:, :, None, :], (n_layers, CONV_LONG, SUBLANES, cw)),
      dw_b.reshape(n_layers, 1, cw), ln_w.reshape(n_layers, 1, cw), ln_b.reshape(n_layers, 1, cw))


def _log_sigmoid(x):
    return jnp.minimum(x, 0.0) - jnp.log(1.0 + jnp.exp(-jnp.abs(x)))


def _mlstm_kernel(qf_ref, kf_ref, vf_ref, gcf_ref, grf_ref, qb_ref, kb_ref, vb_ref, gcb_ref, grb_ref,
                  bc_ref, br_ref, of_ref, ob_ref, *state_refs):
    n_chain = 2 * M_HEADS
    cta_ref, m_ref = state_refs[:n_chain], state_refs[n_chain:]

    @pl.when(pl.program_id(1) == 0)
    def _():
        for ref in state_refs:
            ref[...] = jnp.zeros_like(ref)

    ln = M_CHUNK
    row = lax.broadcasted_iota(jnp.int32, (ln, ln), 0)
    col = lax.broadcasted_iota(jnp.int32, (ln, ln), 1)
    lower = col <= row
    upper = col >= row
    _mlstm_direction(qf_ref, kf_ref, vf_ref, gcf_ref[...] + bc_ref[0], grf_ref[...] + br_ref[0], lower, upper,
                     of_ref, cta_ref, m_ref, 0)
    _mlstm_direction(qb_ref, kb_ref, vb_ref, gcb_ref[...] + bc_ref[1], grb_ref[...] + br_ref[1], upper, lower,
                     ob_ref, cta_ref, m_ref, M_HEADS)


def _mlstm_direction(q_ref, k_ref, v_ref, gc, gr, mask, mask_t, o_ref, cta_ref, m_ref, state0):
    ln = M_CHUNK
    one_col = lax.broadcasted_iota(jnp.int32, (ln, 128), 1) == 0
    ones_blk = jnp.where(one_col, 1.0, 0.0).astype(BF16)
    for h in range(M_HEADS):
        st = state0 + h
        ig_c = gc[:, h:h + 1]
        lf_c = _log_sigmoid(gc[:, M_HEADS + h:M_HEADS + h + 1])
        ig_r = gr[h:h + 1, :]
        lf_r = _log_sigmoid(gr[M_HEADS + h:M_HEADS + h + 1, :])
        qb = (q_ref[:, h * M_QK:(h + 1) * M_QK] * (M_QK ** -0.5)).astype(BF16)
        k = k_ref[:, h * M_QK:(h + 1) * M_QK]
        v = v_ref[:, h * M_V:(h + 1) * M_V]
        cum_c = jnp.sum(jnp.where(mask, lf_r, 0.0), axis=1, keepdims=True)
        cum_r = jnp.sum(jnp.where(mask_t, lf_c, 0.0), axis=0, keepdims=True)
        total = jnp.sum(lf_r, axis=1, keepdims=True)
        m_s = m_ref[st][...]
        g = jnp.where(mask, ig_r - cum_r, NEG)
        mm = jnp.maximum(m_s, jnp.max(g, axis=1, keepdims=True))
        w_inter = jnp.exp(m_s - mm)
        s = lax.dot_general(qb, k.astype(BF16), (((1,), (1,)), ((), ())), preferred_element_type=F32) * jnp.exp(g - mm)
        cta = cta_ref[st][...]
        va = jnp.concatenate([v.astype(BF16), ones_blk], axis=1)
        inter = jnp.dot(qb, cta.astype(BF16), preferred_element_type=F32)
        intra = jnp.dot(s.astype(BF16), va, preferred_element_type=F32)
        num = w_inter * inter[:, :M_V] + intra[:, :M_V]
        den = w_inter * inter[:, M_V:M_V + 1] + intra[:, M_V:M_V + 1]
        o_ref[:, h * M_V:(h + 1) * M_V] = num / jnp.maximum(jnp.abs(den), jnp.exp(-(cum_c + mm)))
        gl = total - cum_c + ig_c
        m_new = jnp.maximum(total + m_s, jnp.max(gl, axis=0, keepdims=True))
        wg = jnp.exp(gl - m_new)
        decay = jnp.exp(total + m_s - m_new)
        wva = jnp.concatenate([(wg * v).astype(BF16), jnp.where(one_col, wg, 0.0).astype(BF16)], axis=1)
        cta_ref[st][...] = decay * cta + jnp.dot(k.astype(BF16).T, wva, preferred_element_type=F32)
        m_ref[st][...] = m_new


def _mlstm(zq, gates, gate_b, batch, ctx_len, seq):
    t = zq.shape[0]
    cb = ctx_len // M_CHUNK
    nc = seq // M_CHUNK
    base = batch * cb
    steps = cb + nc
    half = 2 * M_HEADS
    g_dir = gates.reshape(t, 2, half).transpose(1, 0, 2)
    g_dir_t = g_dir.transpose(0, 2, 1)
    b_dir = gate_b.astype(F32).reshape(2, 1, half)
    b_dir_t = b_dir.transpose(0, 2, 1)

    def rb_f(b, i):
        return jnp.where(i < cb, b * cb + i, base + b * nc + (i - cb))

    def rb_b(b, i):
        return jnp.where(i < cb, b * cb + (cb - 1 - i), base + b * nc + (nc - 1 - (i - cb)))

    qk_w = M_HEADS * M_QK
    v_w = M_HEADS * M_V

    def chunk_specs(d, rb):
        return [pl.BlockSpec((M_CHUNK, qk_w), lambda b, i: (rb(b, i), 0)),
                pl.BlockSpec((M_CHUNK, qk_w), lambda b, i: (rb(b, i), 1)),
                pl.BlockSpec((M_CHUNK, v_w), lambda b, i: (rb(b, i), 1)),
                pl.BlockSpec((None, M_CHUNK, half), lambda b, i: (d, rb(b, i), 0)),
                pl.BlockSpec((None, half, M_CHUNK), lambda b, i: (d, 0, rb(b, i)))]

    return pl.pallas_call(
        _mlstm_kernel,
        grid=(batch, steps),
        in_specs=chunk_specs(0, rb_f) + chunk_specs(1, rb_b) + [
            pl.BlockSpec((2, 1, half), lambda b, i: (0, 0, 0)),
            pl.BlockSpec((2, half, 1), lambda b, i: (0, 0, 0))],
        out_specs=[pl.BlockSpec((M_CHUNK, v_w), lambda b, i: (rb_f(b, i), 0)),
                   pl.BlockSpec((M_CHUNK, v_w), lambda b, i: (rb_b(b, i), 0))],
        out_shape=[jax.ShapeDtypeStruct((t, v_w), F32), jax.ShapeDtypeStruct((t, v_w), F32)],
        scratch_shapes=([pltpu.VMEM((M_QK, M_V + 128), F32)] * (2 * M_HEADS)
                        + [pltpu.VMEM((1, 1), F32)] * (2 * M_HEADS)),
        compiler_params=_params(2),
        name="mlstm",
    )(zq, zq, zq, g_dir, g_dir_t, zq, zq, zq, g_dir, g_dir_t, b_dir, b_dir_t)


def _fresh_weights(be_ref, rb):
    return (rb == 0) | (be_ref[rb] != be_ref[jnp.maximum(rb - 1, 0)])


def _moe_up_kernel(be_ref, nv_ref, x_ref, wg_ref, wu_ref, o_ref, wgb_ref, wub_ref):
    rb = pl.program_id(1)
    valid = rb < nv_ref[0]

    @pl.when(valid & _fresh_weights(be_ref, rb))
    def _():
        wgb_ref[...] = wg_ref[...].astype(BF16)
        wub_ref[...] = wu_ref[...].astype(BF16)

    @pl.when(valid)
    def _():
        x = x_ref[...]
        g = jnp.dot(x, wgb_ref[...], preferred_element_type=F32)
        u = jnp.dot(x, wub_ref[...], preferred_element_type=F32)
        o_ref[...] = (_silu(g) * u).astype(o_ref.dtype)

    @pl.when(jnp.logical_not(valid))
    def _():
        o_ref[...] = jnp.zeros_like(o_ref)


def _moe_down_kernel(be_ref, nv_ref, x_ref, w_ref, o_ref, wb_ref):
    rb = pl.program_id(1)
    valid = rb < nv_ref[0]

    @pl.when(valid & _fresh_weights(be_ref, rb))
    def _():
        wb_ref[...] = w_ref[...].astype(BF16)

    @pl.when(valid)
    def _():
        o_ref[...] = jnp.dot(x_ref[...], wb_ref[...], preferred_element_type=F32).astype(o_ref.dtype)

    @pl.when(jnp.logical_not(valid))
    def _():
        o_ref[...] = jnp.zeros_like(o_ref)


def _moe_experts(x_sorted, block_expert, n_valid, w_gate, w_up, w_down, layer, tf=512, tn=512):
    p, d = x_sorted.shape
    f = w_gate.shape[3]
    nblk = p // ROW_TILE

    def row(rb, nv):
        return jnp.minimum(rb, nv[0] - 1)

    g = pl.pallas_call(
        _moe_up_kernel,
        grid_spec=pltpu.PrefetchScalarGridSpec(
            num_scalar_prefetch=2,
            grid=(f // tf, nblk),
            in_specs=[pl.BlockSpec((ROW_TILE, d), lambda j, rb, be, nv: (row(rb, nv), 0)),
                      pl.BlockSpec((None, None, d, tf), lambda j, rb, be, nv: (layer, be[rb], 0, j)),
                      pl.BlockSpec((None, None, d, tf), lambda j, rb, be, nv: (layer, be[rb], 0, j))],
            out_specs=pl.BlockSpec((ROW_TILE, tf), lambda j, rb, be, nv: (rb, j)),
            scratch_shapes=[pltpu.VMEM((d, tf), BF16), pltpu.VMEM((d, tf), BF16)]),
        out_shape=jax.ShapeDtypeStruct((p, f), BF16),
        compiler_params=_params(2),
        name="moe_up",
    )(block_expert, n_valid, x_sorted, w_gate, w_up)
    return pl.pallas_call(
        _moe_down_kernel,
        grid_spec=pltpu.PrefetchScalarGridSpec(
            num_scalar_prefetch=2,
            grid=(d // tn, nblk),
            in_specs=[pl.BlockSpec((ROW_TILE, f), lambda j, rb, be, nv: (row(rb, nv), 0)),
                      pl.BlockSpec((None, None, f, tn), lambda j, rb, be, nv: (layer, be[rb], 0, j))],
            out_specs=pl.BlockSpec((ROW_TILE, tn), lambda j, rb, be, nv: (rb, j)),
            scratch_shapes=[pltpu.VMEM((f, tn), BF16)]),
        out_shape=jax.ShapeDtypeStruct((p, d), BF16),
        compiler_params=_params(2),
        name="moe_down",
    )(block_expert, n_valid, g, w_down)


def _routing_plan(route):
    t = route.shape[0]
    tm = ROW_TILE
    n_assign = TOP_K * t
    nblk = (n_assign + N_EXPERTS * (tm - 1)) // tm
    e = route[:, :TOP_K].astype(jnp.int32).reshape(-1)
    onehot = (e[:, None] == jnp.arange(N_EXPERTS, dtype=jnp.int32)[None, :]).astype(jnp.int32)
    csum = jnp.cumsum(onehot, axis=0)
    counts = csum[-1]
    padded = ((counts + tm - 1) // tm) * tm
    ends = jnp.cumsum(padded)
    starts = ends - padded
    dest = jnp.sum((csum - onehot + starts[None, :]) * onehot, axis=1)
    src_tok = jnp.zeros((nblk * tm,), jnp.int32).at[dest].set(
        jnp.arange(n_assign, dtype=jnp.int32) // TOP_K, unique_indices=True, mode="promise_in_bounds")
    n_valid = (ends[-1] // tm).astype(jnp.int32)
    blk_start = jnp.arange(nblk, dtype=jnp.int32) * tm
    be = jnp.sum((ends[None, :] <= blk_start[:, None]).astype(jnp.int32), axis=1)
    be = jnp.minimum(be, N_EXPERTS - 1).astype(jnp.int32)
    return src_tok, be, n_valid.reshape(1), dest.reshape(t, TOP_K)


def _combine_kernel(h_ref, y0_ref, y1_ref, r_ref, g_ref, nw_ref, sh_ref, sc_ref, *out_refs):
    r = r_ref[...]
    moe = r[:, TOP_K:TOP_K + 1] * y0_ref[...].astype(F32) + r[:, TOP_K + 1:TOP_K + 2] * y1_ref[...].astype(F32)
    hn = h_ref[...] + g_ref[...] * moe
    if len(out_refs) == 2:
        out_refs[0][...] = hn
    out_refs[-1][...] = _normmod(hn, nw_ref[...], sh_ref[...], sc_ref[...]).astype(out_refs[-1].dtype)


def _combine_norm(h, y0, y1, route, gate, norm_w, shift, scale, tiles_per_seq, row_offset_tiles, emit_h, a_dtype):
    d = h.shape[1]
    n_rows = y0.shape[0]
    off = row_offset_tiles
    seg = _seg_index(tiles_per_seq)
    shifted = lambda w: pl.BlockSpec((ROW_TILE, w), lambda i: (i + off, 0))
    tile = pl.BlockSpec((ROW_TILE, d), lambda i: (i, 0))
    modv = pl.BlockSpec((None, 1, d), lambda i: (seg(i + off), 0, 0))
    out_specs = [tile, tile] if emit_h else [tile]
    out_shape = [jax.ShapeDtypeStruct((n_rows, d), a_dtype)]
    if emit_h:
        out_shape.insert(0, jax.ShapeDtypeStruct((n_rows, d), F32))
    return pl.pallas_call(
        _combine_kernel,
        grid=(n_rows // ROW_TILE,),
        in_specs=[shifted(d), tile, tile, shifted(128), modv, pl.BlockSpec((1, d), lambda i: (0, 0)), modv, modv],
        out_specs=out_specs,
        out_shape=out_shape,
        compiler_params=_params(1),
        name="moe_combine_norm",
    )(h, y0, y1, route, gate, norm_w.reshape(1, d), shift, scale)


def _rope_table(seq):
    rows = seq // GRID_W
    row = jnp.repeat(jnp.arange(rows, dtype=F32), GRID_W)
    col = jnp.tile(jnp.arange(GRID_W, dtype=F32), rows)
    n_freq = HEAD_DIM // 4
    inv_freq = ROPE_BASE ** (-jnp.arange(n_freq, dtype=F32) / n_freq)
    ang = jnp.concatenate([row[:, None] * inv_freq, col[:, None] * inv_freq], axis=-1)
    cos, sin = jnp.cos(ang), jnp.sin(ang)
    return jnp.concatenate([cos, cos, -sin, sin], axis=-1)


def kernel(x, c, ctx, c_ctx, ada_w, ada_b, norm1_w, norm2_w, ev_w_in, ev_sink, ev_conv_w, ev_w_out, ffn_w_gate, ffn_w_up, ffn_w_down, od_w_in, od_gate_b, od_mnorm_w, od_dw_w, od_dw_b, od_ln_w, od_ln_b, od_w_out, moe_router_w, moe_router_b, moe_w_gate, moe_w_up, moe_w_down, final_w):
    batch, seq, d = x.shape
    ctx_len = ctx.shape[1]
    depth = ada_w.shape[0]
    assert batch * ctx_len == ROW_TILE and seq % ROW_TILE == 0 and ctx_len % CONV_TILE == 0
    assert 1 + batch <= 8
    tps = seq // ROW_TILE
    n_ctx_rows = batch * ctx_len

    cs = _rope_table(seq)
    cvec = jnp.zeros((8, d), F32).at[0].set(c_ctx).at[1:1 + batch].set(c)
    mod_table = _ada_table(cvec, ada_w, ada_b)
    h = jnp.concatenate([ctx.reshape(n_ctx_rows, d), x.reshape(batch * seq, d)], axis=0)

    def mods_of(layer):
        mods = mod_table[layer, :1 + batch].reshape(1 + batch, 6, 1, d)
        return [mods[:, i] for i in range(6)]

    zero_mod = jnp.zeros((1 + batch, 1, d), F32)
    ctx_tiles = n_ctx_rows // ROW_TILE
    ev_w_out_b = ev_w_out.astype(BF16)
    od_w_out_b = od_w_out.astype(BF16)
    def take(rows, idx):
        return rows.at[idx].get(mode="promise_in_bounds")

    out = None
    mod = mods_of(0)
    a1 = _norm_modulate(h, norm1_w[0], mod[0], mod[1], tps)
    for layer in range(depth):
        j = layer // 2
        last = layer == depth - 1
        next_mod = None if last else mods_of(layer + 1)
        if layer % 2 == 0:
            z = _matmul(a1, ev_w_in, j, ev_w_in.shape[2], tn=1536, out_dtype=BF16)
            o_a = _window_attention(z, cs, ev_sink[j], batch, ctx_len, seq)
            o_b = _short_conv(z, ev_conv_w, j, batch, ctx_len, seq)
            h, a2 = _out_proj_norm(o_a, o_b, ev_w_out_b, j, h, mod[2], norm2_w[layer], mod[3], mod[4], tps)
            g = _ffn_up(a2, ffn_w_gate, ffn_w_up, j)
            h = _ffn_down_residual(g, ffn_w_down, j, h, mod[5], tps)
            if last:
                out = _norm_modulate(h, final_w, zero_mod, zero_mod, tps, out_dtype=F32,
                                     row_offset_tiles=ctx_tiles, n_rows=batch * seq)
            else:
                a1 = _norm_modulate(h, norm1_w[layer + 1], next_mod[0], next_mod[1], tps)
        else:
            qkv_w = M_HEADS * (2 * M_QK + M_V)
            n_gate = 4 * M_HEADS
            zq = _matmul(a1, od_w_in, j, qkv_w, tn=1024)
            w_gate_cols = jnp.pad(od_w_in[j, :, qkv_w:qkv_w + n_gate], ((0, 0), (0, 128 - n_gate)))[None]
            gates = _matmul(a1, w_gate_cols, 0, 128, tn=128)[:, :n_gate]
            w_rest = od_w_in[j, :, qkv_w + n_gate:][None]
            z_rest = _matmul(a1, w_rest, 0, w_rest.shape[2], tn=1536)
            h_f, h_b = _mlstm(zq, gates, od_gate_b[j], batch, ctx_len, seq)
            u_out = _conformer_conv(z_rest, 1, 2, od_dw_w, od_dw_b, od_ln_w, od_ln_b, j, batch, ctx_len, seq)
            h, a2, route = _out_proj_norm((h_f, h_b, z_rest, od_mnorm_w[j]), u_out, od_w_out_b, j, h, mod[2],
                                          norm2_w[layer], mod[3], mod[4], tps,
                                          router=(moe_router_w[j], moe_router_b[j]))
            row0 = n_ctx_rows if last else 0
            src_tok, block_expert, n_valid, pos = _routing_plan(route[row0:])
            x_sorted = jnp.take(a2, src_tok + row0, axis=0, mode="clip")
            y = _moe_experts(x_sorted, block_expert, n_valid, moe_w_gate, moe_w_up, moe_w_down, j)
            y0, y1 = take(y, pos[:, 0]), take(y, pos[:, 1])
            if last:
                out = _combine_norm(h, y0, y1, route, mod[5], final_w, zero_mod, zero_mod, tps, ctx_tiles,
                                    emit_h=False, a_dtype=F32)[0]
            else:
                h, a1 = _combine_norm(h, y0, y1, route, mod[5], norm1_w[layer + 1], next_mod[0], next_mod[1], tps,
                                      0, emit_h=True, a_dtype=BF16)
        mod = next_mod
    return out.reshape(batch, seq, d)
```

```python
import functools

import jax
import jax.numpy as jnp
from jax import lax
from jax.experimental import pallas as pl
from jax.experimental.pallas import tpu as pltpu

F32 = jnp.float32
BF16 = jnp.bfloat16

NORM_EPS = 1e-6
ROPE_BASE = 10000.0
GRID_W = 64
HEAD_DIM = 128
Q_HEADS = 8
KV_HEADS = 2
ATT_BLOCK = 128
M_HEADS = 4
M_QK = 128
M_V = 256
M_CHUNK = 128
CONV_SHORT = 3
CONV_LONG = 31
N_EXPERTS = 8
TOP_K = 2

ROW_TILE = 512
CONV_TILE = 256
HALO = 16
NEG = -1e30
V7X_VMEM_LIMIT = 56 * 1024 * 1024


def _params(n_axes):
    return pltpu.CompilerParams(dimension_semantics=("arbitrary",) * n_axes, vmem_limit_bytes=V7X_VMEM_LIMIT)


def _sigmoid(x):
    return 1.0 / (1.0 + jnp.exp(-x))


def _silu(x):
    return x * _sigmoid(x)


def _ada_kernel(c_ref, w_ref, b_ref, o_ref):
    s = _silu(c_ref[...]).astype(BF16)
    o_ref[...] = jnp.dot(s, w_ref[...].astype(BF16), preferred_element_type=F32) + b_ref[...]


def _ada_table(cvec, ada_w, ada_b, tn=1024):
    depth, d, n = ada_w.shape
    rows = cvec.shape[0]
    return pl.pallas_call(
        _ada_kernel,
        grid=(depth, n // tn),
        in_specs=[pl.BlockSpec((rows, d), lambda l, j: (0, 0)),
                  pl.BlockSpec((None, d, tn), lambda l, j: (l, 0, j)),
                  pl.BlockSpec((None, 1, tn), lambda l, j: (l, 0, j))],
        out_specs=pl.BlockSpec((None, rows, tn), lambda l, j: (l, 0, j)),
        out_shape=jax.ShapeDtypeStruct((depth, rows, n), F32),
        compiler_params=_params(2),
        name="ada_table",
    )(cvec, ada_w, ada_b.reshape(depth, 1, n))


def _normmod(x, w, shift, scale):
    ms = jnp.mean(x * x, axis=-1, keepdims=True)
    y = x * lax.rsqrt(ms + NORM_EPS) * w
    return y * (1.0 + scale) + shift


def _normmod_kernel(h_ref, w_ref, sh_ref, sc_ref, o_ref):
    o_ref[...] = _normmod(h_ref[...], w_ref[...], sh_ref[...], sc_ref[...]).astype(o_ref.dtype)


def _embed_kernel(ctx_ref, x_ref, w_ref, sh_ref, sc_ref, h_ref, a_ref):
    rows = jnp.where(pl.program_id(0) == 0, ctx_ref[...], x_ref[...])
    h_ref[...] = rows
    a_ref[...] = _normmod(rows, w_ref[...], sh_ref[...], sc_ref[...]).astype(a_ref.dtype)


def _embed_norm(ctx_rows, x_rows, w, shift, scale, tiles_per_seq):
    n_ctx, d = ctx_rows.shape
    assert n_ctx == ROW_TILE
    t = n_ctx + x_rows.shape[0]
    seg = _seg_index(tiles_per_seq)
    tile = pl.BlockSpec((ROW_TILE, d), lambda i: (i, 0))
    modv = pl.BlockSpec((None, 1, d), lambda i: (seg(i), 0, 0))
    return pl.pallas_call(
        _embed_kernel,
        grid=(t // ROW_TILE,),
        in_specs=[pl.BlockSpec((ROW_TILE, d), lambda i: (0, 0)),
                  pl.BlockSpec((ROW_TILE, d), lambda i: (jnp.maximum(i - 1, 0), 0)),
                  pl.BlockSpec((1, d), lambda i: (0, 0)), modv, modv],
        out_specs=[tile, tile],
        out_shape=[jax.ShapeDtypeStruct((t, d), F32), jax.ShapeDtypeStruct((t, d), BF16)],
        compiler_params=_params(1),
        name="embed_norm",
    )(ctx_rows, x_rows, w.reshape(1, d), shift, scale)


def _top2_route(a, rwh_ref, rwl_ref, rb_ref):
    a_hi = a.astype(BF16)
    a_lo = (a - a_hi.astype(F32)).astype(BF16)
    w_hi = rwh_ref[...]
    logits = (jnp.dot(a_hi, w_hi, preferred_element_type=F32)
              + (jnp.dot(a_lo, w_hi, preferred_element_type=F32)
                 + jnp.dot(a_hi, rwl_ref[...], preferred_element_type=F32))) + rb_ref[...]
    lane = lax.broadcasted_iota(jnp.int32, logits.shape, 1)
    logits = jnp.where(lane < N_EXPERTS, logits, NEG)
    big = jnp.int32(1 << 20)
    m1 = jnp.max(logits, axis=-1, keepdims=True)
    i1 = jnp.min(jnp.where(logits == m1, lane, big), axis=-1, keepdims=True)
    rest = jnp.where(lane == i1, NEG, logits)
    m2 = jnp.max(rest, axis=-1, keepdims=True)
    i2 = jnp.min(jnp.where(rest == m2, lane, big), axis=-1, keepdims=True)
    e2 = jnp.exp(m2 - m1)
    w1 = 1.0 / (1.0 + e2)
    w2 = e2 / (1.0 + e2)
    r = jnp.where(lane == 0, i1.astype(F32), 0.0)
    r = jnp.where(lane == 1, i2.astype(F32), r)
    r = jnp.where(lane == 2, w1, r)
    r = jnp.where(lane == 3, w2, r)
    return r


def _seg_index(tiles_per_seq):
    return lambda i: (i + tiles_per_seq - 1) // tiles_per_seq


def _norm_modulate(h, w, shift, scale, tiles_per_seq, out_dtype=BF16, row_offset_tiles=0, n_rows=None):
    t, d = h.shape
    n_rows = t if n_rows is None else n_rows
    seg = _seg_index(tiles_per_seq)
    off = row_offset_tiles
    return pl.pallas_call(
        _normmod_kernel,
        grid=(n_rows // ROW_TILE,),
        in_specs=[pl.BlockSpec((ROW_TILE, d), lambda i: (i + off, 0)),
                  pl.BlockSpec((1, d), lambda i: (0, 0)),
                  pl.BlockSpec((None, 1, d), lambda i: (seg(i + off), 0, 0)),
                  pl.BlockSpec((None, 1, d), lambda i: (seg(i + off), 0, 0))],
        out_specs=pl.BlockSpec((ROW_TILE, d), lambda i: (i, 0)),
        out_shape=jax.ShapeDtypeStruct((n_rows, d), out_dtype),
        compiler_params=_params(1),
        name="norm_modulate",
    )(h, w.reshape(1, d), shift, scale)


def _mm_kernel(x_ref, w_ref, o_ref, wb_ref):
    @pl.when(pl.program_id(1) == 0)
    def _():
        wb_ref[...] = w_ref[...].astype(BF16)

    o_ref[...] = jnp.dot(x_ref[...], wb_ref[...], preferred_element_type=F32).astype(o_ref.dtype)


def _matmul(x, w3, layer, n_cols, tn=512, out_dtype=F32):
    t, k = x.shape
    return pl.pallas_call(
        _mm_kernel,
        grid=(n_cols // tn, t // ROW_TILE),
        in_specs=[pl.BlockSpec((ROW_TILE, k), lambda j, i: (i, 0)),
                  pl.BlockSpec((None, k, tn), lambda j, i: (layer, 0, j))],
        out_specs=pl.BlockSpec((ROW_TILE, tn), lambda j, i: (i, j)),
        out_shape=jax.ShapeDtypeStruct((t, n_cols), out_dtype),
        scratch_shapes=[pltpu.VMEM((k, tn), BF16)],
        compiler_params=_params(2),
        name="matmul",
    )(x, w3)


def _mlstm_head_out_rows(hf_ref, hb_ref, op_ref, mw_ref):
    parts = []
    for hd in range(M_HEADS):
        sl = slice(hd * M_V, (hd + 1) * M_V)
        x = hf_ref[:, sl] + hb_ref[:, sl]
        mu = jnp.mean(x, axis=-1, keepdims=True)
        xc = x - mu
        var = jnp.mean(xc * xc, axis=-1, keepdims=True)
        y = xc * lax.rsqrt(var + NORM_EPS) * mw_ref[:, sl]
        parts.append((y * _sigmoid(op_ref[:, sl])).astype(BF16))
    return jnp.concatenate(parts, axis=1)


def _out_proj_kernel(*refs, route, head_out):
    refs = list(refs)
    if head_out:
        x1 = _mlstm_head_out_rows(*refs[:4])
        refs = refs[4:]
    else:
        x1 = refs.pop(0)[...]
    if route:
        (x2_ref, w1_ref, w2_ref, h_ref, g_ref, nw_ref, sh_ref, sc_ref, rwh_ref, rwl_ref, rb_ref,
         ho_ref, a_ref, r_ref) = refs
    else:
        x2_ref, w1_ref, w2_ref, h_ref, g_ref, nw_ref, sh_ref, sc_ref, ho_ref, a_ref = refs
    y = jnp.dot(x1, w1_ref[...], preferred_element_type=F32)
    y = y + jnp.dot(x2_ref[...], w2_ref[...], preferred_element_type=F32)
    hn = h_ref[...] + g_ref[...] * y
    ho_ref[...] = hn
    a = _normmod(hn, nw_ref[...], sh_ref[...], sc_ref[...])
    a_ref[...] = a.astype(a_ref.dtype)
    if route:
        r_ref[...] = _top2_route(a, rwh_ref, rwl_ref, rb_ref)


def _out_proj_norm(x1, x2, w3b, layer, h, gate, norm_w, shift, scale, tiles_per_seq, router=None):
    t, k2 = x2.shape
    k1 = k2
    d = w3b.shape[2]
    seg = _seg_index(tiles_per_seq)
    row = lambda w: pl.BlockSpec((ROW_TILE, w), lambda i: (i, 0))
    modv = pl.BlockSpec((None, 1, d), lambda i: (seg(i), 0, 0))
    head_out = isinstance(x1, tuple)
    if head_out:
        h_f, h_b, z_rest, mnorm_w = x1
        assert h_f.shape == (t, k1)
        x1_specs = [row(k1), row(k1), row(k1), pl.BlockSpec((1, k1), lambda i: (0, 0))]
        x1_args = [h_f, h_b, z_rest, mnorm_w.reshape(1, k1)]
    else:
        assert x1.shape == (t, k1)
        x1_specs, x1_args = [row(k1)], [x1]
    in_specs = x1_specs + [
        row(k2),
        pl.BlockSpec((None, k1, d), lambda i: (layer, 0, 0), pipeline_mode=pl.Buffered(1)),
        pl.BlockSpec((None, k2, d), lambda i: (layer, 1, 0), pipeline_mode=pl.Buffered(1)),
        row(d), modv, pl.BlockSpec((1, d), lambda i: (0, 0)), modv, modv]
    args = x1_args + [x2, w3b, w3b, h, gate, norm_w.reshape(1, d), shift, scale]
    out_specs = [row(d), row(d)]
    out_shape = [jax.ShapeDtypeStruct((t, d), F32), jax.ShapeDtypeStruct((t, d), BF16)]
    if router is not None:
        router_w, router_b = router
        rw = jnp.pad(router_w.astype(F32), ((0, 0), (0, 128 - N_EXPERTS)))
        rw_hi = rw.astype(BF16)
        rw_lo = (rw - rw_hi.astype(F32)).astype(BF16)
        in_specs += [pl.BlockSpec((d, 128), lambda i: (0, 0)), pl.BlockSpec((d, 128), lambda i: (0, 0)),
                     pl.BlockSpec((1, 128), lambda i: (0, 0))]
        args += [rw_hi, rw_lo, jnp.pad(router_b, (0, 128 - N_EXPERTS)).reshape(1, 128)]
        out_specs.append(row(128))
        out_shape.append(jax.ShapeDtypeStruct((t, 128), F32))
    return pl.pallas_call(
        functools.partial(_out_proj_kernel, route=router is not None, head_out=head_out),
        grid=(t // ROW_TILE,),
        in_specs=in_specs,
        out_specs=out_specs,
        out_shape=out_shape,
        compiler_params=_params(1),
        name="out_proj_norm",
    )(*args)


def _ffn_up_kernel(x_ref, wg_ref, wu_ref, o_ref, wgb_ref, wub_ref):
    @pl.when(pl.program_id(1) == 0)
    def _():
        wgb_ref[...] = wg_ref[...].astype(BF16)
        wub_ref[...] = wu_ref[...].astype(BF16)

    x = x_ref[...]
    g = jnp.dot(x, wgb_ref[...], preferred_element_type=F32)
    u = jnp.dot(x, wub_ref[...], preferred_element_type=F32)
    o_ref[...] = (_silu(g) * u).astype(o_ref.dtype)


def _ffn_up(x, w_gate, w_up, layer, tf=512):
    t, k = x.shape
    f = w_gate.shape[2]
    return pl.pallas_call(
        _ffn_up_kernel,
        grid=(f // tf, t // ROW_TILE),
        in_specs=[pl.BlockSpec((ROW_TILE, k), lambda j, i: (i, 0)),
                  pl.BlockSpec((None, k, tf), lambda j, i: (layer, 0, j)),
                  pl.BlockSpec((None, k, tf), lambda j, i: (layer, 0, j))],
        out_specs=pl.BlockSpec((ROW_TILE, tf), lambda j, i: (i, j)),
        out_shape=jax.ShapeDtypeStruct((t, f), BF16),
        scratch_shapes=[pltpu.VMEM((k, tf), BF16), pltpu.VMEM((k, tf), BF16)],
        compiler_params=_params(2),
        name="ffn_up",
    )(x, w_gate, w_up)


def _mm_resid_kernel(x_ref, w_ref, h_ref, g_ref, o_ref, wb_ref):
    @pl.when(pl.program_id(1) == 0)
    def _():
        wb_ref[...] = w_ref[...].astype(BF16)

    y = jnp.dot(x_ref[...], wb_ref[...], preferred_element_type=F32)
    o_ref[...] = h_ref[...] + g_ref[...] * y


def _ffn_down_residual(x, w3, layer, h, gate, tiles_per_seq, tn=512):
    t, k = x.shape
    n = w3.shape[2]
    seg = _seg_index(tiles_per_seq)
    return pl.pallas_call(
        _mm_resid_kernel,
        grid=(n // tn, t // ROW_TILE),
        in_specs=[pl.BlockSpec((ROW_TILE, k), lambda j, i: (i, 0)),
                  pl.BlockSpec((None, k, tn), lambda j, i: (layer, 0, j)),
                  pl.BlockSpec((ROW_TILE, tn), lambda j, i: (i, j)),
                  pl.BlockSpec((None, 1, tn), lambda j, i: (seg(i), 0, j))],
        out_specs=pl.BlockSpec((ROW_TILE, tn), lambda j, i: (i, j)),
        out_shape=jax.ShapeDtypeStruct((t, n), F32),
        scratch_shapes=[pltpu.VMEM((k, tn), BF16)],
        compiler_params=_params(2),
        name="ffn_down_residual",
    )(x, w3, h, gate)


def _rope(x, cs):
    x = x.astype(F32)
    return x * cs[:, :HEAD_DIM] + pltpu.roll(x, HEAD_DIM // 2, axis=1) * cs[:, HEAD_DIM:]


def _attend(q_all, k_parts, v_parts, sink_ref, o_ref, mask_fn):
    group = Q_HEADS // KV_HEADS
    scale = HEAD_DIM ** -0.5
    for g in range(KV_HEADS):
        qg = jnp.concatenate([q_all[:, (g * group + r) * HEAD_DIM:(g * group + r + 1) * HEAD_DIM]
                              for r in range(group)], axis=0).astype(BF16)
        kg = jnp.concatenate([kp[:, g * HEAD_DIM:(g + 1) * HEAD_DIM] for kp in k_parts], axis=0).astype(BF16)
        vg = jnp.concatenate([vp[:, g * HEAD_DIM:(g + 1) * HEAD_DIM] for vp in v_parts], axis=0).astype(BF16)
        s = lax.dot_general(qg, kg, (((1,), (1,)), ((), ())), preferred_element_type=F32) * scale
        if mask_fn is not None:
            s = jnp.where(mask_fn(s.shape), s, NEG)
        sink = sink_ref[g]
        m = jnp.maximum(jnp.max(s, axis=-1, keepdims=True), sink)
        p = jnp.exp(s - m)
        denom = jnp.sum(p, axis=-1, keepdims=True) + jnp.exp(sink - m)
        o = jnp.dot(p.astype(BF16), vg, preferred_element_type=F32) / denom
        for r in range(group):
            hq = g * group + r
            o_ref[:, hq * HEAD_DIM:(hq + 1) * HEAD_DIM] = o[r * ATT_BLOCK:(r + 1) * ATT_BLOCK].astype(o_ref.dtype)


def _attn_kernel(q_ref, kp_ref, kc_ref, kn_ref, kx_ref, vp_ref, vc_ref, vn_ref, vx_ref,
                 csp_ref, csc_ref, csn_ref, sink_ref, o_ref, *, n_blocks, ctx_blocks):
    step = pl.program_id(1)

    @pl.when(step < ctx_blocks)
    def _():
        _attend(q_ref[...], [kx_ref[...]], [vx_ref[...]], sink_ref, o_ref, None)

    @pl.when(step >= ctx_blocks)
    def _():
        _win_attn_body(q_ref, kp_ref, kc_ref, kn_ref, kx_ref, vp_ref, vc_ref, vn_ref, vx_ref,
                       csp_ref, csc_ref, csn_ref, sink_ref, o_ref, step - ctx_blocks, n_blocks)


def _win_attn_body(q_ref, kp_ref, kc_ref, kn_ref, kx_ref, vp_ref, vc_ref, vn_ref, vx_ref,
                   csp_ref, csc_ref, csn_ref, sink_ref, o_ref, n, n_blocks):
    csc = csc_ref[...]
    q_all = jnp.concatenate([_rope(q_ref[:, h * HEAD_DIM:(h + 1) * HEAD_DIM], csc) for h in range(Q_HEADS)], axis=1)

    def rope_kv(k_ref, cs):
        return jnp.concatenate([_rope(k_ref[:, g * HEAD_DIM:(g + 1) * HEAD_DIM], cs) for g in range(KV_HEADS)], axis=1)

    k_parts = [rope_kv(kp_ref, csp_ref[...]), rope_kv(kc_ref, csc), rope_kv(kn_ref, csn_ref[...]), kx_ref[...]]
    v_parts = [vp_ref[...], vc_ref[...], vn_ref[...], vx_ref[...]]

    def mask_fn(shape):
        row = lax.broadcasted_iota(jnp.int32, shape, 0) & (ATT_BLOCK - 1)
        col = lax.broadcasted_iota(jnp.int32, shape, 1)
        band = (col >= row) & (col <= row + 2 * ATT_BLOCK)
        ok_prev = (col >= ATT_BLOCK) | (n > 0)
        ok_next = (col < 2 * ATT_BLOCK) | (n < n_blocks - 1)
        return (band & ok_prev & ok_next) | (col >= 3 * ATT_BLOCK)

    _attend(q_all, k_parts, v_parts, sink_ref, o_ref, mask_fn)


def _sink_rows(sink):
    group = Q_HEADS // KV_HEADS
    return jnp.repeat(sink.astype(F32).reshape(KV_HEADS, group), ATT_BLOCK, axis=1).reshape(
        KV_HEADS, group * ATT_BLOCK, 1)


def _window_attention(z, cs, sink, batch, ctx_len, seq):
    t = z.shape[0]
    nb = seq // ATT_BLOCK
    cb = ctx_len // ATT_BLOCK
    base = batch * cb
    qw = Q_HEADS * HEAD_DIM
    kw = KV_HEADS * HEAD_DIM
    kcol = qw // kw
    vcol = kcol + 1
    sink_rows = _sink_rows(sink)

    def lat(s):
        return jnp.clip(s - cb, 0, nb - 1)

    def q_block(b, s):
        return jnp.where(s < cb, b * cb + s, base + b * nb + lat(s))

    def kv(shift, col):
        return pl.BlockSpec((ATT_BLOCK, kw), lambda b, s: (base + b * nb + lat(s + shift), col))

    def rot(shift):
        return pl.BlockSpec((ATT_BLOCK, 2 * HEAD_DIM), lambda b, s: (lat(s + shift), 0))

    return pl.pallas_call(
        functools.partial(_attn_kernel, n_blocks=nb, ctx_blocks=cb),
        grid=(batch, cb + nb),
        in_specs=[pl.BlockSpec((ATT_BLOCK, qw), lambda b, s: (q_block(b, s), 0)),
                  kv(-1, kcol), kv(0, kcol), kv(1, kcol),
                  pl.BlockSpec((ctx_len, kw), lambda b, s: (b, kcol)),
                  kv(-1, vcol), kv(0, vcol), kv(1, vcol),
                  pl.BlockSpec((ctx_len, kw), lambda b, s: (b, vcol)),
                  rot(-1), rot(0), rot(1),
                  pl.BlockSpec(sink_rows.shape, lambda b, s: (0, 0, 0))],
        out_specs=pl.BlockSpec((ATT_BLOCK, qw), lambda b, s: (q_block(b, s), 0)),
        out_shape=jax.ShapeDtypeStruct((t, qw), BF16),
        compiler_params=_params(2),
        name="window_attention",
    )(z, z, z, z, z, z, z, z, z, cs, cs, cs, sink_rows)


def _seq_edges(t, batch, ctx_tiles, lat_tiles):
    n_ctx = batch * ctx_tiles
    u = t - n_ctx
    is_ctx = t < n_ctx
    first = jnp.where(is_ctx, lax.rem(t, ctx_tiles) == 0, lax.rem(u, lat_tiles) == 0)
    last = jnp.where(is_ctx, lax.rem(t, ctx_tiles) == ctx_tiles - 1, lax.rem(u, lat_tiles) == lat_tiles - 1)
    return first, last


def _fill_ext(ext_ref, cur, prev, nxt, first, last):
    ext_ref[HALO:HALO + CONV_TILE, :] = cur
    ext_ref[0:HALO, :] = jnp.where(first, 0.0, prev)
    ext_ref[HALO + CONV_TILE:, :] = jnp.where(last, 0.0, nxt)


def _dwconv_rows(ext_ref, w_ref, taps, r0, n_rows):
    pad = taps // 2
    acc = None
    for k in range(taps):
        start = HALO + r0 + k - pad
        term = w_ref[k:k + 1, :] * ext_ref[start:start + n_rows, :]
        acc = term if acc is None else acc + term
    return acc


CONV_ROWS = 32


def _short_conv_kernel(bg_ref, cg_ref, xb_ref, cgp_ref, xbp_ref, cgn_ref, xbn_ref, w_ref, o_ref, ext_ref,
                       *, batch, ctx_tiles, lat_tiles):
    first, last = _seq_edges(pl.program_id(0), batch, ctx_tiles, lat_tiles)
    def prod(a_ref, b_ref):
        return a_ref[...].astype(F32) * b_ref[...].astype(F32)

    _fill_ext(ext_ref, prod(cg_ref, xb_ref), prod(cgp_ref, xbp_ref), prod(cgn_ref, xbn_ref), first, last)
    for r0 in range(0, CONV_TILE, CONV_ROWS):
        conv = _dwconv_rows(ext_ref, w_ref, CONV_SHORT, r0, CONV_ROWS)
        o_ref[r0:r0 + CONV_ROWS, :] = (bg_ref[r0:r0 + CONV_ROWS, :].astype(F32) * conv).astype(o_ref.dtype)


def _halo_specs(width, col, halo_per_tile, n_halo_blocks):
    prev = pl.BlockSpec((HALO, width), lambda t, c: (jnp.maximum(t * halo_per_tile - 1, 0), col(c)))
    nxt = pl.BlockSpec((HALO, width), lambda t, c: (jnp.minimum((t + 1) * halo_per_tile, n_halo_blocks - 1), col(c)))
    return prev, nxt


def _short_conv(z, conv_w3, layer, batch, ctx_len, seq):
    t = z.shape[0]
    cw = 512
    n_c = 1024 // cw
    bg0, cg0, xb0 = 1536 // cw, 2560 // cw, 3584 // cw
    hpt = CONV_TILE // HALO
    nh = t // HALO
    cgp, cgn = _halo_specs(cw, lambda c: cg0 + c, hpt, nh)
    xbp, xbn = _halo_specs(cw, lambda c: xb0 + c, hpt, nh)
    kern = functools.partial(_short_conv_kernel, batch=batch, ctx_tiles=ctx_len // CONV_TILE,
                             lat_tiles=seq // CONV_TILE)
    return pl.pallas_call(
        kern,
        grid=(t // CONV_TILE, n_c),
        in_specs=[pl.BlockSpec((CONV_TILE, cw), lambda t_, c: (t_, bg0 + c)),
                  pl.BlockSpec((CONV_TILE, cw), lambda t_, c: (t_, cg0 + c)),
                  pl.BlockSpec((CONV_TILE, cw), lambda t_, c: (t_, xb0 + c)),
                  cgp, xbp, cgn, xbn,
                  pl.BlockSpec((None, CONV_SHORT, cw), lambda t_, c: (layer, 0, c))],
        out_specs=pl.BlockSpec((CONV_TILE, cw), lambda t_, c: (t_, c)),
        out_shape=jax.ShapeDtypeStruct((t, 1024), BF16),
        scratch_shapes=[pltpu.VMEM((CONV_TILE + 2 * HALO, cw), F32)],
        compiler_params=_params(2),
        name="short_conv",
    )(z, z, z, z, z, z, z, conv_w3)


SUBLANES = 8
SHIFT_ROWS = CONV_TILE + 2 * HALO - SUBLANES


def _dwconv_rows_aligned(ext_ref, sh_ref, w8_ref, taps, r0, n_rows):
    pad = taps // 2
    width = ext_ref.shape[1]
    acc = None
    for k in range(taps):
        start = HALO + r0 + k - pad
        b = start % SUBLANES
        a8 = start - b
        src = ext_ref[a8:a8 + n_rows, :] if b == 0 else sh_ref[b - 1, a8:a8 + n_rows, :]
        term = (src.reshape(n_rows // SUBLANES, SUBLANES, width) * w8_ref[k][None]).reshape(n_rows, width)
        acc = term if acc is None else acc + term
    return acc


def _conformer_kernel(a_ref, g_ref, ap_ref, gp_ref, an_ref, gn_ref, w8_ref, b_ref, lw_ref, lb_ref, o_ref,
                      ext_ref, sh_ref, *, batch, ctx_tiles, lat_tiles):
    first, last = _seq_edges(pl.program_id(0), batch, ctx_tiles, lat_tiles)
    _fill_ext(ext_ref, a_ref[...] * _sigmoid(g_ref[...]), ap_ref[...] * _sigmoid(gp_ref[...]),
              an_ref[...] * _sigmoid(gn_ref[...]), first, last)
    for b in range(1, SUBLANES):
        sh_ref[b - 1] = ext_ref[b:b + SHIFT_ROWS, :]
    for r0 in range(0, CONV_TILE, CONV_ROWS):
        u = _dwconv_rows_aligned(ext_ref, sh_ref, w8_ref, CONV_LONG, r0, CONV_ROWS) + b_ref[...]
        mu = jnp.mean(u, axis=-1, keepdims=True)
        uc = u - mu
        var = jnp.mean(uc * uc, axis=-1, keepdims=True)
        y = uc * lax.rsqrt(var + NORM_EPS) * lw_ref[...] + lb_ref[...]
        o_ref[r0:r0 + CONV_ROWS, :] = _silu(y).astype(o_ref.dtype)


def _conformer_conv(z, a_col, g_col, dw_w3, dw_b, ln_w, ln_b, layer, batch, ctx_len, seq):
    t = z.shape[0]
    cw = 1024
    hpt = CONV_TILE // HALO
    nh = t // HALO
    ap, an = _halo_specs(cw, lambda c: a_col, hpt, nh)
    gp, gn = _halo_specs(cw, lambda c: g_col, hpt, nh)
    kern = functools.partial(_conformer_kernel, batch=batch, ctx_tiles=ctx_len // CONV_TILE,
                             lat_tiles=seq // CONV_TILE)
    vec = pl.BlockSpec((None, 1, cw), lambda t_, c: (layer, 0, 0))
    n_layers = dw_b.shape[0]
    return pl.pallas_call(
        kern,
        grid=(t // CONV_TILE, 1),
        in_specs=[pl.BlockSpec((CONV_TILE, cw), lambda t_, c: (t_, a_col)),
                  pl.BlockSpec((CONV_TILE, cw), lambda t_, c: (t_, g_col)),
                  ap, gp, an, gn,
                  pl.BlockSpec((None, CONV_LONG, SUBLANES, cw), lambda t_, c: (layer, 0, 0, 0)),
                  vec, vec, vec],
        out_specs=pl.BlockSpec((CONV_TILE, cw), lambda t_, c: (t_, 0)),
        out_shape=jax.ShapeDtypeStruct((t, cw), BF16),
        scratch_shapes=[pltpu.VMEM((CONV_TILE + 2 * HALO, cw), F32),
                        pltpu.VMEM((SUBLANES - 1, SHIFT_ROWS, cw), F32)],
        compiler_params=_params(2),
        name="conformer_conv",
    )(z, z, z, z, z, z, jnp.broadcast_to(dw_w3[:, :, None, :], (n_layers, CONV_LONG, SUBLANES, cw)),
      dw_b.reshape(n_layers, 1, cw), ln_w.reshape(n_layers, 1, cw), ln_b.reshape(n_layers, 1, cw))


def _log_sigmoid(x):
    return jnp.minimum(x, 0.0) - jnp.log(1.0 + jnp.exp(-jnp.abs(x)))


def _mlstm_kernel(qf_ref, kf_ref, vf_ref, gcf_ref, grf_ref, qb_ref, kb_ref, vb_ref, gcb_ref, grb_ref,
                  bc_ref, br_ref, of_ref, ob_ref, *state_refs):
    n_chain = 2 * M_HEADS
    cta_ref, m_ref = state_refs[:n_chain], state_refs[n_chain:]

    @pl.when(pl.program_id(1) == 0)
    def _():
        for ref in state_refs:
            ref[...] = jnp.zeros_like(ref)

    ln = M_CHUNK
    row = lax.broadcasted_iota(jnp.int32, (ln, ln), 0)
    col = lax.broadcasted_iota(jnp.int32, (ln, ln), 1)
    lower = col <= row
    upper = col >= row
    _mlstm_direction(qf_ref, kf_ref, vf_ref, gcf_ref[...] + bc_ref[0], grf_ref[...] + br_ref[0], lower, upper,
                     of_ref, cta_ref, m_ref, 0)
    _mlstm_direction(qb_ref, kb_ref, vb_ref, gcb_ref[...] + bc_ref[1], grb_ref[...] + br_ref[1], upper, lower,
                     ob_ref, cta_ref, m_ref, M_HEADS)


def _mlstm_direction(q_ref, k_ref, v_ref, gc, gr, mask, mask_t, o_ref, cta_ref, m_ref, state0):
    ln = M_CHUNK
    one_col = lax.broadcasted_iota(jnp.int32, (ln, 128), 1) == 0
    ones_blk = jnp.where(one_col, 1.0, 0.0).astype(BF16)
    for h in range(M_HEADS):
        st = state0 + h
        ig_c = gc[:, h:h + 1]
        lf_c = _log_sigmoid(gc[:, M_HEADS + h:M_HEADS + h + 1])
        ig_r = gr[h:h + 1, :]
        lf_r = _log_sigmoid(gr[M_HEADS + h:M_HEADS + h + 1, :])
        qb = (q_ref[:, h * M_QK:(h + 1) * M_QK] * (M_QK ** -0.5)).astype(BF16)
        k = k_ref[:, h * M_QK:(h + 1) * M_QK]
        v = v_ref[:, h * M_V:(h + 1) * M_V]
        cum_c = jnp.sum(jnp.where(mask, lf_r, 0.0), axis=1, keepdims=True)
        cum_r = jnp.sum(jnp.where(mask_t, lf_c, 0.0), axis=0, keepdims=True)
        total = jnp.sum(lf_r, axis=1, keepdims=True)
        m_s = m_ref[st][...]
        g = jnp.where(mask, ig_r - cum_r, NEG)
        mm = jnp.maximum(m_s, jnp.max(g, axis=1, keepdims=True))
        w_inter = jnp.exp(m_s - mm)
        s = lax.dot_general(qb, k.astype(BF16), (((1,), (1,)), ((), ())), preferred_element_type=F32) * jnp.exp(g - mm)
        cta = cta_ref[st][...]
        va = jnp.concatenate([v.astype(BF16), ones_blk], axis=1)
        inter = jnp.dot(qb, cta.astype(BF16), preferred_element_type=F32)
        intra = jnp.dot(s.astype(BF16), va, preferred_element_type=F32)
        num = w_inter * inter[:, :M_V] + intra[:, :M_V]
        den = w_inter * inter[:, M_V:M_V + 1] + intra[:, M_V:M_V + 1]
        o_ref[:, h * M_V:(h + 1) * M_V] = num / jnp.maximum(jnp.abs(den), jnp.exp(-(cum_c + mm)))
        gl = total - cum_c + ig_c
        m_new = jnp.maximum(total + m_s, jnp.max(gl, axis=0, keepdims=True))
        wg = jnp.exp(gl - m_new)
        decay = jnp.exp(total + m_s - m_new)
        wva = jnp.concatenate([(wg * v).astype(BF16), jnp.where(one_col, wg, 0.0).astype(BF16)], axis=1)
        cta_ref[st][...] = decay * cta + jnp.dot(k.astype(BF16).T, wva, preferred_element_type=F32)
        m_ref[st][...] = m_new


def _mlstm(zq, gates, gate_b, batch, ctx_len, seq):
    t = zq.shape[0]
    cb = ctx_len // M_CHUNK
    nc = seq // M_CHUNK
    base = batch * cb
    steps = cb + nc
    half = 2 * M_HEADS
    g_dir = gates.reshape(t, 2, half).transpose(1, 0, 2)
    g_dir_t = g_dir.transpose(0, 2, 1)
    b_dir = gate_b.astype(F32).reshape(2, 1, half)
    b_dir_t = b_dir.transpose(0, 2, 1)

    def rb_f(b, i):
        return jnp.where(i < cb, b * cb + i, base + b * nc + (i - cb))

    def rb_b(b, i):
        return jnp.where(i < cb, b * cb + (cb - 1 - i), base + b * nc + (nc - 1 - (i - cb)))

    qk_w = M_HEADS * M_QK
    v_w = M_HEADS * M_V

    def chunk_specs(d, rb):
        return [pl.BlockSpec((M_CHUNK, qk_w), lambda b, i: (rb(b, i), 0)),
                pl.BlockSpec((M_CHUNK, qk_w), lambda b, i: (rb(b, i), 1)),
                pl.BlockSpec((M_CHUNK, v_w), lambda b, i: (rb(b, i), 1)),
                pl.BlockSpec((None, M_CHUNK, half), lambda b, i: (d, rb(b, i), 0)),
                pl.BlockSpec((None, half, M_CHUNK), lambda b, i: (d, 0, rb(b, i)))]

    return pl.pallas_call(
        _mlstm_kernel,
        grid=(batch, steps),
        in_specs=chunk_specs(0, rb_f) + chunk_specs(1, rb_b) + [
            pl.BlockSpec((2, 1, half), lambda b, i: (0, 0, 0)),
            pl.BlockSpec((2, half, 1), lambda b, i: (0, 0, 0))],
        out_specs=[pl.BlockSpec((M_CHUNK, v_w), lambda b, i: (rb_f(b, i), 0)),
                   pl.BlockSpec((M_CHUNK, v_w), lambda b, i: (rb_b(b, i), 0))],
        out_shape=[jax.ShapeDtypeStruct((t, v_w), F32), jax.ShapeDtypeStruct((t, v_w), F32)],
        scratch_shapes=([pltpu.VMEM((M_QK, M_V + 128), F32)] * (2 * M_HEADS)
                        + [pltpu.VMEM((1, 1), F32)] * (2 * M_HEADS)),
        compiler_params=_params(2),
        name="mlstm",
    )(zq, zq, zq, g_dir, g_dir_t, zq, zq, zq, g_dir, g_dir_t, b_dir, b_dir_t)


def _fresh_weights(be_ref, rb):
    return (rb == 0) | (be_ref[rb] != be_ref[jnp.maximum(rb - 1, 0)])


def _moe_up_kernel(be_ref, nv_ref, x_ref, wg_ref, wu_ref, o_ref, wgb_ref, wub_ref):
    rb = pl.program_id(1)
    valid = rb < nv_ref[0]

    @pl.when(valid & _fresh_weights(be_ref, rb))
    def _():
        wgb_ref[...] = wg_ref[...].astype(BF16)
        wub_ref[...] = wu_ref[...].astype(BF16)

    @pl.when(valid)
    def _():
        x = x_ref[...]
        g = jnp.dot(x, wgb_ref[...], preferred_element_type=F32)
        u = jnp.dot(x, wub_ref[...], preferred_element_type=F32)
        o_ref[...] = (_silu(g) * u).astype(o_ref.dtype)

    @pl.when(jnp.logical_not(valid))
    def _():
        o_ref[...] = jnp.zeros_like(o_ref)


def _moe_down_kernel(be_ref, nv_ref, x_ref, w_ref, o_ref, wb_ref):
    rb = pl.program_id(1)
    valid = rb < nv_ref[0]

    @pl.when(valid & _fresh_weights(be_ref, rb))
    def _():
        wb_ref[...] = w_ref[...].astype(BF16)

    @pl.when(valid)
    def _():
        o_ref[...] = jnp.dot(x_ref[...], wb_ref[...], preferred_element_type=F32).astype(o_ref.dtype)

    @pl.when(jnp.logical_not(valid))
    def _():
        o_ref[...] = jnp.zeros_like(o_ref)


def _moe_experts(x_sorted, block_expert, n_valid, w_gate, w_up, w_down, layer, tf=512, tn=512):
    p, d = x_sorted.shape
    f = w_gate.shape[3]
    nblk = p // ROW_TILE

    def row(rb, nv):
        return jnp.minimum(rb, nv[0] - 1)

    g = pl.pallas_call(
        _moe_up_kernel,
        grid_spec=pltpu.PrefetchScalarGridSpec(
            num_scalar_prefetch=2,
            grid=(f // tf, nblk),
            in_specs=[pl.BlockSpec((ROW_TILE, d), lambda j, rb, be, nv: (row(rb, nv), 0)),
                      pl.BlockSpec((None, None, d, tf), lambda j, rb, be, nv: (layer, be[rb], 0, j)),
                      pl.BlockSpec((None, None, d, tf), lambda j, rb, be, nv: (layer, be[rb], 0, j))],
            out_specs=pl.BlockSpec((ROW_TILE, tf), lambda j, rb, be, nv: (rb, j)),
            scratch_shapes=[pltpu.VMEM((d, tf), BF16), pltpu.VMEM((d, tf), BF16)]),
        out_shape=jax.ShapeDtypeStruct((p, f), BF16),
        compiler_params=_params(2),
        name="moe_up",
    )(block_expert, n_valid, x_sorted, w_gate, w_up)
    return pl.pallas_call(
        _moe_down_kernel,
        grid_spec=pltpu.PrefetchScalarGridSpec(
            num_scalar_prefetch=2,
            grid=(d // tn, nblk),
            in_specs=[pl.BlockSpec((ROW_TILE, f), lambda j, rb, be, nv: (row(rb, nv), 0)),
                      pl.BlockSpec((None, None, f, tn), lambda j, rb, be, nv: (layer, be[rb], 0, j))],
            out_specs=pl.BlockSpec((ROW_TILE, tn), lambda j, rb, be, nv: (rb, j)),
            scratch_shapes=[pltpu.VMEM((f, tn), BF16)]),
        out_shape=jax.ShapeDtypeStruct((p, d), BF16),
        compiler_params=_params(2),
        name="moe_down",
    )(block_expert, n_valid, g, w_down)


def _routing_plan(route):
    t = route.shape[0]
    tm = ROW_TILE
    n_assign = TOP_K * t
    nblk = (n_assign + N_EXPERTS * (tm - 1)) // tm
    e = route[:, :TOP_K].astype(jnp.int32).reshape(-1)
    onehot = (e[:, None] == jnp.arange(N_EXPERTS, dtype=jnp.int32)[None, :]).astype(jnp.int32)
    csum = jnp.cumsum(onehot, axis=0)
    counts = csum[-1]
    padded = ((counts + tm - 1) // tm) * tm
    ends = jnp.cumsum(padded)
    starts = ends - padded
    dest = jnp.sum((csum - onehot + starts[None, :]) * onehot, axis=1)
    src_tok = (jnp.arange(nblk * tm, dtype=jnp.int32) % t).at[dest].set(
        jnp.arange(n_assign, dtype=jnp.int32) // TOP_K, unique_indices=True, mode="promise_in_bounds")
    n_valid = (ends[-1] // tm).astype(jnp.int32)
    blk_start = jnp.arange(nblk, dtype=jnp.int32) * tm
    be = jnp.sum((ends[None, :] <= blk_start[:, None]).astype(jnp.int32), axis=1)
    be = jnp.minimum(be, N_EXPERTS - 1).astype(jnp.int32)
    return src_tok, be, n_valid.reshape(1), dest.reshape(t, TOP_K)


def _combine_kernel(h_ref, y0_ref, y1_ref, r_ref, g_ref, nw_ref, sh_ref, sc_ref, *out_refs):
    r = r_ref[...]
    moe = r[:, TOP_K:TOP_K + 1] * y0_ref[...].astype(F32) + r[:, TOP_K + 1:TOP_K + 2] * y1_ref[...].astype(F32)
    hn = h_ref[...] + g_ref[...] * moe
    if len(out_refs) == 2:
        out_refs[0][...] = hn
    out_refs[-1][...] = _normmod(hn, nw_ref[...], sh_ref[...], sc_ref[...]).astype(out_refs[-1].dtype)


def _combine_norm(h, y_pair, route, gate, norm_w, shift, scale, tiles_per_seq, row_offset_tiles, emit_h, a_dtype):
    d = h.shape[1]
    n_rows = y_pair.shape[0]
    off = row_offset_tiles
    seg = _seg_index(tiles_per_seq)
    shifted = lambda w: pl.BlockSpec((ROW_TILE, w), lambda i: (i + off, 0))
    tile = pl.BlockSpec((ROW_TILE, d), lambda i: (i, 0))
    tile1 = pl.BlockSpec((ROW_TILE, d), lambda i: (i, 1))
    modv = pl.BlockSpec((None, 1, d), lambda i: (seg(i + off), 0, 0))
    out_specs = [tile, tile] if emit_h else [tile]
    out_shape = [jax.ShapeDtypeStruct((n_rows, d), a_dtype)]
    if emit_h:
        out_shape.insert(0, jax.ShapeDtypeStruct((n_rows, d), F32))
    return pl.pallas_call(
        _combine_kernel,
        grid=(n_rows // ROW_TILE,),
        in_specs=[shifted(d), tile, tile1, shifted(128), modv, pl.BlockSpec((1, d), lambda i: (0, 0)), modv, modv],
        out_specs=out_specs,
        out_shape=out_shape,
        compiler_params=_params(1),
        name="moe_combine_norm",
    )(h, y_pair, y_pair, route, gate, norm_w.reshape(1, d), shift, scale)


def _rope_table(seq):
    rows = seq // GRID_W
    row = jnp.repeat(jnp.arange(rows, dtype=F32), GRID_W)
    col = jnp.tile(jnp.arange(GRID_W, dtype=F32), rows)
    n_freq = HEAD_DIM // 4
    inv_freq = ROPE_BASE ** (-jnp.arange(n_freq, dtype=F32) / n_freq)
    ang = jnp.concatenate([row[:, None] * inv_freq, col[:, None] * inv_freq], axis=-1)
    cos, sin = jnp.cos(ang), jnp.sin(ang)
    return jnp.concatenate([cos, cos, -sin, sin], axis=-1)


def kernel(x, c, ctx, c_ctx, ada_w, ada_b, norm1_w, norm2_w, ev_w_in, ev_sink, ev_conv_w, ev_w_out, ffn_w_gate, ffn_w_up, ffn_w_down, od_w_in, od_gate_b, od_mnorm_w, od_dw_w, od_dw_b, od_ln_w, od_ln_b, od_w_out, moe_router_w, moe_router_b, moe_w_gate, moe_w_up, moe_w_down, final_w):
    batch, seq, d = x.shape
    ctx_len = ctx.shape[1]
    depth = ada_w.shape[0]
    assert batch * ctx_len == ROW_TILE and seq % ROW_TILE == 0 and ctx_len % CONV_TILE == 0
    assert 1 + batch <= 8
    tps = seq // ROW_TILE
    n_ctx_rows = batch * ctx_len

    cs = _rope_table(seq)
    cvec = jnp.zeros((8, d), F32).at[0].set(c_ctx).at[1:1 + batch].set(c)
    mod_table = _ada_table(cvec, ada_w, ada_b)

    def mods_of(layer):
        mods = mod_table[layer, :1 + batch].reshape(1 + batch, 6, 1, d)
        return [mods[:, i] for i in range(6)]

    zero_mod = jnp.zeros((1 + batch, 1, d), F32)
    ctx_tiles = n_ctx_rows // ROW_TILE
    ev_w_out_b = ev_w_out.astype(BF16)
    od_w_out_b = od_w_out.astype(BF16)

    def take(rows, idx):
        return rows.at[idx].get(mode="promise_in_bounds")

    out = None
    mod = mods_of(0)
    h, a1 = _embed_norm(ctx.reshape(n_ctx_rows, d), x.reshape(batch * seq, d), norm1_w[0], mod[0], mod[1], tps)
    for layer in range(depth):
        j = layer // 2
        last = layer == depth - 1
        next_mod = None if last else mods_of(layer + 1)
        if layer % 2 == 0:
            z = _matmul(a1, ev_w_in, j, ev_w_in.shape[2], tn=1536, out_dtype=BF16)
            o_a = _window_attention(z, cs, ev_sink[j], batch, ctx_len, seq)
            o_b = _short_conv(z, ev_conv_w, j, batch, ctx_len, seq)
            h, a2 = _out_proj_norm(o_a, o_b, ev_w_out_b, j, h, mod[2], norm2_w[layer], mod[3], mod[4], tps)
            g = _ffn_up(a2, ffn_w_gate, ffn_w_up, j)
            h = _ffn_down_residual(g, ffn_w_down, j, h, mod[5], tps)
            if last:
                out = _norm_modulate(h, final_w, zero_mod, zero_mod, tps, out_dtype=F32,
                                     row_offset_tiles=ctx_tiles, n_rows=batch * seq)
            else:
                a1 = _norm_modulate(h, norm1_w[layer + 1], next_mod[0], next_mod[1], tps)
        else:
            qkv_w = M_HEADS * (2 * M_QK + M_V)
            n_gate = 4 * M_HEADS
            zq = _matmul(a1, od_w_in, j, qkv_w, tn=1024)
            w_gate_cols = jnp.pad(od_w_in[j, :, qkv_w:qkv_w + n_gate], ((0, 0), (0, 128 - n_gate)))[None]
            gates = _matmul(a1, w_gate_cols, 0, 128, tn=128)[:, :n_gate]
            w_rest = od_w_in[j, :, qkv_w + n_gate:][None]
            z_rest = _matmul(a1, w_rest, 0, w_rest.shape[2], tn=1536)
            h_f, h_b = _mlstm(zq, gates, od_gate_b[j], batch, ctx_len, seq)
            u_out = _conformer_conv(z_rest, 1, 2, od_dw_w, od_dw_b, od_ln_w, od_ln_b, j, batch, ctx_len, seq)
            h, a2, route = _out_proj_norm((h_f, h_b, z_rest, od_mnorm_w[j]), u_out, od_w_out_b, j, h, mod[2],
                                          norm2_w[layer], mod[3], mod[4], tps,
                                          router=(moe_router_w[j], moe_router_b[j]))
            row0 = n_ctx_rows if last else 0
            src_tok, block_expert, n_valid, pos = _routing_plan(route[row0:])
            x_sorted = jnp.take(a2, src_tok + row0, axis=0, mode="clip")
            y = _moe_experts(x_sorted, block_expert, n_valid, moe_w_gate, moe_w_up, moe_w_down, j)
            y_pair = take(y, pos.reshape(-1)).reshape(pos.shape[0], TOP_K * d)
            if last:
                out = _combine_norm(h, y_pair, route, mod[5], final_w, zero_mod, zero_mod, tps, ctx_tiles,
                                    emit_h=False, a_dtype=F32)[0]
            else:
                h, a1 = _combine_norm(h, y_pair, route, mod[5], norm1_w[layer + 1], next_mod[0], next_mod[1], tps,
                                      0, emit_h=True, a_dtype=BF16)
        mod = next_mod
    return out.reshape(batch, seq, d)
```

```python
import functools

import jax
import jax.numpy as jnp
from jax import lax
from jax.experimental import pallas as pl
from jax.experimental.pallas import tpu as pltpu

F32 = jnp.float32
BF16 = jnp.bfloat16

NORM_EPS = 1e-6
ROPE_BASE = 10000.0
GRID_W = 64
HEAD_DIM = 128
Q_HEADS = 8
KV_HEADS = 2
ATT_BLOCK = 128
M_HEADS = 4
M_QK = 128
M_V = 256
M_CHUNK = 128
CONV_SHORT = 3
CONV_LONG = 31
N_EXPERTS = 8
TOP_K = 2

ROW_TILE = 512
CONV_TILE = 256
HALO = 16
NEG = -1e30
V7X_VMEM_LIMIT = 56 * 1024 * 1024


def _params(n_axes):
    return pltpu.CompilerParams(dimension_semantics=("arbitrary",) * n_axes, vmem_limit_bytes=V7X_VMEM_LIMIT)


def _sigmoid(x):
    return 1.0 / (1.0 + jnp.exp(-x))


def _silu(x):
    return x * _sigmoid(x)


def _ada_kernel(c_ref, w_ref, b_ref, o_ref):
    s = _silu(c_ref[...]).astype(BF16)
    o_ref[...] = jnp.dot(s, w_ref[...].astype(BF16), preferred_element_type=F32) + b_ref[...]


def _ada_table(cvec, ada_w, ada_b, tn=1024):
    depth, d, n = ada_w.shape
    rows = cvec.shape[0]
    return pl.pallas_call(
        _ada_kernel,
        grid=(depth, n // tn),
        in_specs=[pl.BlockSpec((rows, d), lambda l, j: (0, 0)),
                  pl.BlockSpec((None, d, tn), lambda l, j: (l, 0, j)),
                  pl.BlockSpec((None, 1, tn), lambda l, j: (l, 0, j))],
        out_specs=pl.BlockSpec((None, rows, tn), lambda l, j: (l, 0, j)),
        out_shape=jax.ShapeDtypeStruct((depth, rows, n), F32),
        compiler_params=_params(2),
        name="ada_table",
    )(cvec, ada_w, ada_b.reshape(depth, 1, n))


def _normmod(x, w, shift, scale):
    ms = jnp.mean(x * x, axis=-1, keepdims=True)
    y = x * lax.rsqrt(ms + NORM_EPS) * w
    return y * (1.0 + scale) + shift


def _normmod_kernel(h_ref, w_ref, sh_ref, sc_ref, o_ref):
    o_ref[...] = _normmod(h_ref[...], w_ref[...], sh_ref[...], sc_ref[...]).astype(o_ref.dtype)


def _embed_kernel(ctx_ref, x_ref, w_ref, sh_ref, sc_ref, h_ref, a_ref):
    rows = jnp.where(pl.program_id(0) == 0, ctx_ref[...], x_ref[...])
    h_ref[...] = rows
    a_ref[...] = _normmod(rows, w_ref[...], sh_ref[...], sc_ref[...]).astype(a_ref.dtype)


def _embed_norm(ctx_rows, x_rows, w, shift, scale, tiles_per_seq):
    n_ctx, d = ctx_rows.shape
    assert n_ctx == ROW_TILE
    t = n_ctx + x_rows.shape[0]
    seg = _seg_index(tiles_per_seq)
    tile = pl.BlockSpec((ROW_TILE, d), lambda i: (i, 0))
    modv = pl.BlockSpec((None, 1, d), lambda i: (seg(i), 0, 0))
    return pl.pallas_call(
        _embed_kernel,
        grid=(t // ROW_TILE,),
        in_specs=[pl.BlockSpec((ROW_TILE, d), lambda i: (0, 0)),
                  pl.BlockSpec((ROW_TILE, d), lambda i: (jnp.maximum(i - 1, 0), 0)),
                  pl.BlockSpec((1, d), lambda i: (0, 0)), modv, modv],
        out_specs=[tile, tile],
        out_shape=[jax.ShapeDtypeStruct((t, d), F32), jax.ShapeDtypeStruct((t, d), BF16)],
        compiler_params=_params(1),
        name="embed_norm",
    )(ctx_rows, x_rows, w.reshape(1, d), shift, scale)


def _top2_route(a, rwh_ref, rwl_ref, rb_ref):
    a_hi = a.astype(BF16)
    a_lo = (a - a_hi.astype(F32)).astype(BF16)
    w_hi = rwh_ref[...]
    logits = (jnp.dot(a_hi, w_hi, preferred_element_type=F32)
              + (jnp.dot(a_lo, w_hi, preferred_element_type=F32)
                 + jnp.dot(a_hi, rwl_ref[...], preferred_element_type=F32))) + rb_ref[...]
    lane = lax.broadcasted_iota(jnp.int32, logits.shape, 1)
    logits = jnp.where(lane < N_EXPERTS, logits, NEG)
    big = jnp.int32(1 << 20)
    m1 = jnp.max(logits, axis=-1, keepdims=True)
    i1 = jnp.min(jnp.where(logits == m1, lane, big), axis=-1, keepdims=True)
    rest = jnp.where(lane == i1, NEG, logits)
    m2 = jnp.max(rest, axis=-1, keepdims=True)
    i2 = jnp.min(jnp.where(rest == m2, lane, big), axis=-1, keepdims=True)
    e2 = jnp.exp(m2 - m1)
    w1 = 1.0 / (1.0 + e2)
    w2 = e2 / (1.0 + e2)
    r = jnp.where(lane == 0, i1.astype(F32), 0.0)
    r = jnp.where(lane == 1, i2.astype(F32), r)
    r = jnp.where(lane == 2, w1, r)
    r = jnp.where(lane == 3, w2, r)
    return r


def _seg_index(tiles_per_seq):
    return lambda i: (i + tiles_per_seq - 1) // tiles_per_seq


def _norm_modulate(h, w, shift, scale, tiles_per_seq, out_dtype=BF16, row_offset_tiles=0, n_rows=None):
    t, d = h.shape
    n_rows = t if n_rows is None else n_rows
    seg = _seg_index(tiles_per_seq)
    off = row_offset_tiles
    return pl.pallas_call(
        _normmod_kernel,
        grid=(n_rows // ROW_TILE,),
        in_specs=[pl.BlockSpec((ROW_TILE, d), lambda i: (i + off, 0)),
                  pl.BlockSpec((1, d), lambda i: (0, 0)),
                  pl.BlockSpec((None, 1, d), lambda i: (seg(i + off), 0, 0)),
                  pl.BlockSpec((None, 1, d), lambda i: (seg(i + off), 0, 0))],
        out_specs=pl.BlockSpec((ROW_TILE, d), lambda i: (i, 0)),
        out_shape=jax.ShapeDtypeStruct((n_rows, d), out_dtype),
        compiler_params=_params(1),
        name="norm_modulate",
    )(h, w.reshape(1, d), shift, scale)


def _mm_kernel(x_ref, w_ref, o_ref, wb_ref):
    @pl.when(pl.program_id(1) == 0)
    def _():
        wb_ref[...] = w_ref[...].astype(BF16)

    o_ref[...] = jnp.dot(x_ref[...], wb_ref[...], preferred_element_type=F32).astype(o_ref.dtype)


def _matmul(x, w3, layer, n_cols, tn=512, out_dtype=F32):
    t, k = x.shape
    return pl.pallas_call(
        _mm_kernel,
        grid=(n_cols // tn, t // ROW_TILE),
        in_specs=[pl.BlockSpec((ROW_TILE, k), lambda j, i: (i, 0)),
                  pl.BlockSpec((None, k, tn), lambda j, i: (layer, 0, j))],
        out_specs=pl.BlockSpec((ROW_TILE, tn), lambda j, i: (i, j)),
        out_shape=jax.ShapeDtypeStruct((t, n_cols), out_dtype),
        scratch_shapes=[pltpu.VMEM((k, tn), BF16)],
        compiler_params=_params(2),
        name="matmul",
    )(x, w3)


def _mlstm_head_out_rows(hf_ref, hb_ref, op_ref, mw_ref):
    parts = []
    for hd in range(M_HEADS):
        sl = slice(hd * M_V, (hd + 1) * M_V)
        x = hf_ref[:, sl] + hb_ref[:, sl]
        mu = jnp.mean(x, axis=-1, keepdims=True)
        xc = x - mu
        var = jnp.mean(xc * xc, axis=-1, keepdims=True)
        y = xc * lax.rsqrt(var + NORM_EPS) * mw_ref[:, sl]
        parts.append((y * _sigmoid(op_ref[:, sl])).astype(BF16))
    return jnp.concatenate(parts, axis=1)


def _out_proj_kernel(*refs, route, head_out):
    refs = list(refs)
    if head_out:
        x1 = _mlstm_head_out_rows(*refs[:4])
        refs = refs[4:]
    else:
        x1 = refs.pop(0)[...]
    if route:
        (x2_ref, w1_ref, w2_ref, h_ref, g_ref, nw_ref, sh_ref, sc_ref, rwh_ref, rwl_ref, rb_ref,
         ho_ref, a_ref, r_ref) = refs
    else:
        x2_ref, w1_ref, w2_ref, h_ref, g_ref, nw_ref, sh_ref, sc_ref, ho_ref, a_ref = refs
    y = jnp.dot(x1, w1_ref[...], preferred_element_type=F32)
    y = y + jnp.dot(x2_ref[...], w2_ref[...], preferred_element_type=F32)
    hn = h_ref[...] + g_ref[...] * y
    ho_ref[...] = hn
    a = _normmod(hn, nw_ref[...], sh_ref[...], sc_ref[...])
    a_ref[...] = a.astype(a_ref.dtype)
    if route:
        r_ref[...] = _top2_route(a, rwh_ref, rwl_ref, rb_ref)


def _out_proj_norm(x1, x2, w3b, layer, h, gate, norm_w, shift, scale, tiles_per_seq, router=None):
    t, k2 = x2.shape
    k1 = k2
    d = w3b.shape[2]
    seg = _seg_index(tiles_per_seq)
    row = lambda w: pl.BlockSpec((ROW_TILE, w), lambda i: (i, 0))
    modv = pl.BlockSpec((None, 1, d), lambda i: (seg(i), 0, 0))
    head_out = isinstance(x1, tuple)
    if head_out:
        h_f, h_b, z_rest, mnorm_w = x1
        assert h_f.shape == (t, k1)
        x1_specs = [row(k1), row(k1), row(k1), pl.BlockSpec((1, k1), lambda i: (0, 0))]
        x1_args = [h_f, h_b, z_rest, mnorm_w.reshape(1, k1)]
    else:
        assert x1.shape == (t, k1)
        x1_specs, x1_args = [row(k1)], [x1]
    in_specs = x1_specs + [
        row(k2),
        pl.BlockSpec((None, k1, d), lambda i: (layer, 0, 0), pipeline_mode=pl.Buffered(1)),
        pl.BlockSpec((None, k2, d), lambda i: (layer, 1, 0), pipeline_mode=pl.Buffered(1)),
        row(d), modv, pl.BlockSpec((1, d), lambda i: (0, 0)), modv, modv]
    args = x1_args + [x2, w3b, w3b, h, gate, norm_w.reshape(1, d), shift, scale]
    out_specs = [row(d), row(d)]
    out_shape = [jax.ShapeDtypeStruct((t, d), F32), jax.ShapeDtypeStruct((t, d), BF16)]
    if router is not None:
        router_w, router_b = router
        rw = jnp.pad(router_w.astype(F32), ((0, 0), (0, 128 - N_EXPERTS)))
        rw_hi = rw.astype(BF16)
        rw_lo = (rw - rw_hi.astype(F32)).astype(BF16)
        in_specs += [pl.BlockSpec((d, 128), lambda i: (0, 0)), pl.BlockSpec((d, 128), lambda i: (0, 0)),
                     pl.BlockSpec((1, 128), lambda i: (0, 0))]
        args += [rw_hi, rw_lo, jnp.pad(router_b, (0, 128 - N_EXPERTS)).reshape(1, 128)]
        out_specs.append(row(128))
        out_shape.append(jax.ShapeDtypeStruct((t, 128), F32))
    return pl.pallas_call(
        functools.partial(_out_proj_kernel, route=router is not None, head_out=head_out),
        grid=(t // ROW_TILE,),
        in_specs=in_specs,
        out_specs=out_specs,
        out_shape=out_shape,
        compiler_params=_params(1),
        name="out_proj_norm",
    )(*args)


def _ffn_up_kernel(x_ref, wg_ref, wu_ref, o_ref, wgb_ref, wub_ref):
    @pl.when(pl.program_id(1) == 0)
    def _():
        wgb_ref[...] = wg_ref[...].astype(BF16)
        wub_ref[...] = wu_ref[...].astype(BF16)

    x = x_ref[...]
    g = jnp.dot(x, wgb_ref[...], preferred_element_type=F32)
    u = jnp.dot(x, wub_ref[...], preferred_element_type=F32)
    o_ref[...] = (_silu(g) * u).astype(o_ref.dtype)


def _ffn_up(x, w_gate, w_up, layer, tf=512):
    t, k = x.shape
    f = w_gate.shape[2]
    return pl.pallas_call(
        _ffn_up_kernel,
        grid=(f // tf, t // ROW_TILE),
        in_specs=[pl.BlockSpec((ROW_TILE, k), lambda j, i: (i, 0)),
                  pl.BlockSpec((None, k, tf), lambda j, i: (layer, 0, j)),
                  pl.BlockSpec((None, k, tf), lambda j, i: (layer, 0, j))],
        out_specs=pl.BlockSpec((ROW_TILE, tf), lambda j, i: (i, j)),
        out_shape=jax.ShapeDtypeStruct((t, f), BF16),
        scratch_shapes=[pltpu.VMEM((k, tf), BF16), pltpu.VMEM((k, tf), BF16)],
        compiler_params=_params(2),
        name="ffn_up",
    )(x, w_gate, w_up)


def _mm_resid_kernel(x_ref, w_ref, h_ref, g_ref, o_ref, wb_ref):
    @pl.when(pl.program_id(1) == 0)
    def _():
        wb_ref[...] = w_ref[...].astype(BF16)

    y = jnp.dot(x_ref[...], wb_ref[...], preferred_element_type=F32)
    o_ref[...] = h_ref[...] + g_ref[...] * y


def _ffn_down_residual(x, w3, layer, h, gate, tiles_per_seq, tn=512):
    t, k = x.shape
    n = w3.shape[2]
    seg = _seg_index(tiles_per_seq)
    return pl.pallas_call(
        _mm_resid_kernel,
        grid=(n // tn, t // ROW_TILE),
        in_specs=[pl.BlockSpec((ROW_TILE, k), lambda j, i: (i, 0)),
                  pl.BlockSpec((None, k, tn), lambda j, i: (layer, 0, j)),
                  pl.BlockSpec((ROW_TILE, tn), lambda j, i: (i, j)),
                  pl.BlockSpec((None, 1, tn), lambda j, i: (seg(i), 0, j))],
        out_specs=pl.BlockSpec((ROW_TILE, tn), lambda j, i: (i, j)),
        out_shape=jax.ShapeDtypeStruct((t, n), F32),
        scratch_shapes=[pltpu.VMEM((k, tn), BF16)],
        compiler_params=_params(2),
        name="ffn_down_residual",
    )(x, w3, h, gate)


def _rope(x, cs):
    x = x.astype(F32)
    return x * cs[:, :HEAD_DIM] + pltpu.roll(x, HEAD_DIM // 2, axis=1) * cs[:, HEAD_DIM:]


def _attend(q_all, k_parts, v_parts, sink_ref, o_ref, mask_fn):
    group = Q_HEADS // KV_HEADS
    scale = HEAD_DIM ** -0.5
    for g in range(KV_HEADS):
        qg = jnp.concatenate([q_all[:, (g * group + r) * HEAD_DIM:(g * group + r + 1) * HEAD_DIM]
                              for r in range(group)], axis=0).astype(BF16)
        kg = jnp.concatenate([kp[:, g * HEAD_DIM:(g + 1) * HEAD_DIM] for kp in k_parts], axis=0).astype(BF16)
        vg = jnp.concatenate([vp[:, g * HEAD_DIM:(g + 1) * HEAD_DIM] for vp in v_parts], axis=0).astype(BF16)
        s = lax.dot_general(qg, kg, (((1,), (1,)), ((), ())), preferred_element_type=F32) * scale
        if mask_fn is not None:
            s = jnp.where(mask_fn(s.shape), s, NEG)
        sink = sink_ref[g]
        m = jnp.maximum(jnp.max(s, axis=-1, keepdims=True), sink)
        p = jnp.exp(s - m)
        denom = jnp.sum(p, axis=-1, keepdims=True) + jnp.exp(sink - m)
        o = jnp.dot(p.astype(BF16), vg, preferred_element_type=F32) / denom
        for r in range(group):
            hq = g * group + r
            o_ref[:, hq * HEAD_DIM:(hq + 1) * HEAD_DIM] = o[r * ATT_BLOCK:(r + 1) * ATT_BLOCK].astype(o_ref.dtype)


def _attn_kernel(q_ref, kp_ref, kc_ref, kn_ref, kx_ref, vp_ref, vc_ref, vn_ref, vx_ref,
                 csp_ref, csc_ref, csn_ref, sink_ref, o_ref, *, n_blocks, ctx_blocks):
    step = pl.program_id(1)

    @pl.when(step < ctx_blocks)
    def _():
        _attend(q_ref[...], [kx_ref[...]], [vx_ref[...]], sink_ref, o_ref, None)

    @pl.when(step >= ctx_blocks)
    def _():
        _win_attn_body(q_ref, kp_ref, kc_ref, kn_ref, kx_ref, vp_ref, vc_ref, vn_ref, vx_ref,
                       csp_ref, csc_ref, csn_ref, sink_ref, o_ref, step - ctx_blocks, n_blocks)


def _win_attn_body(q_ref, kp_ref, kc_ref, kn_ref, kx_ref, vp_ref, vc_ref, vn_ref, vx_ref,
                   csp_ref, csc_ref, csn_ref, sink_ref, o_ref, n, n_blocks):
    csc = csc_ref[...]
    q_all = jnp.concatenate([_rope(q_ref[:, h * HEAD_DIM:(h + 1) * HEAD_DIM], csc) for h in range(Q_HEADS)], axis=1)

    def rope_kv(k_ref, cs):
        return jnp.concatenate([_rope(k_ref[:, g * HEAD_DIM:(g + 1) * HEAD_DIM], cs) for g in range(KV_HEADS)], axis=1)

    k_parts = [rope_kv(kp_ref, csp_ref[...]), rope_kv(kc_ref, csc), rope_kv(kn_ref, csn_ref[...]), kx_ref[...]]
    v_parts = [vp_ref[...], vc_ref[...], vn_ref[...], vx_ref[...]]

    def mask_fn(shape):
        row = lax.broadcasted_iota(jnp.int32, shape, 0) & (ATT_BLOCK - 1)
        col = lax.broadcasted_iota(jnp.int32, shape, 1)
        band = (col >= row) & (col <= row + 2 * ATT_BLOCK)
        ok_prev = (col >= ATT_BLOCK) | (n > 0)
        ok_next = (col < 2 * ATT_BLOCK) | (n < n_blocks - 1)
        return (band & ok_prev & ok_next) | (col >= 3 * ATT_BLOCK)

    _attend(q_all, k_parts, v_parts, sink_ref, o_ref, mask_fn)


def _sink_rows(sink):
    group = Q_HEADS // KV_HEADS
    return jnp.repeat(sink.astype(F32).reshape(KV_HEADS, group), ATT_BLOCK, axis=1).reshape(
        KV_HEADS, group * ATT_BLOCK, 1)


def _window_attention(z, cs, sink, batch, ctx_len, seq):
    t = z.shape[0]
    nb = seq // ATT_BLOCK
    cb = ctx_len // ATT_BLOCK
    base = batch * cb
    qw = Q_HEADS * HEAD_DIM
    kw = KV_HEADS * HEAD_DIM
    kcol = qw // kw
    vcol = kcol + 1
    sink_rows = _sink_rows(sink)

    def lat(s):
        return jnp.clip(s - cb, 0, nb - 1)

    def q_block(b, s):
        return jnp.where(s < cb, b * cb + s, base + b * nb + lat(s))

    def kv(shift, col):
        return pl.BlockSpec((ATT_BLOCK, kw), lambda b, s: (base + b * nb + lat(s + shift), col))

    def rot(shift):
        return pl.BlockSpec((ATT_BLOCK, 2 * HEAD_DIM), lambda b, s: (lat(s + shift), 0))

    return pl.pallas_call(
        functools.partial(_attn_kernel, n_blocks=nb, ctx_blocks=cb),
        grid=(batch, cb + nb),
        in_specs=[pl.BlockSpec((ATT_BLOCK, qw), lambda b, s: (q_block(b, s), 0)),
                  kv(-1, kcol), kv(0, kcol), kv(1, kcol),
                  pl.BlockSpec((ctx_len, kw), lambda b, s: (b, kcol)),
                  kv(-1, vcol), kv(0, vcol), kv(1, vcol),
                  pl.BlockSpec((ctx_len, kw), lambda b, s: (b, vcol)),
                  rot(-1), rot(0), rot(1),
                  pl.BlockSpec(sink_rows.shape, lambda b, s: (0, 0, 0))],
        out_specs=pl.BlockSpec((ATT_BLOCK, qw), lambda b, s: (q_block(b, s), 0)),
        out_shape=jax.ShapeDtypeStruct((t, qw), BF16),
        compiler_params=_params(2),
        name="window_attention",
    )(z, z, z, z, z, z, z, z, z, cs, cs, cs, sink_rows)


def _seq_edges(t, batch, ctx_tiles, lat_tiles):
    n_ctx = batch * ctx_tiles
    u = t - n_ctx
    is_ctx = t < n_ctx
    first = jnp.where(is_ctx, lax.rem(t, ctx_tiles) == 0, lax.rem(u, lat_tiles) == 0)
    last = jnp.where(is_ctx, lax.rem(t, ctx_tiles) == ctx_tiles - 1, lax.rem(u, lat_tiles) == lat_tiles - 1)
    return first, last


def _fill_ext(ext_ref, cur, prev, nxt, first, last):
    ext_ref[HALO:HALO + CONV_TILE, :] = cur
    ext_ref[0:HALO, :] = jnp.where(first, 0.0, prev)
    ext_ref[HALO + CONV_TILE:, :] = jnp.where(last, 0.0, nxt)


def _dwconv_rows(ext_ref, w_ref, taps, r0, n_rows):
    pad = taps // 2
    acc = None
    for k in range(taps):
        start = HALO + r0 + k - pad
        term = w_ref[k:k + 1, :] * ext_ref[start:start + n_rows, :]
        acc = term if acc is None else acc + term
    return acc


CONV_ROWS = 32


def _short_conv_kernel(bg_ref, cg_ref, xb_ref, cgp_ref, xbp_ref, cgn_ref, xbn_ref, w_ref, o_ref, ext_ref,
                       *, batch, ctx_tiles, lat_tiles):
    first, last = _seq_edges(pl.program_id(0), batch, ctx_tiles, lat_tiles)
    def prod(a_ref, b_ref):
        return a_ref[...].astype(F32) * b_ref[...].astype(F32)

    _fill_ext(ext_ref, prod(cg_ref, xb_ref), prod(cgp_ref, xbp_ref), prod(cgn_ref, xbn_ref), first, last)
    for r0 in range(0, CONV_TILE, CONV_ROWS):
        conv = _dwconv_rows(ext_ref, w_ref, CONV_SHORT, r0, CONV_ROWS)
        o_ref[r0:r0 + CONV_ROWS, :] = (bg_ref[r0:r0 + CONV_ROWS, :].astype(F32) * conv).astype(o_ref.dtype)


def _halo_specs(width, col, halo_per_tile, n_halo_blocks):
    prev = pl.BlockSpec((HALO, width), lambda t, c: (jnp.maximum(t * halo_per_tile - 1, 0), col(c)))
    nxt = pl.BlockSpec((HALO, width), lambda t, c: (jnp.minimum((t + 1) * halo_per_tile, n_halo_blocks - 1), col(c)))
    return prev, nxt


def _short_conv(z, conv_w3, layer, batch, ctx_len, seq):
    t = z.shape[0]
    cw = 512
    n_c = 1024 // cw
    bg0, cg0, xb0 = 1536 // cw, 2560 // cw, 3584 // cw
    hpt = CONV_TILE // HALO
    nh = t // HALO
    cgp, cgn = _halo_specs(cw, lambda c: cg0 + c, hpt, nh)
    xbp, xbn = _halo_specs(cw, lambda c: xb0 + c, hpt, nh)
    kern = functools.partial(_short_conv_kernel, batch=batch, ctx_tiles=ctx_len // CONV_TILE,
                             lat_tiles=seq // CONV_TILE)
    return pl.pallas_call(
        kern,
        grid=(t // CONV_TILE, n_c),
        in_specs=[pl.BlockSpec((CONV_TILE, cw), lambda t_, c: (t_, bg0 + c)),
                  pl.BlockSpec((CONV_TILE, cw), lambda t_, c: (t_, cg0 + c)),
                  pl.BlockSpec((CONV_TILE, cw), lambda t_, c: (t_, xb0 + c)),
                  cgp, xbp, cgn, xbn,
                  pl.BlockSpec((None, CONV_SHORT, cw), lambda t_, c: (layer, 0, c))],
        out_specs=pl.BlockSpec((CONV_TILE, cw), lambda t_, c: (t_, c)),
        out_shape=jax.ShapeDtypeStruct((t, 1024), BF16),
        scratch_shapes=[pltpu.VMEM((CONV_TILE + 2 * HALO, cw), F32)],
        compiler_params=_params(2),
        name="short_conv",
    )(z, z, z, z, z, z, z, conv_w3)


SUBLANES = 8
SHIFT_ROWS = CONV_TILE + 2 * HALO - SUBLANES


def _dwconv_rows_aligned(ext_ref, sh_ref, w8_ref, taps, r0, n_rows):
    pad = taps // 2
    width = ext_ref.shape[1]
    acc = None
    for k in range(taps):
        start = HALO + r0 + k - pad
        b = start % SUBLANES
        a8 = start - b
        src = ext_ref[a8:a8 + n_rows, :] if b == 0 else sh_ref[b - 1, a8:a8 + n_rows, :]
        term = (src.reshape(n_rows // SUBLANES, SUBLANES, width) * w8_ref[k][None]).reshape(n_rows, width)
        acc = term if acc is None else acc + term
    return acc


def _conformer_kernel(a_ref, g_ref, ap_ref, gp_ref, an_ref, gn_ref, w8_ref, b_ref, lw_ref, lb_ref, o_ref,
                      ext_ref, sh_ref, *, batch, ctx_tiles, lat_tiles):
    first, last = _seq_edges(pl.program_id(0), batch, ctx_tiles, lat_tiles)
    _fill_ext(ext_ref, a_ref[...] * _sigmoid(g_ref[...]), ap_ref[...] * _sigmoid(gp_ref[...]),
              an_ref[...] * _sigmoid(gn_ref[...]), first, last)
    for b in range(1, SUBLANES):
        sh_ref[b - 1] = ext_ref[b:b + SHIFT_ROWS, :]
    for r0 in range(0, CONV_TILE, CONV_ROWS):
        u = _dwconv_rows_aligned(ext_ref, sh_ref, w8_ref, CONV_LONG, r0, CONV_ROWS) + b_ref[...]
        mu = jnp.mean(u, axis=-1, keepdims=True)
        uc = u - mu
        var = jnp.mean(uc * uc, axis=-1, keepdims=True)
        y = uc * lax.rsqrt(var + NORM_EPS) * lw_ref[...] + lb_ref[...]
        o_ref[r0:r0 + CONV_ROWS, :] = _silu(y).astype(o_ref.dtype)


def _conformer_conv(z, a_col, g_col, dw_w3, dw_b, ln_w, ln_b, layer, batch, ctx_len, seq):
    t = z.shape[0]
    cw = 1024
    hpt = CONV_TILE // HALO
    nh = t // HALO
    ap, an = _halo_specs(cw, lambda c: a_col, hpt, nh)
    gp, gn = _halo_specs(cw, lambda c: g_col, hpt, nh)
    kern = functools.partial(_conformer_kernel, batch=batch, ctx_tiles=ctx_len // CONV_TILE,
                             lat_tiles=seq // CONV_TILE)
    vec = pl.BlockSpec((None, 1, cw), lambda t_, c: (layer, 0, 0))
    n_layers = dw_b.shape[0]
    return pl.pallas_call(
        kern,
        grid=(t // CONV_TILE, 1),
        in_specs=[pl.BlockSpec((CONV_TILE, cw), lambda t_, c: (t_, a_col)),
                  pl.BlockSpec((CONV_TILE, cw), lambda t_, c: (t_, g_col)),
                  ap, gp, an, gn,
                  pl.BlockSpec((None, CONV_LONG, SUBLANES, cw), lambda t_, c: (layer, 0, 0, 0)),
                  vec, vec, vec],
        out_specs=pl.BlockSpec((CONV_TILE, cw), lambda t_, c: (t_, 0)),
        out_shape=jax.ShapeDtypeStruct((t, cw), BF16),
        scratch_shapes=[pltpu.VMEM((CONV_TILE + 2 * HALO, cw), F32),
                        pltpu.VMEM((SUBLANES - 1, SHIFT_ROWS, cw), F32)],
        compiler_params=_params(2),
        name="conformer_conv",
    )(z, z, z, z, z, z, jnp.broadcast_to(dw_w3[:, :, None, :], (n_layers, CONV_LONG, SUBLANES, cw)),
      dw_b.reshape(n_layers, 1, cw), ln_w.reshape(n_layers, 1, cw), ln_b.reshape(n_layers, 1, cw))


def _log_sigmoid(x):
    return jnp.minimum(x, 0.0) - jnp.log(1.0 + jnp.exp(-jnp.abs(x)))


def _mlstm_kernel(qf_ref, kf_ref, vf_ref, gcf_ref, grf_ref, qb_ref, kb_ref, vb_ref, gcb_ref, grb_ref,
                  bc_ref, br_ref, of_ref, ob_ref, *state_refs):
    n_chain = 2 * M_HEADS
    cta_ref, m_ref = state_refs[:n_chain], state_refs[n_chain:]

    @pl.when(pl.program_id(1) == 0)
    def _():
        for ref in state_refs:
            ref[...] = jnp.zeros_like(ref)

    ln = M_CHUNK
    row = lax.broadcasted_iota(jnp.int32, (ln, ln), 0)
    col = lax.broadcasted_iota(jnp.int32, (ln, ln), 1)
    lower = col <= row
    upper = col >= row
    _mlstm_direction(qf_ref, kf_ref, vf_ref, gcf_ref[...] + bc_ref[0], grf_ref[...] + br_ref[0], lower, upper,
                     of_ref, cta_ref, m_ref, 0)
    _mlstm_direction(qb_ref, kb_ref, vb_ref, gcb_ref[...] + bc_ref[1], grb_ref[...] + br_ref[1], upper, lower,
                     ob_ref, cta_ref, m_ref, M_HEADS)


def _mlstm_direction(q_ref, k_ref, v_ref, gc, gr, mask, mask_t, o_ref, cta_ref, m_ref, state0):
    ln = M_CHUNK
    one_col = lax.broadcasted_iota(jnp.int32, (ln, 128), 1) == 0
    ones_blk = jnp.where(one_col, 1.0, 0.0).astype(BF16)
    for h in range(M_HEADS):
        st = state0 + h
        ig_c = gc[:, h:h + 1]
        lf_c = _log_sigmoid(gc[:, M_HEADS + h:M_HEADS + h + 1])
        ig_r = gr[h:h + 1, :]
        lf_r = _log_sigmoid(gr[M_HEADS + h:M_HEADS + h + 1, :])
        qb = (q_ref[:, h * M_QK:(h + 1) * M_QK] * (M_QK ** -0.5)).astype(BF16)
        k = k_ref[:, h * M_QK:(h + 1) * M_QK]
        v = v_ref[:, h * M_V:(h + 1) * M_V]
        cum_c = jnp.sum(jnp.where(mask, lf_r, 0.0), axis=1, keepdims=True)
        cum_r = jnp.sum(jnp.where(mask_t, lf_c, 0.0), axis=0, keepdims=True)
        total = jnp.sum(lf_r, axis=1, keepdims=True)
        m_s = m_ref[st][...]
        g = jnp.where(mask, ig_r - cum_r, NEG)
        mm = jnp.maximum(m_s, jnp.max(g, axis=1, keepdims=True))
        w_inter = jnp.exp(m_s - mm)
        s = lax.dot_general(qb, k.astype(BF16), (((1,), (1,)), ((), ())), preferred_element_type=F32) * jnp.exp(g - mm)
        cta = cta_ref[st][...]
        va = jnp.concatenate([v.astype(BF16), ones_blk], axis=1)
        inter = jnp.dot(qb, cta.astype(BF16), preferred_element_type=F32)
        intra = jnp.dot(s.astype(BF16), va, preferred_element_type=F32)
        num = w_inter * inter[:, :M_V] + intra[:, :M_V]
        den = w_inter * inter[:, M_V:M_V + 1] + intra[:, M_V:M_V + 1]
        o_ref[:, h * M_V:(h + 1) * M_V] = num / jnp.maximum(jnp.abs(den), jnp.exp(-(cum_c + mm)))
        gl = total - cum_c + ig_c
        m_new = jnp.maximum(total + m_s, jnp.max(gl, axis=0, keepdims=True))
        wg = jnp.exp(gl - m_new)
        decay = jnp.exp(total + m_s - m_new)
        wva = jnp.concatenate([(wg * v).astype(BF16), jnp.where(one_col, wg, 0.0).astype(BF16)], axis=1)
        cta_ref[st][...] = decay * cta + jnp.dot(k.astype(BF16).T, wva, preferred_element_type=F32)
        m_ref[st][...] = m_new


def _mlstm(zq, gates, gate_b, batch, ctx_len, seq):
    t = zq.shape[0]
    cb = ctx_len // M_CHUNK
    nc = seq // M_CHUNK
    base = batch * cb
    steps = cb + nc
    half = 2 * M_HEADS
    g_dir = gates.reshape(t, 2, half).transpose(1, 0, 2)
    g_dir_t = g_dir.transpose(0, 2, 1)
    b_dir = gate_b.astype(F32).reshape(2, 1, half)
    b_dir_t = b_dir.transpose(0, 2, 1)

    def rb_f(b, i):
        return jnp.where(i < cb, b * cb + i, base + b * nc + (i - cb))

    def rb_b(b, i):
        return jnp.where(i < cb, b * cb + (cb - 1 - i), base + b * nc + (nc - 1 - (i - cb)))

    qk_w = M_HEADS * M_QK
    v_w = M_HEADS * M_V

    def chunk_specs(d, rb):
        return [pl.BlockSpec((M_CHUNK, qk_w), lambda b, i: (rb(b, i), 0)),
                pl.BlockSpec((M_CHUNK, qk_w), lambda b, i: (rb(b, i), 1)),
                pl.BlockSpec((M_CHUNK, v_w), lambda b, i: (rb(b, i), 1)),
                pl.BlockSpec((None, M_CHUNK, half), lambda b, i: (d, rb(b, i), 0)),
                pl.BlockSpec((None, half, M_CHUNK), lambda b, i: (d, 0, rb(b, i)))]

    return pl.pallas_call(
        _mlstm_kernel,
        grid=(batch, steps),
        in_specs=chunk_specs(0, rb_f) + chunk_specs(1, rb_b) + [
            pl.BlockSpec((2, 1, half), lambda b, i: (0, 0, 0)),
            pl.BlockSpec((2, half, 1), lambda b, i: (0, 0, 0))],
        out_specs=[pl.BlockSpec((M_CHUNK, v_w), lambda b, i: (rb_f(b, i), 0)),
                   pl.BlockSpec((M_CHUNK, v_w), lambda b, i: (rb_b(b, i), 0))],
        out_shape=[jax.ShapeDtypeStruct((t, v_w), F32), jax.ShapeDtypeStruct((t, v_w), F32)],
        scratch_shapes=([pltpu.VMEM((M_QK, M_V + 128), F32)] * (2 * M_HEADS)
                        + [pltpu.VMEM((1, 1), F32)] * (2 * M_HEADS)),
        compiler_params=_params(2),
        name="mlstm",
    )(zq, zq, zq, g_dir, g_dir_t, zq, zq, zq, g_dir, g_dir_t, b_dir, b_dir_t)


def _fresh_weights(be_ref, rb):
    return (rb == 0) | (be_ref[rb] != be_ref[jnp.maximum(rb - 1, 0)])


def _moe_up_kernel(be_ref, nv_ref, x_ref, wg_ref, wu_ref, o_ref, wgb_ref, wub_ref):
    rb = pl.program_id(1)
    valid = rb < nv_ref[0]

    @pl.when(valid & _fresh_weights(be_ref, rb))
    def _():
        wgb_ref[...] = wg_ref[...].astype(BF16)
        wub_ref[...] = wu_ref[...].astype(BF16)

    @pl.when(valid)
    def _():
        x = x_ref[...]
        g = jnp.dot(x, wgb_ref[...], preferred_element_type=F32)
        u = jnp.dot(x, wub_ref[...], preferred_element_type=F32)
        o_ref[...] = (_silu(g) * u).astype(o_ref.dtype)

    @pl.when(jnp.logical_not(valid))
    def _():
        o_ref[...] = jnp.zeros_like(o_ref)


def _moe_down_kernel(be_ref, nv_ref, x_ref, w_ref, o_ref, wb_ref):
    rb = pl.program_id(1)
    valid = rb < nv_ref[0]

    @pl.when(valid & _fresh_weights(be_ref, rb))
    def _():
        wb_ref[...] = w_ref[...].astype(BF16)

    @pl.when(valid)
    def _():
        o_ref[...] = jnp.dot(x_ref[...], wb_ref[...], preferred_element_type=F32).astype(o_ref.dtype)

    @pl.when(jnp.logical_not(valid))
    def _():
        o_ref[...] = jnp.zeros_like(o_ref)


def _moe_experts(x_sorted, block_expert, n_valid, w_gate, w_up, w_down, layer, tf=512, tn=512):
    p, d = x_sorted.shape
    f = w_gate.shape[3]
    nblk = p // ROW_TILE

    def row(rb, nv):
        return jnp.minimum(rb, nv[0] - 1)

    g = pl.pallas_call(
        _moe_up_kernel,
        grid_spec=pltpu.PrefetchScalarGridSpec(
            num_scalar_prefetch=2,
            grid=(f // tf, nblk),
            in_specs=[pl.BlockSpec((ROW_TILE, d), lambda j, rb, be, nv: (row(rb, nv), 0)),
                      pl.BlockSpec((None, None, d, tf), lambda j, rb, be, nv: (layer, be[rb], 0, j)),
                      pl.BlockSpec((None, None, d, tf), lambda j, rb, be, nv: (layer, be[rb], 0, j))],
            out_specs=pl.BlockSpec((ROW_TILE, tf), lambda j, rb, be, nv: (rb, j)),
            scratch_shapes=[pltpu.VMEM((d, tf), BF16), pltpu.VMEM((d, tf), BF16)]),
        out_shape=jax.ShapeDtypeStruct((p, f), BF16),
        compiler_params=_params(2),
        name="moe_up",
    )(block_expert, n_valid, x_sorted, w_gate, w_up)
    return pl.pallas_call(
        _moe_down_kernel,
        grid_spec=pltpu.PrefetchScalarGridSpec(
            num_scalar_prefetch=2,
            grid=(d // tn, nblk),
            in_specs=[pl.BlockSpec((ROW_TILE, f), lambda j, rb, be, nv: (row(rb, nv), 0)),
                      pl.BlockSpec((None, None, f, tn), lambda j, rb, be, nv: (layer, be[rb], 0, j))],
            out_specs=pl.BlockSpec((ROW_TILE, tn), lambda j, rb, be, nv: (rb, j)),
            scratch_shapes=[pltpu.VMEM((f, tn), BF16)]),
        out_shape=jax.ShapeDtypeStruct((p, d), BF16),
        compiler_params=_params(2),
        name="moe_down",
    )(block_expert, n_valid, g, w_down)


def _routing_plan(route):
    t = route.shape[0]
    tm = ROW_TILE
    n_assign = TOP_K * t
    nblk = (n_assign + N_EXPERTS * (tm - 1)) // tm
    e = route[:, :TOP_K].astype(jnp.int32).reshape(-1)
    onehot = (e[:, None] == jnp.arange(N_EXPERTS, dtype=jnp.int32)[None, :]).astype(jnp.int32)
    csum = jnp.cumsum(onehot, axis=0)
    counts = csum[-1]
    padded = ((counts + tm - 1) // tm) * tm
    ends = jnp.cumsum(padded)
    starts = ends - padded
    dest = jnp.sum((csum - onehot + starts[None, :]) * onehot, axis=1)
    src_tok = (jnp.arange(nblk * tm, dtype=jnp.int32) % t).at[dest].set(
        jnp.arange(n_assign, dtype=jnp.int32) // TOP_K, unique_indices=True, mode="promise_in_bounds")
    n_valid = (ends[-1] // tm).astype(jnp.int32)
    blk_start = jnp.arange(nblk, dtype=jnp.int32) * tm
    be = jnp.sum((ends[None, :] <= blk_start[:, None]).astype(jnp.int32), axis=1)
    be = jnp.minimum(be, N_EXPERTS - 1).astype(jnp.int32)
    return src_tok, be, n_valid.reshape(1), dest.reshape(t, TOP_K)


def _combine_kernel(h_ref, y0_ref, y1_ref, r_ref, g_ref, nw_ref, sh_ref, sc_ref, *out_refs):
    r = r_ref[...]
    moe = r[:, TOP_K:TOP_K + 1] * y0_ref[...].astype(F32) + r[:, TOP_K + 1:TOP_K + 2] * y1_ref[...].astype(F32)
    hn = h_ref[...] + g_ref[...] * moe
    if len(out_refs) == 2:
        out_refs[0][...] = hn
    out_refs[-1][...] = _normmod(hn, nw_ref[...], sh_ref[...], sc_ref[...]).astype(out_refs[-1].dtype)


def _combine_norm(h, y_pair, route, gate, norm_w, shift, scale, tiles_per_seq, row_offset_tiles, emit_h, a_dtype):
    d = h.shape[1]
    n_rows = y_pair.shape[0] // TOP_K
    n_tiles = n_rows // ROW_TILE
    off = row_offset_tiles
    seg = _seg_index(tiles_per_seq)
    shifted = lambda w: pl.BlockSpec((ROW_TILE, w), lambda i: (i + off, 0))
    tile = pl.BlockSpec((ROW_TILE, d), lambda i: (i, 0))
    tile1 = pl.BlockSpec((ROW_TILE, d), lambda i: (i + n_tiles, 0))
    modv = pl.BlockSpec((None, 1, d), lambda i: (seg(i + off), 0, 0))
    out_specs = [tile, tile] if emit_h else [tile]
    out_shape = [jax.ShapeDtypeStruct((n_rows, d), a_dtype)]
    if emit_h:
        out_shape.insert(0, jax.ShapeDtypeStruct((n_rows, d), F32))
    return pl.pallas_call(
        _combine_kernel,
        grid=(n_rows // ROW_TILE,),
        in_specs=[shifted(d), tile, tile1, shifted(128), modv, pl.BlockSpec((1, d), lambda i: (0, 0)), modv, modv],
        out_specs=out_specs,
        out_shape=out_shape,
        compiler_params=_params(1),
        name="moe_combine_norm",
    )(h, y_pair, y_pair, route, gate, norm_w.reshape(1, d), shift, scale)


def _rope_table(seq):
    rows = seq // GRID_W
    row = jnp.repeat(jnp.arange(rows, dtype=F32), GRID_W)
    col = jnp.tile(jnp.arange(GRID_W, dtype=F32), rows)
    n_freq = HEAD_DIM // 4
    inv_freq = ROPE_BASE ** (-jnp.arange(n_freq, dtype=F32) / n_freq)
    ang = jnp.concatenate([row[:, None] * inv_freq, col[:, None] * inv_freq], axis=-1)
    cos, sin = jnp.cos(ang), jnp.sin(ang)
    return jnp.concatenate([cos, cos, -sin, sin], axis=-1)


def kernel(x, c, ctx, c_ctx, ada_w, ada_b, norm1_w, norm2_w, ev_w_in, ev_sink, ev_conv_w, ev_w_out, ffn_w_gate, ffn_w_up, ffn_w_down, od_w_in, od_gate_b, od_mnorm_w, od_dw_w, od_dw_b, od_ln_w, od_ln_b, od_w_out, moe_router_w, moe_router_b, moe_w_gate, moe_w_up, moe_w_down, final_w):
    batch, seq, d = x.shape
    ctx_len = ctx.shape[1]
    depth = ada_w.shape[0]
    assert batch * ctx_len == ROW_TILE and seq % ROW_TILE == 0 and ctx_len % CONV_TILE == 0
    assert 1 + batch <= 8
    tps = seq // ROW_TILE
    n_ctx_rows = batch * ctx_len

    cs = _rope_table(seq)
    cvec = jnp.zeros((8, d), F32).at[0].set(c_ctx).at[1:1 + batch].set(c)
    mod_table = _ada_table(cvec, ada_w, ada_b)

    def mods_of(layer):
        mods = mod_table[layer, :1 + batch].reshape(1 + batch, 6, 1, d)
        return [mods[:, i] for i in range(6)]

    zero_mod = jnp.zeros((1 + batch, 1, d), F32)
    ctx_tiles = n_ctx_rows // ROW_TILE
    ev_w_out_b = ev_w_out.astype(BF16)
    od_w_out_b = od_w_out.astype(BF16)

    def take(rows, idx):
        return rows.at[idx].get(mode="promise_in_bounds")

    out = None
    mod = mods_of(0)
    h, a1 = _embed_norm(ctx.reshape(n_ctx_rows, d), x.reshape(batch * seq, d), norm1_w[0], mod[0], mod[1], tps)
    for layer in range(depth):
        j = layer // 2
        last = layer == depth - 1
        next_mod = None if last else mods_of(layer + 1)
        if layer % 2 == 0:
            z = _matmul(a1, ev_w_in, j, ev_w_in.shape[2], tn=1536, out_dtype=BF16)
            o_a = _window_attention(z, cs, ev_sink[j], batch, ctx_len, seq)
            o_b = _short_conv(z, ev_conv_w, j, batch, ctx_len, seq)
            h, a2 = _out_proj_norm(o_a, o_b, ev_w_out_b, j, h, mod[2], norm2_w[layer], mod[3], mod[4], tps)
            g = _ffn_up(a2, ffn_w_gate, ffn_w_up, j)
            h = _ffn_down_residual(g, ffn_w_down, j, h, mod[5], tps)
            if last:
                out = _norm_modulate(h, final_w, zero_mod, zero_mod, tps, out_dtype=F32,
                                     row_offset_tiles=ctx_tiles, n_rows=batch * seq)
            else:
                a1 = _norm_modulate(h, norm1_w[layer + 1], next_mod[0], next_mod[1], tps)
        else:
            qkv_w = M_HEADS * (2 * M_QK + M_V)
            n_gate = 4 * M_HEADS
            zq = _matmul(a1, od_w_in, j, qkv_w, tn=1024)
            w_gate_cols = jnp.pad(od_w_in[j, :, qkv_w:qkv_w + n_gate], ((0, 0), (0, 128 - n_gate)))[None]
            gates = _matmul(a1, w_gate_cols, 0, 128, tn=128)[:, :n_gate]
            w_rest = od_w_in[j, :, qkv_w + n_gate:][None]
            z_rest = _matmul(a1, w_rest, 0, w_rest.shape[2], tn=1536)
            h_f, h_b = _mlstm(zq, gates, od_gate_b[j], batch, ctx_len, seq)
            u_out = _conformer_conv(z_rest, 1, 2, od_dw_w, od_dw_b, od_ln_w, od_ln_b, j, batch, ctx_len, seq)
            h, a2, route = _out_proj_norm((h_f, h_b, z_rest, od_mnorm_w[j]), u_out, od_w_out_b, j, h, mod[2],
                                          norm2_w[layer], mod[3], mod[4], tps,
                                          router=(moe_router_w[j], moe_router_b[j]))
            row0 = n_ctx_rows if last else 0
            src_tok, block_expert, n_valid, pos = _routing_plan(route[row0:])
            x_sorted = jnp.take(a2, src_tok + row0, axis=0, mode="clip")
            y = _moe_experts(x_sorted, block_expert, n_valid, moe_w_gate, moe_w_up, moe_w_down, j)
            y_pair = take(y, pos.T.reshape(-1))
            if last:
                out = _combine_norm(h, y_pair, route, mod[5], final_w, zero_mod, zero_mod, tps, ctx_tiles,
                                    emit_h=False, a_dtype=F32)[0]
            else:
                h, a1 = _combine_norm(h, y_pair, route, mod[5], norm1_w[layer + 1], next_mod[0], next_mod[1], tps,
                                      0, emit_h=True, a_dtype=BF16)
        mod = next_mod
    return out.reshape(batch, seq, d)
```

```python
import functools

import jax
import jax.numpy as jnp
from jax import lax
from jax.experimental import pallas as pl
from jax.experimental.pallas import tpu as pltpu

F32 = jnp.float32
BF16 = jnp.bfloat16

NORM_EPS = 1e-6
ROPE_BASE = 10000.0
GRID_W = 64
HEAD_DIM = 128
Q_HEADS = 8
KV_HEADS = 2
ATT_BLOCK = 128
M_HEADS = 4
M_QK = 128
M_V = 256
M_CHUNK = 128
CONV_SHORT = 3
CONV_LONG = 31
N_EXPERTS = 8
TOP_K = 2

ROW_TILE = 512
CONV_TILE = 256
HALO = 16
NEG = -1e30
V7X_VMEM_LIMIT = 56 * 1024 * 1024


def _params(n_axes):
    return pltpu.CompilerParams(dimension_semantics=("arbitrary",) * n_axes, vmem_limit_bytes=V7X_VMEM_LIMIT)


def _sigmoid(x):
    return 1.0 / (1.0 + jnp.exp(-x))


def _silu(x):
    return x * _sigmoid(x)


def _ada_kernel(c_ref, w_ref, b_ref, o_ref):
    s = _silu(c_ref[...]).astype(BF16)
    o_ref[...] = jnp.dot(s, w_ref[...].astype(BF16), preferred_element_type=F32) + b_ref[...]


def _ada_table(cvec, ada_w, ada_b, tn=1024):
    depth, d, n = ada_w.shape
    rows = cvec.shape[0]
    return pl.pallas_call(
        _ada_kernel,
        grid=(depth, n // tn),
        in_specs=[pl.BlockSpec((rows, d), lambda l, j: (0, 0)),
                  pl.BlockSpec((None, d, tn), lambda l, j: (l, 0, j)),
                  pl.BlockSpec((None, 1, tn), lambda l, j: (l, 0, j))],
        out_specs=pl.BlockSpec((None, rows, tn), lambda l, j: (l, 0, j)),
        out_shape=jax.ShapeDtypeStruct((depth, rows, n), F32),
        compiler_params=_params(2),
        name="ada_table",
    )(cvec, ada_w, ada_b.reshape(depth, 1, n))


def _normmod(x, w, shift, scale):
    ms = jnp.mean(x * x, axis=-1, keepdims=True)
    y = x * lax.rsqrt(ms + NORM_EPS) * w
    return y * (1.0 + scale) + shift


def _normmod_kernel(h_ref, w_ref, sh_ref, sc_ref, o_ref):
    o_ref[...] = _normmod(h_ref[...], w_ref[...], sh_ref[...], sc_ref[...]).astype(o_ref.dtype)


def _embed_kernel(ctx_ref, x_ref, w_ref, sh_ref, sc_ref, h_ref, a_ref):
    rows = jnp.where(pl.program_id(0) == 0, ctx_ref[...], x_ref[...])
    h_ref[...] = rows
    a_ref[...] = _normmod(rows, w_ref[...], sh_ref[...], sc_ref[...]).astype(a_ref.dtype)


def _embed_norm(ctx_rows, x_rows, w, shift, scale, tiles_per_seq):
    n_ctx, d = ctx_rows.shape
    assert n_ctx == ROW_TILE
    t = n_ctx + x_rows.shape[0]
    seg = _seg_index(tiles_per_seq)
    tile = pl.BlockSpec((ROW_TILE, d), lambda i: (i, 0))
    modv = pl.BlockSpec((None, 1, d), lambda i: (seg(i), 0, 0))
    return pl.pallas_call(
        _embed_kernel,
        grid=(t // ROW_TILE,),
        in_specs=[pl.BlockSpec((ROW_TILE, d), lambda i: (0, 0)),
                  pl.BlockSpec((ROW_TILE, d), lambda i: (jnp.maximum(i - 1, 0), 0)),
                  pl.BlockSpec((1, d), lambda i: (0, 0)), modv, modv],
        out_specs=[tile, tile],
        out_shape=[jax.ShapeDtypeStruct((t, d), F32), jax.ShapeDtypeStruct((t, d), BF16)],
        compiler_params=_params(1),
        name="embed_norm",
    )(ctx_rows, x_rows, w.reshape(1, d), shift, scale)


def _top2_route(a, rwh_ref, rwl_ref, rb_ref):
    a_hi = a.astype(BF16)
    a_lo = (a - a_hi.astype(F32)).astype(BF16)
    w_hi = rwh_ref[...]
    logits = (jnp.dot(a_hi, w_hi, preferred_element_type=F32)
              + (jnp.dot(a_lo, w_hi, preferred_element_type=F32)
                 + jnp.dot(a_hi, rwl_ref[...], preferred_element_type=F32))) + rb_ref[...]
    lane = lax.broadcasted_iota(jnp.int32, logits.shape, 1)
    logits = jnp.where(lane < N_EXPERTS, logits, NEG)
    big = jnp.int32(1 << 20)
    m1 = jnp.max(logits, axis=-1, keepdims=True)
    i1 = jnp.min(jnp.where(logits == m1, lane, big), axis=-1, keepdims=True)
    rest = jnp.where(lane == i1, NEG, logits)
    m2 = jnp.max(rest, axis=-1, keepdims=True)
    i2 = jnp.min(jnp.where(rest == m2, lane, big), axis=-1, keepdims=True)
    e2 = jnp.exp(m2 - m1)
    w1 = 1.0 / (1.0 + e2)
    w2 = e2 / (1.0 + e2)
    r = jnp.where(lane == 0, i1.astype(F32), 0.0)
    r = jnp.where(lane == 1, i2.astype(F32), r)
    r = jnp.where(lane == 2, w1, r)
    r = jnp.where(lane == 3, w2, r)
    return r


def _seg_index(tiles_per_seq):
    return lambda i: (i + tiles_per_seq - 1) // tiles_per_seq


def _norm_modulate(h, w, shift, scale, tiles_per_seq, out_dtype=BF16, row_offset_tiles=0, n_rows=None):
    t, d = h.shape
    n_rows = t if n_rows is None else n_rows
    seg = _seg_index(tiles_per_seq)
    off = row_offset_tiles
    return pl.pallas_call(
        _normmod_kernel,
        grid=(n_rows // ROW_TILE,),
        in_specs=[pl.BlockSpec((ROW_TILE, d), lambda i: (i + off, 0)),
                  pl.BlockSpec((1, d), lambda i: (0, 0)),
                  pl.BlockSpec((None, 1, d), lambda i: (seg(i + off), 0, 0)),
                  pl.BlockSpec((None, 1, d), lambda i: (seg(i + off), 0, 0))],
        out_specs=pl.BlockSpec((ROW_TILE, d), lambda i: (i, 0)),
        out_shape=jax.ShapeDtypeStruct((n_rows, d), out_dtype),
        compiler_params=_params(1),
        name="norm_modulate",
    )(h, w.reshape(1, d), shift, scale)


def _mm_kernel(x_ref, w_ref, o_ref, wb_ref):
    @pl.when(pl.program_id(1) == 0)
    def _():
        wb_ref[...] = w_ref[...].astype(BF16)

    o_ref[...] = jnp.dot(x_ref[...], wb_ref[...], preferred_element_type=F32).astype(o_ref.dtype)


def _matmul(x, w3, layer, n_cols, tn=512, out_dtype=F32):
    t, k = x.shape
    return pl.pallas_call(
        _mm_kernel,
        grid=(n_cols // tn, t // ROW_TILE),
        in_specs=[pl.BlockSpec((ROW_TILE, k), lambda j, i: (i, 0)),
                  pl.BlockSpec((None, k, tn), lambda j, i: (layer, 0, j))],
        out_specs=pl.BlockSpec((ROW_TILE, tn), lambda j, i: (i, j)),
        out_shape=jax.ShapeDtypeStruct((t, n_cols), out_dtype),
        scratch_shapes=[pltpu.VMEM((k, tn), BF16)],
        compiler_params=_params(2),
        name="matmul",
    )(x, w3)


def _mlstm_head_out_rows(hf_ref, hb_ref, op_ref, mw_ref):
    parts = []
    for hd in range(M_HEADS):
        sl = slice(hd * M_V, (hd + 1) * M_V)
        x = hf_ref[:, sl] + hb_ref[:, sl]
        mu = jnp.mean(x, axis=-1, keepdims=True)
        xc = x - mu
        var = jnp.mean(xc * xc, axis=-1, keepdims=True)
        y = xc * lax.rsqrt(var + NORM_EPS) * mw_ref[:, sl]
        parts.append((y * _sigmoid(op_ref[:, sl])).astype(BF16))
    return jnp.concatenate(parts, axis=1)


def _out_proj_kernel(*refs, route, head_out):
    refs = list(refs)
    if head_out:
        x1 = _mlstm_head_out_rows(*refs[:4])
        refs = refs[4:]
    else:
        x1 = refs.pop(0)[...]
    if route:
        (x2_ref, w1_ref, w2_ref, h_ref, g_ref, nw_ref, sh_ref, sc_ref, rwh_ref, rwl_ref, rb_ref,
         ho_ref, a_ref, r_ref) = refs
    else:
        x2_ref, w1_ref, w2_ref, h_ref, g_ref, nw_ref, sh_ref, sc_ref, ho_ref, a_ref = refs
    y = jnp.dot(x1, w1_ref[...], preferred_element_type=F32)
    y = y + jnp.dot(x2_ref[...], w2_ref[...], preferred_element_type=F32)
    hn = h_ref[...] + g_ref[...] * y
    ho_ref[...] = hn
    a = _normmod(hn, nw_ref[...], sh_ref[...], sc_ref[...])
    a_ref[...] = a.astype(a_ref.dtype)
    if route:
        r_ref[...] = _top2_route(a, rwh_ref, rwl_ref, rb_ref)


def _out_proj_norm(x1, x2, w3b, layer, h, gate, norm_w, shift, scale, tiles_per_seq, router=None):
    t, k2 = x2.shape
    k1 = k2
    d = w3b.shape[2]
    seg = _seg_index(tiles_per_seq)
    row = lambda w: pl.BlockSpec((ROW_TILE, w), lambda i: (i, 0))
    modv = pl.BlockSpec((None, 1, d), lambda i: (seg(i), 0, 0))
    head_out = isinstance(x1, tuple)
    if head_out:
        h_f, h_b, z_rest, mnorm_w = x1
        assert h_f.shape == (t, k1)
        x1_specs = [row(k1), row(k1), row(k1), pl.BlockSpec((1, k1), lambda i: (0, 0))]
        x1_args = [h_f, h_b, z_rest, mnorm_w.reshape(1, k1)]
    else:
        assert x1.shape == (t, k1)
        x1_specs, x1_args = [row(k1)], [x1]
    in_specs = x1_specs + [
        row(k2),
        pl.BlockSpec((None, k1, d), lambda i: (layer, 0, 0), pipeline_mode=pl.Buffered(1)),
        pl.BlockSpec((None, k2, d), lambda i: (layer, 1, 0), pipeline_mode=pl.Buffered(1)),
        row(d), modv, pl.BlockSpec((1, d), lambda i: (0, 0)), modv, modv]
    args = x1_args + [x2, w3b, w3b, h, gate, norm_w.reshape(1, d), shift, scale]
    out_specs = [row(d), row(d)]
    out_shape = [jax.ShapeDtypeStruct((t, d), F32), jax.ShapeDtypeStruct((t, d), BF16)]
    if router is not None:
        router_w, router_b = router
        rw = jnp.pad(router_w.astype(F32), ((0, 0), (0, 128 - N_EXPERTS)))
        rw_hi = rw.astype(BF16)
        rw_lo = (rw - rw_hi.astype(F32)).astype(BF16)
        in_specs += [pl.BlockSpec((d, 128), lambda i: (0, 0)), pl.BlockSpec((d, 128), lambda i: (0, 0)),
                     pl.BlockSpec((1, 128), lambda i: (0, 0))]
        args += [rw_hi, rw_lo, jnp.pad(router_b, (0, 128 - N_EXPERTS)).reshape(1, 128)]
        out_specs.append(row(128))
        out_shape.append(jax.ShapeDtypeStruct((t, 128), F32))
    return pl.pallas_call(
        functools.partial(_out_proj_kernel, route=router is not None, head_out=head_out),
        grid=(t // ROW_TILE,),
        in_specs=in_specs,
        out_specs=out_specs,
        out_shape=out_shape,
        compiler_params=_params(1),
        name="out_proj_norm",
    )(*args)


def _ffn_up_kernel(x_ref, wg_ref, wu_ref, o_ref, wgb_ref, wub_ref):
    @pl.when(pl.program_id(1) == 0)
    def _():
        wgb_ref[...] = wg_ref[...].astype(BF16)
        wub_ref[...] = wu_ref[...].astype(BF16)

    x = x_ref[...]
    g = jnp.dot(x, wgb_ref[...], preferred_element_type=F32)
    u = jnp.dot(x, wub_ref[...], preferred_element_type=F32)
    o_ref[...] = (_silu(g) * u).astype(o_ref.dtype)


def _ffn_up(x, w_gate, w_up, layer, tf=512):
    t, k = x.shape
    f = w_gate.shape[2]
    tm = 3 * ROW_TILE if t % (3 * ROW_TILE) == 0 else ROW_TILE
    return pl.pallas_call(
        _ffn_up_kernel,
        grid=(f // tf, t // tm),
        in_specs=[pl.BlockSpec((tm, k), lambda j, i: (i, 0)),
                  pl.BlockSpec((None, k, tf), lambda j, i: (layer, 0, j)),
                  pl.BlockSpec((None, k, tf), lambda j, i: (layer, 0, j))],
        out_specs=pl.BlockSpec((tm, tf), lambda j, i: (i, j)),
        out_shape=jax.ShapeDtypeStruct((t, f), BF16),
        scratch_shapes=[pltpu.VMEM((k, tf), BF16), pltpu.VMEM((k, tf), BF16)],
        compiler_params=_params(2),
        name="ffn_up",
    )(x, w_gate, w_up)


def _mm_resid_kernel(x_ref, w_ref, h_ref, g_ref, o_ref, wb_ref):
    @pl.when(pl.program_id(1) == 0)
    def _():
        wb_ref[...] = w_ref[...].astype(BF16)

    y = jnp.dot(x_ref[...], wb_ref[...], preferred_element_type=F32)
    o_ref[...] = h_ref[...] + g_ref[...] * y


def _ffn_down_residual(x, w3, layer, h, gate, tiles_per_seq, tn=512):
    t, k = x.shape
    n = w3.shape[2]
    seg = _seg_index(tiles_per_seq)
    return pl.pallas_call(
        _mm_resid_kernel,
        grid=(n // tn, t // ROW_TILE),
        in_specs=[pl.BlockSpec((ROW_TILE, k), lambda j, i: (i, 0)),
                  pl.BlockSpec((None, k, tn), lambda j, i: (layer, 0, j)),
                  pl.BlockSpec((ROW_TILE, tn), lambda j, i: (i, j)),
                  pl.BlockSpec((None, 1, tn), lambda j, i: (seg(i), 0, j))],
        out_specs=pl.BlockSpec((ROW_TILE, tn), lambda j, i: (i, j)),
        out_shape=jax.ShapeDtypeStruct((t, n), F32),
        scratch_shapes=[pltpu.VMEM((k, tn), BF16)],
        compiler_params=_params(2),
        name="ffn_down_residual",
    )(x, w3, h, gate)


def _rope(x, cs):
    x = x.astype(F32)
    return x * cs[:, :HEAD_DIM] + pltpu.roll(x, HEAD_DIM // 2, axis=1) * cs[:, HEAD_DIM:]


def _attend(q_all, k_parts, v_parts, sink_ref, o_ref, mask_fn):
    group = Q_HEADS // KV_HEADS
    scale = HEAD_DIM ** -0.5
    for g in range(KV_HEADS):
        qg = jnp.concatenate([q_all[:, (g * group + r) * HEAD_DIM:(g * group + r + 1) * HEAD_DIM]
                              for r in range(group)], axis=0).astype(BF16)
        kg = jnp.concatenate([kp[:, g * HEAD_DIM:(g + 1) * HEAD_DIM] for kp in k_parts], axis=0).astype(BF16)
        vg = jnp.concatenate([vp[:, g * HEAD_DIM:(g + 1) * HEAD_DIM] for vp in v_parts], axis=0).astype(BF16)
        s = lax.dot_general(qg, kg, (((1,), (1,)), ((), ())), preferred_element_type=F32) * scale
        if mask_fn is not None:
            s = jnp.where(mask_fn(s.shape), s, NEG)
        sink = sink_ref[g]
        m = jnp.maximum(jnp.max(s, axis=-1, keepdims=True), sink)
        p = jnp.exp(s - m)
        denom = jnp.sum(p, axis=-1, keepdims=True) + jnp.exp(sink - m)
        o = jnp.dot(p.astype(BF16), vg, preferred_element_type=F32) / denom
        for r in range(group):
            hq = g * group + r
            o_ref[:, hq * HEAD_DIM:(hq + 1) * HEAD_DIM] = o[r * ATT_BLOCK:(r + 1) * ATT_BLOCK].astype(o_ref.dtype)


def _attn_kernel(q_ref, kp_ref, kc_ref, kn_ref, kx_ref, vp_ref, vc_ref, vn_ref, vx_ref,
                 csp_ref, csc_ref, csn_ref, sink_ref, o_ref, *, n_blocks, ctx_blocks):
    step = pl.program_id(1)

    @pl.when(step < ctx_blocks)
    def _():
        _attend(q_ref[...], [kx_ref[...]], [vx_ref[...]], sink_ref, o_ref, None)

    @pl.when(step >= ctx_blocks)
    def _():
        _win_attn_body(q_ref, kp_ref, kc_ref, kn_ref, kx_ref, vp_ref, vc_ref, vn_ref, vx_ref,
                       csp_ref, csc_ref, csn_ref, sink_ref, o_ref, step - ctx_blocks, n_blocks)


def _win_attn_body(q_ref, kp_ref, kc_ref, kn_ref, kx_ref, vp_ref, vc_ref, vn_ref, vx_ref,
                   csp_ref, csc_ref, csn_ref, sink_ref, o_ref, n, n_blocks):
    csc = csc_ref[...]
    q_all = jnp.concatenate([_rope(q_ref[:, h * HEAD_DIM:(h + 1) * HEAD_DIM], csc) for h in range(Q_HEADS)], axis=1)

    def rope_kv(k_ref, cs):
        return jnp.concatenate([_rope(k_ref[:, g * HEAD_DIM:(g + 1) * HEAD_DIM], cs) for g in range(KV_HEADS)], axis=1)

    k_parts = [rope_kv(kp_ref, csp_ref[...]), rope_kv(kc_ref, csc), rope_kv(kn_ref, csn_ref[...]), kx_ref[...]]
    v_parts = [vp_ref[...], vc_ref[...], vn_ref[...], vx_ref[...]]

    def mask_fn(shape):
        row = lax.broadcasted_iota(jnp.int32, shape, 0) & (ATT_BLOCK - 1)
        col = lax.broadcasted_iota(jnp.int32, shape, 1)
        band = (col >= row) & (col <= row + 2 * ATT_BLOCK)
        ok_prev = (col >= ATT_BLOCK) | (n > 0)
        ok_next = (col < 2 * ATT_BLOCK) | (n < n_blocks - 1)
        return (band & ok_prev & ok_next) | (col >= 3 * ATT_BLOCK)

    _attend(q_all, k_parts, v_parts, sink_ref, o_ref, mask_fn)


def _sink_rows(sink):
    group = Q_HEADS // KV_HEADS
    return jnp.repeat(sink.astype(F32).reshape(KV_HEADS, group), ATT_BLOCK, axis=1).reshape(
        KV_HEADS, group * ATT_BLOCK, 1)


def _window_attention(z, cs, sink, batch, ctx_len, seq):
    t = z.shape[0]
    nb = seq // ATT_BLOCK
    cb = ctx_len // ATT_BLOCK
    base = batch * cb
    qw = Q_HEADS * HEAD_DIM
    kw = KV_HEADS * HEAD_DIM
    kcol = qw // kw
    vcol = kcol + 1
    sink_rows = _sink_rows(sink)

    def lat(s):
        return jnp.clip(s - cb, 0, nb - 1)

    def q_block(b, s):
        return jnp.where(s < cb, b * cb + s, base + b * nb + lat(s))

    def kv(shift, col):
        return pl.BlockSpec((ATT_BLOCK, kw), lambda b, s: (base + b * nb + lat(s + shift), col))

    def rot(shift):
        return pl.BlockSpec((ATT_BLOCK, 2 * HEAD_DIM), lambda b, s: (lat(s + shift), 0))

    return pl.pallas_call(
        functools.partial(_attn_kernel, n_blocks=nb, ctx_blocks=cb),
        grid=(batch, cb + nb),
        in_specs=[pl.BlockSpec((ATT_BLOCK, qw), lambda b, s: (q_block(b, s), 0)),
                  kv(-1, kcol), kv(0, kcol), kv(1, kcol),
                  pl.BlockSpec((ctx_len, kw), lambda b, s: (b, kcol)),
                  kv(-1, vcol), kv(0, vcol), kv(1, vcol),
                  pl.BlockSpec((ctx_len, kw), lambda b, s: (b, vcol)),
                  rot(-1), rot(0), rot(1),
                  pl.BlockSpec(sink_rows.shape, lambda b, s: (0, 0, 0))],
        out_specs=pl.BlockSpec((ATT_BLOCK, qw), lambda b, s: (q_block(b, s), 0)),
        out_shape=jax.ShapeDtypeStruct((t, qw), BF16),
        compiler_params=_params(2),
        name="window_attention",
    )(z, z, z, z, z, z, z, z, z, cs, cs, cs, sink_rows)


def _seq_edges(t, batch, ctx_tiles, lat_tiles):
    n_ctx = batch * ctx_tiles
    u = t - n_ctx
    is_ctx = t < n_ctx
    first = jnp.where(is_ctx, lax.rem(t, ctx_tiles) == 0, lax.rem(u, lat_tiles) == 0)
    last = jnp.where(is_ctx, lax.rem(t, ctx_tiles) == ctx_tiles - 1, lax.rem(u, lat_tiles) == lat_tiles - 1)
    return first, last


def _fill_ext(ext_ref, cur, prev, nxt, first, last):
    ext_ref[HALO:HALO + CONV_TILE, :] = cur
    ext_ref[0:HALO, :] = jnp.where(first, 0.0, prev)
    ext_ref[HALO + CONV_TILE:, :] = jnp.where(last, 0.0, nxt)


def _dwconv_rows(ext_ref, w_ref, taps, r0, n_rows):
    pad = taps // 2
    acc = None
    for k in range(taps):
        start = HALO + r0 + k - pad
        term = w_ref[k:k + 1, :] * ext_ref[start:start + n_rows, :]
        acc = term if acc is None else acc + term
    return acc


CONV_ROWS = 32


def _short_conv_kernel(bg_ref, cg_ref, xb_ref, cgp_ref, xbp_ref, cgn_ref, xbn_ref, w_ref, o_ref, ext_ref,
                       *, batch, ctx_tiles, lat_tiles):
    first, last = _seq_edges(pl.program_id(0), batch, ctx_tiles, lat_tiles)
    def prod(a_ref, b_ref):
        return a_ref[...].astype(F32) * b_ref[...].astype(F32)

    _fill_ext(ext_ref, prod(cg_ref, xb_ref), prod(cgp_ref, xbp_ref), prod(cgn_ref, xbn_ref), first, last)
    for r0 in range(0, CONV_TILE, CONV_ROWS):
        conv = _dwconv_rows(ext_ref, w_ref, CONV_SHORT, r0, CONV_ROWS)
        o_ref[r0:r0 + CONV_ROWS, :] = (bg_ref[r0:r0 + CONV_ROWS, :].astype(F32) * conv).astype(o_ref.dtype)


def _halo_specs(width, col, halo_per_tile, n_halo_blocks):
    prev = pl.BlockSpec((HALO, width), lambda t, c: (jnp.maximum(t * halo_per_tile - 1, 0), col(c)))
    nxt = pl.BlockSpec((HALO, width), lambda t, c: (jnp.minimum((t + 1) * halo_per_tile, n_halo_blocks - 1), col(c)))
    return prev, nxt


def _short_conv(z, conv_w3, layer, batch, ctx_len, seq):
    t = z.shape[0]
    cw = 512
    n_c = 1024 // cw
    bg0, cg0, xb0 = 1536 // cw, 2560 // cw, 3584 // cw
    hpt = CONV_TILE // HALO
    nh = t // HALO
    cgp, cgn = _halo_specs(cw, lambda c: cg0 + c, hpt, nh)
    xbp, xbn = _halo_specs(cw, lambda c: xb0 + c, hpt, nh)
    kern = functools.partial(_short_conv_kernel, batch=batch, ctx_tiles=ctx_len // CONV_TILE,
                             lat_tiles=seq // CONV_TILE)
    return pl.pallas_call(
        kern,
        grid=(t // CONV_TILE, n_c),
        in_specs=[pl.BlockSpec((CONV_TILE, cw), lambda t_, c: (t_, bg0 + c)),
                  pl.BlockSpec((CONV_TILE, cw), lambda t_, c: (t_, cg0 + c)),
                  pl.BlockSpec((CONV_TILE, cw), lambda t_, c: (t_, xb0 + c)),
                  cgp, xbp, cgn, xbn,
                  pl.BlockSpec((None, CONV_SHORT, cw), lambda t_, c: (layer, 0, c))],
        out_specs=pl.BlockSpec((CONV_TILE, cw), lambda t_, c: (t_, c)),
        out_shape=jax.ShapeDtypeStruct((t, 1024), BF16),
        scratch_shapes=[pltpu.VMEM((CONV_TILE + 2 * HALO, cw), F32)],
        compiler_params=_params(2),
        name="short_conv",
    )(z, z, z, z, z, z, z, conv_w3)


SUBLANES = 8
SHIFT_ROWS = CONV_TILE + 2 * HALO - SUBLANES


def _dwconv_rows_aligned(ext_ref, sh_ref, w8_ref, taps, r0, n_rows):
    pad = taps // 2
    width = ext_ref.shape[1]
    acc = None
    for k in range(taps):
        start = HALO + r0 + k - pad
        b = start % SUBLANES
        a8 = start - b
        src = ext_ref[a8:a8 + n_rows, :] if b == 0 else sh_ref[b - 1, a8:a8 + n_rows, :]
        term = (src.reshape(n_rows // SUBLANES, SUBLANES, width) * w8_ref[k][None]).reshape(n_rows, width)
        acc = term if acc is None else acc + term
    return acc


def _conformer_kernel(a_ref, g_ref, ap_ref, gp_ref, an_ref, gn_ref, w8_ref, b_ref, lw_ref, lb_ref, o_ref,
                      ext_ref, sh_ref, *, batch, ctx_tiles, lat_tiles):
    first, last = _seq_edges(pl.program_id(0), batch, ctx_tiles, lat_tiles)
    _fill_ext(ext_ref, a_ref[...] * _sigmoid(g_ref[...]), ap_ref[...] * _sigmoid(gp_ref[...]),
              an_ref[...] * _sigmoid(gn_ref[...]), first, last)
    for b in range(1, SUBLANES):
        sh_ref[b - 1] = ext_ref[b:b + SHIFT_ROWS, :]
    for r0 in range(0, CONV_TILE, CONV_ROWS):
        u = _dwconv_rows_aligned(ext_ref, sh_ref, w8_ref, CONV_LONG, r0, CONV_ROWS) + b_ref[...]
        mu = jnp.mean(u, axis=-1, keepdims=True)
        uc = u - mu
        var = jnp.mean(uc * uc, axis=-1, keepdims=True)
        y = uc * lax.rsqrt(var + NORM_EPS) * lw_ref[...] + lb_ref[...]
        o_ref[r0:r0 + CONV_ROWS, :] = _silu(y).astype(o_ref.dtype)


def _conformer_conv(z, a_col, g_col, dw_w3, dw_b, ln_w, ln_b, layer, batch, ctx_len, seq):
    t = z.shape[0]
    cw = 1024
    hpt = CONV_TILE // HALO
    nh = t // HALO
    ap, an = _halo_specs(cw, lambda c: a_col, hpt, nh)
    gp, gn = _halo_specs(cw, lambda c: g_col, hpt, nh)
    kern = functools.partial(_conformer_kernel, batch=batch, ctx_tiles=ctx_len // CONV_TILE,
                             lat_tiles=seq // CONV_TILE)
    vec = pl.BlockSpec((None, 1, cw), lambda t_, c: (layer, 0, 0))
    n_layers = dw_b.shape[0]
    return pl.pallas_call(
        kern,
        grid=(t // CONV_TILE, 1),
        in_specs=[pl.BlockSpec((CONV_TILE, cw), lambda t_, c: (t_, a_col)),
                  pl.BlockSpec((CONV_TILE, cw), lambda t_, c: (t_, g_col)),
                  ap, gp, an, gn,
                  pl.BlockSpec((None, CONV_LONG, SUBLANES, cw), lambda t_, c: (layer, 0, 0, 0)),
                  vec, vec, vec],
        out_specs=pl.BlockSpec((CONV_TILE, cw), lambda t_, c: (t_, 0)),
        out_shape=jax.ShapeDtypeStruct((t, cw), BF16),
        scratch_shapes=[pltpu.VMEM((CONV_TILE + 2 * HALO, cw), F32),
                        pltpu.VMEM((SUBLANES - 1, SHIFT_ROWS, cw), F32)],
        compiler_params=_params(2),
        name="conformer_conv",
    )(z, z, z, z, z, z, jnp.broadcast_to(dw_w3[:, :, None, :], (n_layers, CONV_LONG, SUBLANES, cw)),
      dw_b.reshape(n_layers, 1, cw), ln_w.reshape(n_layers, 1, cw), ln_b.reshape(n_layers, 1, cw))


def _log_sigmoid(x):
    return jnp.minimum(x, 0.0) - jnp.log(1.0 + jnp.exp(-jnp.abs(x)))


def _mlstm_kernel(qf_ref, kf_ref, vf_ref, gcf_ref, grf_ref, qb_ref, kb_ref, vb_ref, gcb_ref, grb_ref,
                  bc_ref, br_ref, of_ref, ob_ref, *state_refs):
    n_chain = 2 * M_HEADS
    cta_ref, m_ref = state_refs[:n_chain], state_refs[n_chain:]

    @pl.when(pl.program_id(1) == 0)
    def _():
        for ref in state_refs:
            ref[...] = jnp.zeros_like(ref)

    ln = M_CHUNK
    row = lax.broadcasted_iota(jnp.int32, (ln, ln), 0)
    col = lax.broadcasted_iota(jnp.int32, (ln, ln), 1)
    lower = col <= row
    upper = col >= row
    _mlstm_direction(qf_ref, kf_ref, vf_ref, gcf_ref[...] + bc_ref[0], grf_ref[...] + br_ref[0], lower, upper,
                     of_ref, cta_ref, m_ref, 0)
    _mlstm_direction(qb_ref, kb_ref, vb_ref, gcb_ref[...] + bc_ref[1], grb_ref[...] + br_ref[1], upper, lower,
                     ob_ref, cta_ref, m_ref, M_HEADS)


def _mlstm_direction(q_ref, k_ref, v_ref, gc, gr, mask, mask_t, o_ref, cta_ref, m_ref, state0):
    ln = M_CHUNK
    one_col = lax.broadcasted_iota(jnp.int32, (ln, 128), 1) == 0
    ones_blk = jnp.where(one_col, 1.0, 0.0).astype(BF16)
    for h in range(M_HEADS):
        st = state0 + h
        ig_c = gc[:, h:h + 1]
        lf_c = _log_sigmoid(gc[:, M_HEADS + h:M_HEADS + h + 1])
        ig_r = gr[h:h + 1, :]
        lf_r = _log_sigmoid(gr[M_HEADS + h:M_HEADS + h + 1, :])
        qb = (q_ref[:, h * M_QK:(h + 1) * M_QK] * (M_QK ** -0.5)).astype(BF16)
        k = k_ref[:, h * M_QK:(h + 1) * M_QK]
        v = v_ref[:, h * M_V:(h + 1) * M_V]
        cum_c = jnp.sum(jnp.where(mask, lf_r, 0.0), axis=1, keepdims=True)
        cum_r = jnp.sum(jnp.where(mask_t, lf_c, 0.0), axis=0, keepdims=True)
        total = jnp.sum(lf_r, axis=1, keepdims=True)
        m_s = m_ref[st][...]
        g = jnp.where(mask, ig_r - cum_r, NEG)
        mm = jnp.maximum(m_s, jnp.max(g, axis=1, keepdims=True))
        w_inter = jnp.exp(m_s - mm)
        s = lax.dot_general(qb, k.astype(BF16), (((1,), (1,)), ((), ())), preferred_element_type=F32) * jnp.exp(g - mm)
        cta = cta_ref[st][...]
        va = jnp.concatenate([v.astype(BF16), ones_blk], axis=1)
        inter = jnp.dot(qb, cta.astype(BF16), preferred_element_type=F32)
        intra = jnp.dot(s.astype(BF16), va, preferred_element_type=F32)
        num = w_inter * inter[:, :M_V] + intra[:, :M_V]
        den = w_inter * inter[:, M_V:M_V + 1] + intra[:, M_V:M_V + 1]
        o_ref[:, h * M_V:(h + 1) * M_V] = num / jnp.maximum(jnp.abs(den), jnp.exp(-(cum_c + mm)))
        gl = total - cum_c + ig_c
        m_new = jnp.maximum(total + m_s, jnp.max(gl, axis=0, keepdims=True))
        wg = jnp.exp(gl - m_new)
        decay = jnp.exp(total + m_s - m_new)
        wva = jnp.concatenate([(wg * v).astype(BF16), jnp.where(one_col, wg, 0.0).astype(BF16)], axis=1)
        cta_ref[st][...] = decay * cta + jnp.dot(k.astype(BF16).T, wva, preferred_element_type=F32)
        m_ref[st][...] = m_new


def _mlstm(zq, gates, gate_b, batch, ctx_len, seq):
    t = zq.shape[0]
    cb = ctx_len // M_CHUNK
    nc = seq // M_CHUNK
    base = batch * cb
    steps = cb + nc
    half = 2 * M_HEADS
    g_dir = gates.reshape(t, 2, half).transpose(1, 0, 2)
    g_dir_t = g_dir.transpose(0, 2, 1)
    b_dir = gate_b.astype(F32).reshape(2, 1, half)
    b_dir_t = b_dir.transpose(0, 2, 1)

    def rb_f(b, i):
        return jnp.where(i < cb, b * cb + i, base + b * nc + (i - cb))

    def rb_b(b, i):
        return jnp.where(i < cb, b * cb + (cb - 1 - i), base + b * nc + (nc - 1 - (i - cb)))

    qk_w = M_HEADS * M_QK
    v_w = M_HEADS * M_V

    def chunk_specs(d, rb):
        return [pl.BlockSpec((M_CHUNK, qk_w), lambda b, i: (rb(b, i), 0)),
                pl.BlockSpec((M_CHUNK, qk_w), lambda b, i: (rb(b, i), 1)),
                pl.BlockSpec((M_CHUNK, v_w), lambda b, i: (rb(b, i), 1)),
                pl.BlockSpec((None, M_CHUNK, half), lambda b, i: (d, rb(b, i), 0)),
                pl.BlockSpec((None, half, M_CHUNK), lambda b, i: (d, 0, rb(b, i)))]

    return pl.pallas_call(
        _mlstm_kernel,
        grid=(batch, steps),
        in_specs=chunk_specs(0, rb_f) + chunk_specs(1, rb_b) + [
            pl.BlockSpec((2, 1, half), lambda b, i: (0, 0, 0)),
            pl.BlockSpec((2, half, 1), lambda b, i: (0, 0, 0))],
        out_specs=[pl.BlockSpec((M_CHUNK, v_w), lambda b, i: (rb_f(b, i), 0)),
                   pl.BlockSpec((M_CHUNK, v_w), lambda b, i: (rb_b(b, i), 0))],
        out_shape=[jax.ShapeDtypeStruct((t, v_w), F32), jax.ShapeDtypeStruct((t, v_w), F32)],
        scratch_shapes=([pltpu.VMEM((M_QK, M_V + 128), F32)] * (2 * M_HEADS)
                        + [pltpu.VMEM((1, 1), F32)] * (2 * M_HEADS)),
        compiler_params=_params(2),
        name="mlstm",
    )(zq, zq, zq, g_dir, g_dir_t, zq, zq, zq, g_dir, g_dir_t, b_dir, b_dir_t)


def _fresh_weights(be_ref, rb):
    return (rb == 0) | (be_ref[rb] != be_ref[jnp.maximum(rb - 1, 0)])


def _moe_up_kernel(be_ref, nv_ref, x_ref, wg_ref, wu_ref, o_ref, wgb_ref, wub_ref):
    rb = pl.program_id(1)
    valid = rb < nv_ref[0]

    @pl.when(valid & _fresh_weights(be_ref, rb))
    def _():
        wgb_ref[...] = wg_ref[...].astype(BF16)
        wub_ref[...] = wu_ref[...].astype(BF16)

    @pl.when(valid)
    def _():
        x = x_ref[...]
        g = jnp.dot(x, wgb_ref[...], preferred_element_type=F32)
        u = jnp.dot(x, wub_ref[...], preferred_element_type=F32)
        o_ref[...] = (_silu(g) * u).astype(o_ref.dtype)

    @pl.when(jnp.logical_not(valid))
    def _():
        o_ref[...] = jnp.zeros_like(o_ref)


def _moe_down_kernel(be_ref, nv_ref, x_ref, w_ref, o_ref, wb_ref):
    rb = pl.program_id(1)
    valid = rb < nv_ref[0]

    @pl.when(valid & _fresh_weights(be_ref, rb))
    def _():
        wb_ref[...] = w_ref[...].astype(BF16)

    @pl.when(valid)
    def _():
        o_ref[...] = jnp.dot(x_ref[...], wb_ref[...], preferred_element_type=F32).astype(o_ref.dtype)

    @pl.when(jnp.logical_not(valid))
    def _():
        o_ref[...] = jnp.zeros_like(o_ref)


def _moe_experts(x_sorted, block_expert, n_valid, w_gate, w_up, w_down, layer, tf=512, tn=512):
    p, d = x_sorted.shape
    f = w_gate.shape[3]
    nblk = p // ROW_TILE

    def row(rb, nv):
        return jnp.minimum(rb, nv[0] - 1)

    g = pl.pallas_call(
        _moe_up_kernel,
        grid_spec=pltpu.PrefetchScalarGridSpec(
            num_scalar_prefetch=2,
            grid=(f // tf, nblk),
            in_specs=[pl.BlockSpec((ROW_TILE, d), lambda j, rb, be, nv: (row(rb, nv), 0)),
                      pl.BlockSpec((None, None, d, tf), lambda j, rb, be, nv: (layer, be[rb], 0, j)),
                      pl.BlockSpec((None, None, d, tf), lambda j, rb, be, nv: (layer, be[rb], 0, j))],
            out_specs=pl.BlockSpec((ROW_TILE, tf), lambda j, rb, be, nv: (rb, j)),
            scratch_shapes=[pltpu.VMEM((d, tf), BF16), pltpu.VMEM((d, tf), BF16)]),
        out_shape=jax.ShapeDtypeStruct((p, f), BF16),
        compiler_params=_params(2),
        name="moe_up",
    )(block_expert, n_valid, x_sorted, w_gate, w_up)
    return pl.pallas_call(
        _moe_down_kernel,
        grid_spec=pltpu.PrefetchScalarGridSpec(
            num_scalar_prefetch=2,
            grid=(d // tn, nblk),
            in_specs=[pl.BlockSpec((ROW_TILE, f), lambda j, rb, be, nv: (row(rb, nv), 0)),
                      pl.BlockSpec((None, None, f, tn), lambda j, rb, be, nv: (layer, be[rb], 0, j))],
            out_specs=pl.BlockSpec((ROW_TILE, tn), lambda j, rb, be, nv: (rb, j)),
            scratch_shapes=[pltpu.VMEM((f, tn), BF16)]),
        out_shape=jax.ShapeDtypeStruct((p, d), BF16),
        compiler_params=_params(2),
        name="moe_down",
    )(block_expert, n_valid, g, w_down)


def _routing_plan(route):
    t = route.shape[0]
    tm = ROW_TILE
    n_assign = TOP_K * t
    nblk = (n_assign + N_EXPERTS * (tm - 1)) // tm
    e = route[:, :TOP_K].astype(jnp.int32).reshape(-1)
    onehot = (e[:, None] == jnp.arange(N_EXPERTS, dtype=jnp.int32)[None, :]).astype(jnp.int32)
    csum = jnp.cumsum(onehot, axis=0)
    counts = csum[-1]
    padded = ((counts + tm - 1) // tm) * tm
    ends = jnp.cumsum(padded)
    starts = ends - padded
    dest = jnp.sum((csum - onehot + starts[None, :]) * onehot, axis=1)
    src_tok = (jnp.arange(nblk * tm, dtype=jnp.int32) % t).at[dest].set(
        jnp.arange(n_assign, dtype=jnp.int32) // TOP_K, unique_indices=True, mode="promise_in_bounds")
    n_valid = (ends[-1] // tm).astype(jnp.int32)
    blk_start = jnp.arange(nblk, dtype=jnp.int32) * tm
    be = jnp.sum((ends[None, :] <= blk_start[:, None]).astype(jnp.int32), axis=1)
    be = jnp.minimum(be, N_EXPERTS - 1).astype(jnp.int32)
    return src_tok, be, n_valid.reshape(1), dest.reshape(t, TOP_K)


def _combine_kernel(h_ref, y0_ref, y1_ref, r_ref, g_ref, nw_ref, sh_ref, sc_ref, *out_refs):
    r = r_ref[...]
    moe = r[:, TOP_K:TOP_K + 1] * y0_ref[...].astype(F32) + r[:, TOP_K + 1:TOP_K + 2] * y1_ref[...].astype(F32)
    hn = h_ref[...] + g_ref[...] * moe
    if len(out_refs) == 2:
        out_refs[0][...] = hn
    out_refs[-1][...] = _normmod(hn, nw_ref[...], sh_ref[...], sc_ref[...]).astype(out_refs[-1].dtype)


def _combine_norm(h, y_pair, route, gate, norm_w, shift, scale, tiles_per_seq, row_offset_tiles, emit_h, a_dtype):
    d = h.shape[1]
    n_rows = y_pair.shape[0] // TOP_K
    n_tiles = n_rows // ROW_TILE
    off = row_offset_tiles
    seg = _seg_index(tiles_per_seq)
    shifted = lambda w: pl.BlockSpec((ROW_TILE, w), lambda i: (i + off, 0))
    tile = pl.BlockSpec((ROW_TILE, d), lambda i: (i, 0))
    tile1 = pl.BlockSpec((ROW_TILE, d), lambda i: (i + n_tiles, 0))
    modv = pl.BlockSpec((None, 1, d), lambda i: (seg(i + off), 0, 0))
    out_specs = [tile, tile] if emit_h else [tile]
    out_shape = [jax.ShapeDtypeStruct((n_rows, d), a_dtype)]
    if emit_h:
        out_shape.insert(0, jax.ShapeDtypeStruct((n_rows, d), F32))
    return pl.pallas_call(
        _combine_kernel,
        grid=(n_rows // ROW_TILE,),
        in_specs=[shifted(d), tile, tile1, shifted(128), modv, pl.BlockSpec((1, d), lambda i: (0, 0)), modv, modv],
        out_specs=out_specs,
        out_shape=out_shape,
        compiler_params=_params(1),
        name="moe_combine_norm",
    )(h, y_pair, y_pair, route, gate, norm_w.reshape(1, d), shift, scale)


def _rope_table(seq):
    rows = seq // GRID_W
    row = jnp.repeat(jnp.arange(rows, dtype=F32), GRID_W)
    col = jnp.tile(jnp.arange(GRID_W, dtype=F32), rows)
    n_freq = HEAD_DIM // 4
    inv_freq = ROPE_BASE ** (-jnp.arange(n_freq, dtype=F32) / n_freq)
    ang = jnp.concatenate([row[:, None] * inv_freq, col[:, None] * inv_freq], axis=-1)
    cos, sin = jnp.cos(ang), jnp.sin(ang)
    return jnp.concatenate([cos, cos, -sin, sin], axis=-1)


def kernel(x, c, ctx, c_ctx, ada_w, ada_b, norm1_w, norm2_w, ev_w_in, ev_sink, ev_conv_w, ev_w_out, ffn_w_gate, ffn_w_up, ffn_w_down, od_w_in, od_gate_b, od_mnorm_w, od_dw_w, od_dw_b, od_ln_w, od_ln_b, od_w_out, moe_router_w, moe_router_b, moe_w_gate, moe_w_up, moe_w_down, final_w):
    batch, seq, d = x.shape
    ctx_len = ctx.shape[1]
    depth = ada_w.shape[0]
    assert batch * ctx_len == ROW_TILE and seq % ROW_TILE == 0 and ctx_len % CONV_TILE == 0
    assert 1 + batch <= 8
    tps = seq // ROW_TILE
    n_ctx_rows = batch * ctx_len

    cs = _rope_table(seq)
    cvec = jnp.zeros((8, d), F32).at[0].set(c_ctx).at[1:1 + batch].set(c)
    mod_table = _ada_table(cvec, ada_w, ada_b)

    def mods_of(layer):
        mods = mod_table[layer, :1 + batch].reshape(1 + batch, 6, 1, d)
        return [mods[:, i] for i in range(6)]

    zero_mod = jnp.zeros((1 + batch, 1, d), F32)
    ctx_tiles = n_ctx_rows // ROW_TILE
    ev_w_out_b = ev_w_out.astype(BF16)
    od_w_out_b = od_w_out.astype(BF16)

    def take(rows, idx):
        return rows.at[idx].get(mode="promise_in_bounds")

    out = None
    mod = mods_of(0)
    h, a1 = _embed_norm(ctx.reshape(n_ctx_rows, d), x.reshape(batch * seq, d), norm1_w[0], mod[0], mod[1], tps)
    for layer in range(depth):
        j = layer // 2
        last = layer == depth - 1
        next_mod = None if last else mods_of(layer + 1)
        if layer % 2 == 0:
            z = _matmul(a1, ev_w_in, j, ev_w_in.shape[2], tn=1536, out_dtype=BF16)
            o_a = _window_attention(z, cs, ev_sink[j], batch, ctx_len, seq)
            o_b = _short_conv(z, ev_conv_w, j, batch, ctx_len, seq)
            h, a2 = _out_proj_norm(o_a, o_b, ev_w_out_b, j, h, mod[2], norm2_w[layer], mod[3], mod[4], tps)
            g = _ffn_up(a2, ffn_w_gate, ffn_w_up, j)
            h = _ffn_down_residual(g, ffn_w_down, j, h, mod[5], tps)
            if last:
                out = _norm_modulate(h, final_w, zero_mod, zero_mod, tps, out_dtype=F32,
                                     row_offset_tiles=ctx_tiles, n_rows=batch * seq)
            else:
                a1 = _norm_modulate(h, norm1_w[layer + 1], next_mod[0], next_mod[1], tps)
        else:
            qkv_w = M_HEADS * (2 * M_QK + M_V)
            n_gate = 4 * M_HEADS
            zq = _matmul(a1, od_w_in, j, qkv_w, tn=1024)
            w_gate_cols = jnp.pad(od_w_in[j, :, qkv_w:qkv_w + n_gate], ((0, 0), (0, 128 - n_gate)))[None]
            gates = _matmul(a1, w_gate_cols, 0, 128, tn=128)[:, :n_gate]
            w_rest = od_w_in[j, :, qkv_w + n_gate:][None]
            z_rest = _matmul(a1, w_rest, 0, w_rest.shape[2], tn=1536)
            h_f, h_b = _mlstm(zq, gates, od_gate_b[j], batch, ctx_len, seq)
            u_out = _conformer_conv(z_rest, 1, 2, od_dw_w, od_dw_b, od_ln_w, od_ln_b, j, batch, ctx_len, seq)
            h, a2, route = _out_proj_norm((h_f, h_b, z_rest, od_mnorm_w[j]), u_out, od_w_out_b, j, h, mod[2],
                                          norm2_w[layer], mod[3], mod[4], tps,
                                          router=(moe_router_w[j], moe_router_b[j]))
            row0 = n_ctx_rows if last else 0
            src_tok, block_expert, n_valid, pos = _routing_plan(route[row0:])
            x_sorted = jnp.take(a2, src_tok + row0, axis=0, mode="clip")
            y = _moe_experts(x_sorted, block_expert, n_valid, moe_w_gate, moe_w_up, moe_w_down, j)
            y_pair = take(y, pos.T.reshape(-1))
            if last:
                out = _combine_norm(h, y_pair, route, mod[5], final_w, zero_mod, zero_mod, tps, ctx_tiles,
                                    emit_h=False, a_dtype=F32)[0]
            else:
                h, a1 = _combine_norm(h, y_pair, route, mod[5], norm1_w[layer + 1], next_mod[0], next_mod[1], tps,
                                      0, emit_h=True, a_dtype=BF16)
        mod = next_mod
    return out.reshape(batch, seq, d)
```

```python
import functools

import jax
import jax.numpy as jnp
from jax import lax
from jax.experimental import pallas as pl
from jax.experimental.pallas import tpu as pltpu

F32 = jnp.float32
BF16 = jnp.bfloat16

NORM_EPS = 1e-6
ROPE_BASE = 10000.0
GRID_W = 64
HEAD_DIM = 128
Q_HEADS = 8
KV_HEADS = 2
ATT_BLOCK = 128
M_HEADS = 4
M_QK = 128
M_V = 256
M_CHUNK = 128
CONV_SHORT = 3
CONV_LONG = 31
N_EXPERTS = 8
TOP_K = 2

ROW_TILE = 512
CONV_TILE = 256
HALO = 16
NEG = -1e30
V7X_VMEM_LIMIT = 56 * 1024 * 1024


def _params(n_axes):
    return pltpu.CompilerParams(dimension_semantics=("arbitrary",) * n_axes, vmem_limit_bytes=V7X_VMEM_LIMIT)


def _sigmoid(x):
    return 1.0 / (1.0 + jnp.exp(-x))


def _silu(x):
    return x * _sigmoid(x)


def _ada_kernel(c_ref, w_ref, b_ref, o_ref):
    s = _silu(c_ref[...]).astype(BF16)
    o_ref[...] = jnp.dot(s, w_ref[...].astype(BF16), preferred_element_type=F32) + b_ref[...]


def _ada_table(cvec, ada_w, ada_b, tn=1024):
    depth, d, n = ada_w.shape
    rows = cvec.shape[0]
    return pl.pallas_call(
        _ada_kernel,
        grid=(depth, n // tn),
        in_specs=[pl.BlockSpec((rows, d), lambda l, j: (0, 0)),
                  pl.BlockSpec((None, d, tn), lambda l, j: (l, 0, j)),
                  pl.BlockSpec((None, 1, tn), lambda l, j: (l, 0, j))],
        out_specs=pl.BlockSpec((None, rows, tn), lambda l, j: (l, 0, j)),
        out_shape=jax.ShapeDtypeStruct((depth, rows, n), F32),
        compiler_params=_params(2),
        name="ada_table",
    )(cvec, ada_w, ada_b.reshape(depth, 1, n))


def _normmod(x, w, shift, scale):
    ms = jnp.mean(x * x, axis=-1, keepdims=True)
    y = x * lax.rsqrt(ms + NORM_EPS) * w
    return y * (1.0 + scale) + shift


def _normmod_kernel(h_ref, w_ref, sh_ref, sc_ref, o_ref):
    o_ref[...] = _normmod(h_ref[...], w_ref[...], sh_ref[...], sc_ref[...]).astype(o_ref.dtype)


def _embed_kernel(ctx_ref, x_ref, w_ref, sh_ref, sc_ref, h_ref, a_ref):
    rows = jnp.where(pl.program_id(0) == 0, ctx_ref[...], x_ref[...])
    h_ref[...] = rows
    a_ref[...] = _normmod(rows, w_ref[...], sh_ref[...], sc_ref[...]).astype(a_ref.dtype)


def _embed_norm(ctx_rows, x_rows, w, shift, scale, tiles_per_seq):
    n_ctx, d = ctx_rows.shape
    assert n_ctx == ROW_TILE
    t = n_ctx + x_rows.shape[0]
    seg = _seg_index(tiles_per_seq)
    tile = pl.BlockSpec((ROW_TILE, d), lambda i: (i, 0))
    modv = pl.BlockSpec((None, 1, d), lambda i: (seg(i), 0, 0))
    return pl.pallas_call(
        _embed_kernel,
        grid=(t // ROW_TILE,),
        in_specs=[pl.BlockSpec((ROW_TILE, d), lambda i: (0, 0)),
                  pl.BlockSpec((ROW_TILE, d), lambda i: (jnp.maximum(i - 1, 0), 0)),
                  pl.BlockSpec((1, d), lambda i: (0, 0)), modv, modv],
        out_specs=[tile, tile],
        out_shape=[jax.ShapeDtypeStruct((t, d), F32), jax.ShapeDtypeStruct((t, d), BF16)],
        compiler_params=_params(1),
        name="embed_norm",
    )(ctx_rows, x_rows, w.reshape(1, d), shift, scale)


def _top2_route(a, rwh_ref, rwl_ref, rb_ref):
    a_hi = a.astype(BF16)
    a_lo = (a - a_hi.astype(F32)).astype(BF16)
    w_hi = rwh_ref[...]
    logits = (jnp.dot(a_hi, w_hi, preferred_element_type=F32)
              + (jnp.dot(a_lo, w_hi, preferred_element_type=F32)
                 + jnp.dot(a_hi, rwl_ref[...], preferred_element_type=F32))) + rb_ref[...]
    lane = lax.broadcasted_iota(jnp.int32, logits.shape, 1)
    logits = jnp.where(lane < N_EXPERTS, logits, NEG)
    big = jnp.int32(1 << 20)
    m1 = jnp.max(logits, axis=-1, keepdims=True)
    i1 = jnp.min(jnp.where(logits == m1, lane, big), axis=-1, keepdims=True)
    rest = jnp.where(lane == i1, NEG, logits)
    m2 = jnp.max(rest, axis=-1, keepdims=True)
    i2 = jnp.min(jnp.where(rest == m2, lane, big), axis=-1, keepdims=True)
    e2 = jnp.exp(m2 - m1)
    w1 = 1.0 / (1.0 + e2)
    w2 = e2 / (1.0 + e2)
    r = jnp.where(lane == 0, i1.astype(F32), 0.0)
    r = jnp.where(lane == 1, i2.astype(F32), r)
    r = jnp.where(lane == 2, w1, r)
    r = jnp.where(lane == 3, w2, r)
    return r


def _seg_index(tiles_per_seq):
    return lambda i: (i + tiles_per_seq - 1) // tiles_per_seq


def _norm_modulate(h, w, shift, scale, tiles_per_seq, out_dtype=BF16, row_offset_tiles=0, n_rows=None):
    t, d = h.shape
    n_rows = t if n_rows is None else n_rows
    seg = _seg_index(tiles_per_seq)
    off = row_offset_tiles
    return pl.pallas_call(
        _normmod_kernel,
        grid=(n_rows // ROW_TILE,),
        in_specs=[pl.BlockSpec((ROW_TILE, d), lambda i: (i + off, 0)),
                  pl.BlockSpec((1, d), lambda i: (0, 0)),
                  pl.BlockSpec((None, 1, d), lambda i: (seg(i + off), 0, 0)),
                  pl.BlockSpec((None, 1, d), lambda i: (seg(i + off), 0, 0))],
        out_specs=pl.BlockSpec((ROW_TILE, d), lambda i: (i, 0)),
        out_shape=jax.ShapeDtypeStruct((n_rows, d), out_dtype),
        compiler_params=_params(1),
        name="norm_modulate",
    )(h, w.reshape(1, d), shift, scale)


def _mm_kernel(x_ref, w_ref, o_ref, wb_ref):
    @pl.when(pl.program_id(1) == 0)
    def _():
        wb_ref[...] = w_ref[...].astype(BF16)

    o_ref[...] = jnp.dot(x_ref[...], wb_ref[...], preferred_element_type=F32).astype(o_ref.dtype)


def _matmul(x, w3, layer, n_cols, tn=512, out_dtype=F32):
    t, k = x.shape
    return pl.pallas_call(
        _mm_kernel,
        grid=(n_cols // tn, t // ROW_TILE),
        in_specs=[pl.BlockSpec((ROW_TILE, k), lambda j, i: (i, 0)),
                  pl.BlockSpec((None, k, tn), lambda j, i: (layer, 0, j))],
        out_specs=pl.BlockSpec((ROW_TILE, tn), lambda j, i: (i, j)),
        out_shape=jax.ShapeDtypeStruct((t, n_cols), out_dtype),
        scratch_shapes=[pltpu.VMEM((k, tn), BF16)],
        compiler_params=_params(2),
        name="matmul",
    )(x, w3)


def _mlstm_head_out_rows(hf_ref, hb_ref, op_ref, mw_ref):
    parts = []
    for hd in range(M_HEADS):
        sl = slice(hd * M_V, (hd + 1) * M_V)
        x = hf_ref[:, sl] + hb_ref[:, sl]
        mu = jnp.mean(x, axis=-1, keepdims=True)
        xc = x - mu
        var = jnp.mean(xc * xc, axis=-1, keepdims=True)
        y = xc * lax.rsqrt(var + NORM_EPS) * mw_ref[:, sl]
        parts.append((y * _sigmoid(op_ref[:, sl])).astype(BF16))
    return jnp.concatenate(parts, axis=1)


def _out_proj_kernel(*refs, route, head_out):
    refs = list(refs)
    if head_out:
        x1 = _mlstm_head_out_rows(*refs[:4])
        refs = refs[4:]
    else:
        x1 = refs.pop(0)[...]
    if route:
        (x2_ref, w1_ref, w2_ref, h_ref, g_ref, nw_ref, sh_ref, sc_ref, rwh_ref, rwl_ref, rb_ref,
         ho_ref, a_ref, r_ref) = refs
    else:
        x2_ref, w1_ref, w2_ref, h_ref, g_ref, nw_ref, sh_ref, sc_ref, ho_ref, a_ref = refs
    y = jnp.dot(x1, w1_ref[...], preferred_element_type=F32)
    y = y + jnp.dot(x2_ref[...], w2_ref[...], preferred_element_type=F32)
    hn = h_ref[...] + g_ref[...] * y
    ho_ref[...] = hn
    a = _normmod(hn, nw_ref[...], sh_ref[...], sc_ref[...])
    a_ref[...] = a.astype(a_ref.dtype)
    if route:
        r_ref[...] = _top2_route(a, rwh_ref, rwl_ref, rb_ref)


def _out_proj_norm(x1, x2, w3b, layer, h, gate, norm_w, shift, scale, tiles_per_seq, router=None):
    t, k2 = x2.shape
    k1 = k2
    d = w3b.shape[2]
    seg = _seg_index(tiles_per_seq)
    row = lambda w: pl.BlockSpec((ROW_TILE, w), lambda i: (i, 0))
    modv = pl.BlockSpec((None, 1, d), lambda i: (seg(i), 0, 0))
    head_out = isinstance(x1, tuple)
    if head_out:
        h_f, h_b, z_rest, mnorm_w = x1
        assert h_f.shape == (t, k1)
        x1_specs = [row(k1), row(k1), row(k1), pl.BlockSpec((1, k1), lambda i: (0, 0))]
        x1_args = [h_f, h_b, z_rest, mnorm_w.reshape(1, k1)]
    else:
        assert x1.shape == (t, k1)
        x1_specs, x1_args = [row(k1)], [x1]
    in_specs = x1_specs + [
        row(k2),
        pl.BlockSpec((None, k1, d), lambda i: (layer, 0, 0), pipeline_mode=pl.Buffered(1)),
        pl.BlockSpec((None, k2, d), lambda i: (layer, 1, 0), pipeline_mode=pl.Buffered(1)),
        row(d), modv, pl.BlockSpec((1, d), lambda i: (0, 0)), modv, modv]
    args = x1_args + [x2, w3b, w3b, h, gate, norm_w.reshape(1, d), shift, scale]
    out_specs = [row(d), row(d)]
    out_shape = [jax.ShapeDtypeStruct((t, d), F32), jax.ShapeDtypeStruct((t, d), BF16)]
    if router is not None:
        router_w, router_b = router
        rw = jnp.pad(router_w.astype(F32), ((0, 0), (0, 128 - N_EXPERTS)))
        rw_hi = rw.astype(BF16)
        rw_lo = (rw - rw_hi.astype(F32)).astype(BF16)
        in_specs += [pl.BlockSpec((d, 128), lambda i: (0, 0)), pl.BlockSpec((d, 128), lambda i: (0, 0)),
                     pl.BlockSpec((1, 128), lambda i: (0, 0))]
        args += [rw_hi, rw_lo, jnp.pad(router_b, (0, 128 - N_EXPERTS)).reshape(1, 128)]
        out_specs.append(row(128))
        out_shape.append(jax.ShapeDtypeStruct((t, 128), F32))
    return pl.pallas_call(
        functools.partial(_out_proj_kernel, route=router is not None, head_out=head_out),
        grid=(t // ROW_TILE,),
        in_specs=in_specs,
        out_specs=out_specs,
        out_shape=out_shape,
        compiler_params=_params(1),
        name="out_proj_norm",
    )(*args)


def _ffn_up_kernel(x_ref, wg_ref, wu_ref, o_ref, wgb_ref, wub_ref):
    @pl.when(pl.program_id(1) == 0)
    def _():
        wgb_ref[...] = wg_ref[...].astype(BF16)
        wub_ref[...] = wu_ref[...].astype(BF16)

    x = x_ref[...]
    g = jnp.dot(x, wgb_ref[...], preferred_element_type=F32)
    u = jnp.dot(x, wub_ref[...], preferred_element_type=F32)
    o_ref[...] = (_silu(g) * u).astype(o_ref.dtype)


def _ffn_up(x, w_gate, w_up, layer, tf=512):
    t, k = x.shape
    f = w_gate.shape[2]
    tm = 3 * ROW_TILE if t % (3 * ROW_TILE) == 0 else ROW_TILE
    return pl.pallas_call(
        _ffn_up_kernel,
        grid=(f // tf, t // tm),
        in_specs=[pl.BlockSpec((tm, k), lambda j, i: (i, 0)),
                  pl.BlockSpec((None, k, tf), lambda j, i: (layer, 0, j)),
                  pl.BlockSpec((None, k, tf), lambda j, i: (layer, 0, j))],
        out_specs=pl.BlockSpec((tm, tf), lambda j, i: (i, j)),
        out_shape=jax.ShapeDtypeStruct((t, f), BF16),
        scratch_shapes=[pltpu.VMEM((k, tf), BF16), pltpu.VMEM((k, tf), BF16)],
        compiler_params=_params(2),
        name="ffn_up",
    )(x, w_gate, w_up)


def _mm_resid_kernel(x_ref, w_ref, h_ref, g_ref, o_ref, wb_ref):
    @pl.when(pl.program_id(1) == 0)
    def _():
        wb_ref[...] = w_ref[...].astype(BF16)

    y = jnp.dot(x_ref[...], wb_ref[...], preferred_element_type=F32)
    o_ref[...] = h_ref[...] + g_ref[...] * y


def _ffn_down_residual(x, w3, layer, h, gate, tiles_per_seq, tn=512):
    t, k = x.shape
    n = w3.shape[2]
    seg = _seg_index(tiles_per_seq)
    return pl.pallas_call(
        _mm_resid_kernel,
        grid=(n // tn, t // ROW_TILE),
        in_specs=[pl.BlockSpec((ROW_TILE, k), lambda j, i: (i, 0)),
                  pl.BlockSpec((None, k, tn), lambda j, i: (layer, 0, j)),
                  pl.BlockSpec((ROW_TILE, tn), lambda j, i: (i, j)),
                  pl.BlockSpec((None, 1, tn), lambda j, i: (seg(i), 0, j))],
        out_specs=pl.BlockSpec((ROW_TILE, tn), lambda j, i: (i, j)),
        out_shape=jax.ShapeDtypeStruct((t, n), F32),
        scratch_shapes=[pltpu.VMEM((k, tn), BF16)],
        compiler_params=_params(2),
        name="ffn_down_residual",
    )(x, w3, h, gate)


def _rope(x, cs):
    x = x.astype(F32)
    return x * cs[:, :HEAD_DIM] + pltpu.roll(x, HEAD_DIM // 2, axis=1) * cs[:, HEAD_DIM:]


def _attend(q_all, k_parts, v_parts, sink_ref, o_ref, mask_fn):
    group = Q_HEADS // KV_HEADS
    scale = HEAD_DIM ** -0.5
    for g in range(KV_HEADS):
        qg = jnp.concatenate([q_all[:, (g * group + r) * HEAD_DIM:(g * group + r + 1) * HEAD_DIM]
                              for r in range(group)], axis=0).astype(BF16)
        kg = jnp.concatenate([kp[:, g * HEAD_DIM:(g + 1) * HEAD_DIM] for kp in k_parts], axis=0).astype(BF16)
        vg = jnp.concatenate([vp[:, g * HEAD_DIM:(g + 1) * HEAD_DIM] for vp in v_parts], axis=0).astype(BF16)
        s = lax.dot_general(qg, kg, (((1,), (1,)), ((), ())), preferred_element_type=F32) * scale
        if mask_fn is not None:
            s = jnp.where(mask_fn(s.shape), s, NEG)
        sink = sink_ref[g]
        m = jnp.maximum(jnp.max(s, axis=-1, keepdims=True), sink)
        p = jnp.exp(s - m)
        denom = jnp.sum(p, axis=-1, keepdims=True) + jnp.exp(sink - m)
        o = jnp.dot(p.astype(BF16), vg, preferred_element_type=F32) / denom
        for r in range(group):
            hq = g * group + r
            o_ref[:, hq * HEAD_DIM:(hq + 1) * HEAD_DIM] = o[r * ATT_BLOCK:(r + 1) * ATT_BLOCK].astype(o_ref.dtype)


def _attn_kernel(q_ref, kp_ref, kc_ref, kn_ref, kx_ref, vp_ref, vc_ref, vn_ref, vx_ref,
                 csp_ref, csc_ref, csn_ref, sink_ref, o_ref, *, n_blocks, ctx_blocks):
    step = pl.program_id(1)

    @pl.when(step < ctx_blocks)
    def _():
        _attend(q_ref[...], [kx_ref[...]], [vx_ref[...]], sink_ref, o_ref, None)

    @pl.when(step >= ctx_blocks)
    def _():
        _win_attn_body(q_ref, kp_ref, kc_ref, kn_ref, kx_ref, vp_ref, vc_ref, vn_ref, vx_ref,
                       csp_ref, csc_ref, csn_ref, sink_ref, o_ref, step - ctx_blocks, n_blocks)


def _win_attn_body(q_ref, kp_ref, kc_ref, kn_ref, kx_ref, vp_ref, vc_ref, vn_ref, vx_ref,
                   csp_ref, csc_ref, csn_ref, sink_ref, o_ref, n, n_blocks):
    csc = csc_ref[...]
    q_all = jnp.concatenate([_rope(q_ref[:, h * HEAD_DIM:(h + 1) * HEAD_DIM], csc) for h in range(Q_HEADS)], axis=1)

    def rope_kv(k_ref, cs):
        return jnp.concatenate([_rope(k_ref[:, g * HEAD_DIM:(g + 1) * HEAD_DIM], cs) for g in range(KV_HEADS)], axis=1)

    k_parts = [rope_kv(kp_ref, csp_ref[...]), rope_kv(kc_ref, csc), rope_kv(kn_ref, csn_ref[...]), kx_ref[...]]
    v_parts = [vp_ref[...], vc_ref[...], vn_ref[...], vx_ref[...]]

    def mask_fn(shape):
        row = lax.broadcasted_iota(jnp.int32, shape, 0) & (ATT_BLOCK - 1)
        col = lax.broadcasted_iota(jnp.int32, shape, 1)
        band = (col >= row) & (col <= row + 2 * ATT_BLOCK)
        ok_prev = (col >= ATT_BLOCK) | (n > 0)
        ok_next = (col < 2 * ATT_BLOCK) | (n < n_blocks - 1)
        return (band & ok_prev & ok_next) | (col >= 3 * ATT_BLOCK)

    _attend(q_all, k_parts, v_parts, sink_ref, o_ref, mask_fn)


def _sink_rows(sink):
    group = Q_HEADS // KV_HEADS
    return jnp.repeat(sink.astype(F32).reshape(KV_HEADS, group), ATT_BLOCK, axis=1).reshape(
        KV_HEADS, group * ATT_BLOCK, 1)


def _window_attention(z, cs, sink, batch, ctx_len, seq):
    t = z.shape[0]
    nb = seq // ATT_BLOCK
    cb = ctx_len // ATT_BLOCK
    base = batch * cb
    qw = Q_HEADS * HEAD_DIM
    kw = KV_HEADS * HEAD_DIM
    kcol = qw // kw
    vcol = kcol + 1
    sink_rows = _sink_rows(sink)

    def lat(s):
        return jnp.clip(s - cb, 0, nb - 1)

    def q_block(b, s):
        return jnp.where(s < cb, b * cb + s, base + b * nb + lat(s))

    def kv(shift, col):
        return pl.BlockSpec((ATT_BLOCK, kw), lambda b, s: (base + b * nb + lat(s + shift), col))

    def rot(shift):
        return pl.BlockSpec((ATT_BLOCK, 2 * HEAD_DIM), lambda b, s: (lat(s + shift), 0))

    return pl.pallas_call(
        functools.partial(_attn_kernel, n_blocks=nb, ctx_blocks=cb),
        grid=(batch, cb + nb),
        in_specs=[pl.BlockSpec((ATT_BLOCK, qw), lambda b, s: (q_block(b, s), 0)),
                  kv(-1, kcol), kv(0, kcol), kv(1, kcol),
                  pl.BlockSpec((ctx_len, kw), lambda b, s: (b, kcol)),
                  kv(-1, vcol), kv(0, vcol), kv(1, vcol),
                  pl.BlockSpec((ctx_len, kw), lambda b, s: (b, vcol)),
                  rot(-1), rot(0), rot(1),
                  pl.BlockSpec(sink_rows.shape, lambda b, s: (0, 0, 0))],
        out_specs=pl.BlockSpec((ATT_BLOCK, qw), lambda b, s: (q_block(b, s), 0)),
        out_shape=jax.ShapeDtypeStruct((t, qw), BF16),
        compiler_params=_params(2),
        name="window_attention",
    )(z, z, z, z, z, z, z, z, z, cs, cs, cs, sink_rows)


def _seq_edges(t, batch, ctx_tiles, lat_tiles):
    n_ctx = batch * ctx_tiles
    u = t - n_ctx
    is_ctx = t < n_ctx
    first = jnp.where(is_ctx, lax.rem(t, ctx_tiles) == 0, lax.rem(u, lat_tiles) == 0)
    last = jnp.where(is_ctx, lax.rem(t, ctx_tiles) == ctx_tiles - 1, lax.rem(u, lat_tiles) == lat_tiles - 1)
    return first, last


def _fill_ext(ext_ref, cur, prev, nxt, first, last):
    ext_ref[HALO:HALO + CONV_TILE, :] = cur
    ext_ref[0:HALO, :] = jnp.where(first, 0.0, prev)
    ext_ref[HALO + CONV_TILE:, :] = jnp.where(last, 0.0, nxt)


def _dwconv_rows(ext_ref, w_ref, taps, r0, n_rows):
    pad = taps // 2
    acc = None
    for k in range(taps):
        start = HALO + r0 + k - pad
        term = w_ref[k:k + 1, :] * ext_ref[start:start + n_rows, :]
        acc = term if acc is None else acc + term
    return acc


CONV_ROWS = 32


def _short_conv_kernel(bg_ref, cg_ref, xb_ref, cgp_ref, xbp_ref, cgn_ref, xbn_ref, w_ref, o_ref, ext_ref,
                       *, batch, ctx_tiles, lat_tiles):
    first, last = _seq_edges(pl.program_id(0), batch, ctx_tiles, lat_tiles)
    def prod(a_ref, b_ref):
        return a_ref[...].astype(F32) * b_ref[...].astype(F32)

    _fill_ext(ext_ref, prod(cg_ref, xb_ref), prod(cgp_ref, xbp_ref), prod(cgn_ref, xbn_ref), first, last)
    for r0 in range(0, CONV_TILE, CONV_ROWS):
        conv = _dwconv_rows(ext_ref, w_ref, CONV_SHORT, r0, CONV_ROWS)
        o_ref[r0:r0 + CONV_ROWS, :] = (bg_ref[r0:r0 + CONV_ROWS, :].astype(F32) * conv).astype(o_ref.dtype)


def _halo_specs(width, col, halo_per_tile, n_halo_blocks):
    prev = pl.BlockSpec((HALO, width), lambda t, c: (jnp.maximum(t * halo_per_tile - 1, 0), col(c)))
    nxt = pl.BlockSpec((HALO, width), lambda t, c: (jnp.minimum((t + 1) * halo_per_tile, n_halo_blocks - 1), col(c)))
    return prev, nxt


def _short_conv(z, conv_w3, layer, batch, ctx_len, seq):
    t = z.shape[0]
    cw = 512
    n_c = 1024 // cw
    bg0, cg0, xb0 = 1536 // cw, 2560 // cw, 3584 // cw
    hpt = CONV_TILE // HALO
    nh = t // HALO
    cgp, cgn = _halo_specs(cw, lambda c: cg0 + c, hpt, nh)
    xbp, xbn = _halo_specs(cw, lambda c: xb0 + c, hpt, nh)
    kern = functools.partial(_short_conv_kernel, batch=batch, ctx_tiles=ctx_len // CONV_TILE,
                             lat_tiles=seq // CONV_TILE)
    return pl.pallas_call(
        kern,
        grid=(t // CONV_TILE, n_c),
        in_specs=[pl.BlockSpec((CONV_TILE, cw), lambda t_, c: (t_, bg0 + c)),
                  pl.BlockSpec((CONV_TILE, cw), lambda t_, c: (t_, cg0 + c)),
                  pl.BlockSpec((CONV_TILE, cw), lambda t_, c: (t_, xb0 + c)),
                  cgp, xbp, cgn, xbn,
                  pl.BlockSpec((None, CONV_SHORT, cw), lambda t_, c: (layer, 0, c))],
        out_specs=pl.BlockSpec((CONV_TILE, cw), lambda t_, c: (t_, c)),
        out_shape=jax.ShapeDtypeStruct((t, 1024), BF16),
        scratch_shapes=[pltpu.VMEM((CONV_TILE + 2 * HALO, cw), F32)],
        compiler_params=_params(2),
        name="short_conv",
    )(z, z, z, z, z, z, z, conv_w3)


SUBLANES = 8
SHIFT_ROWS = CONV_TILE + 2 * HALO - SUBLANES


def _dwconv_rows_aligned(ext_ref, sh_ref, w8_ref, taps, r0, n_rows):
    pad = taps // 2
    width = ext_ref.shape[1]
    acc = None
    for k in range(taps):
        start = HALO + r0 + k - pad
        b = start % SUBLANES
        a8 = start - b
        src = ext_ref[a8:a8 + n_rows, :] if b == 0 else sh_ref[b - 1, a8:a8 + n_rows, :]
        term = (src.reshape(n_rows // SUBLANES, SUBLANES, width) * w8_ref[k][None]).reshape(n_rows, width)
        acc = term if acc is None else acc + term
    return acc


def _conformer_kernel(a_ref, g_ref, ap_ref, gp_ref, an_ref, gn_ref, w8_ref, b_ref, lw_ref, lb_ref, o_ref,
                      ext_ref, sh_ref, *, batch, ctx_tiles, lat_tiles):
    first, last = _seq_edges(pl.program_id(0), batch, ctx_tiles, lat_tiles)
    _fill_ext(ext_ref, a_ref[...] * _sigmoid(g_ref[...]), ap_ref[...] * _sigmoid(gp_ref[...]),
              an_ref[...] * _sigmoid(gn_ref[...]), first, last)
    for b in range(1, SUBLANES):
        sh_ref[b - 1] = ext_ref[b:b + SHIFT_ROWS, :]
    for r0 in range(0, CONV_TILE, CONV_ROWS):
        u = _dwconv_rows_aligned(ext_ref, sh_ref, w8_ref, CONV_LONG, r0, CONV_ROWS) + b_ref[...]
        mu = jnp.mean(u, axis=-1, keepdims=True)
        uc = u - mu
        var = jnp.mean(uc * uc, axis=-1, keepdims=True)
        y = uc * lax.rsqrt(var + NORM_EPS) * lw_ref[...] + lb_ref[...]
        o_ref[r0:r0 + CONV_ROWS, :] = _silu(y).astype(o_ref.dtype)


def _conformer_conv(z, a_col, g_col, dw_w3, dw_b, ln_w, ln_b, layer, batch, ctx_len, seq):
    t = z.shape[0]
    cw = 1024
    hpt = CONV_TILE // HALO
    nh = t // HALO
    ap, an = _halo_specs(cw, lambda c: a_col, hpt, nh)
    gp, gn = _halo_specs(cw, lambda c: g_col, hpt, nh)
    kern = functools.partial(_conformer_kernel, batch=batch, ctx_tiles=ctx_len // CONV_TILE,
                             lat_tiles=seq // CONV_TILE)
    vec = pl.BlockSpec((None, 1, cw), lambda t_, c: (layer, 0, 0))
    n_layers = dw_b.shape[0]
    return pl.pallas_call(
        kern,
        grid=(t // CONV_TILE, 1),
        in_specs=[pl.BlockSpec((CONV_TILE, cw), lambda t_, c: (t_, a_col)),
                  pl.BlockSpec((CONV_TILE, cw), lambda t_, c: (t_, g_col)),
                  ap, gp, an, gn,
                  pl.BlockSpec((None, CONV_LONG, SUBLANES, cw), lambda t_, c: (layer, 0, 0, 0)),
                  vec, vec, vec],
        out_specs=pl.BlockSpec((CONV_TILE, cw), lambda t_, c: (t_, 0)),
        out_shape=jax.ShapeDtypeStruct((t, cw), BF16),
        scratch_shapes=[pltpu.VMEM((CONV_TILE + 2 * HALO, cw), F32),
                        pltpu.VMEM((SUBLANES - 1, SHIFT_ROWS, cw), F32)],
        compiler_params=_params(2),
        name="conformer_conv",
    )(z, z, z, z, z, z, jnp.broadcast_to(dw_w3[:, :, None, :], (n_layers, CONV_LONG, SUBLANES, cw)),
      dw_b.reshape(n_layers, 1, cw), ln_w.reshape(n_layers, 1, cw), ln_b.reshape(n_layers, 1, cw))


def _log_sigmoid(x):
    return jnp.minimum(x, 0.0) - jnp.log(1.0 + jnp.exp(-jnp.abs(x)))


def _mlstm_kernel(qf_ref, kf_ref, vf_ref, gcf_ref, grf_ref, qb_ref, kb_ref, vb_ref, gcb_ref, grb_ref,
                  bc_ref, br_ref, of_ref, ob_ref, *state_refs):
    n_chain = 2 * M_HEADS
    cta_ref, m_ref = state_refs[:n_chain], state_refs[n_chain:]

    @pl.when(pl.program_id(1) == 0)
    def _():
        for ref in state_refs:
            ref[...] = jnp.zeros_like(ref)

    ln = M_CHUNK
    row = lax.broadcasted_iota(jnp.int32, (ln, ln), 0)
    col = lax.broadcasted_iota(jnp.int32, (ln, ln), 1)
    lower = col <= row
    upper = col >= row
    _mlstm_direction(qf_ref, kf_ref, vf_ref, gcf_ref[...] + bc_ref[0], grf_ref[...] + br_ref[0], lower, upper,
                     of_ref, cta_ref, m_ref, 0)
    _mlstm_direction(qb_ref, kb_ref, vb_ref, gcb_ref[...] + bc_ref[1], grb_ref[...] + br_ref[1], upper, lower,
                     ob_ref, cta_ref, m_ref, M_HEADS)


def _mlstm_direction(q_ref, k_ref, v_ref, gc, gr, mask, mask_t, o_ref, cta_ref, m_ref, state0):
    ln = M_CHUNK
    one_col = lax.broadcasted_iota(jnp.int32, (ln, 128), 1) == 0
    ones_blk = jnp.where(one_col, 1.0, 0.0).astype(BF16)
    for h in range(M_HEADS):
        st = state0 + h
        ig_c = gc[:, h:h + 1]
        lf_c = _log_sigmoid(gc[:, M_HEADS + h:M_HEADS + h + 1])
        ig_r = gr[h:h + 1, :]
        lf_r = _log_sigmoid(gr[M_HEADS + h:M_HEADS + h + 1, :])
        qb = (q_ref[:, h * M_QK:(h + 1) * M_QK] * (M_QK ** -0.5)).astype(BF16)
        k = k_ref[:, h * M_QK:(h + 1) * M_QK]
        v = v_ref[:, h * M_V:(h + 1) * M_V]
        cum_c = jnp.sum(jnp.where(mask, lf_r, 0.0), axis=1, keepdims=True)
        cum_r = jnp.sum(jnp.where(mask_t, lf_c, 0.0), axis=0, keepdims=True)
        total = jnp.sum(lf_r, axis=1, keepdims=True)
        m_s = m_ref[st][...]
        g = jnp.where(mask, ig_r - cum_r, NEG)
        mm = jnp.maximum(m_s, jnp.max(g, axis=1, keepdims=True))
        w_inter = jnp.exp(m_s - mm)
        s = lax.dot_general(qb, k.astype(BF16), (((1,), (1,)), ((), ())), preferred_element_type=F32) * jnp.exp(g - mm)
        cta = cta_ref[st][...]
        va = jnp.concatenate([v.astype(BF16), ones_blk], axis=1)
        inter = jnp.dot(qb, cta.astype(BF16), preferred_element_type=F32)
        intra = jnp.dot(s.astype(BF16), va, preferred_element_type=F32)
        num = w_inter * inter[:, :M_V] + intra[:, :M_V]
        den = w_inter * inter[:, M_V:M_V + 1] + intra[:, M_V:M_V + 1]
        o_ref[:, h * M_V:(h + 1) * M_V] = num / jnp.maximum(jnp.abs(den), jnp.exp(-(cum_c + mm)))
        gl = total - cum_c + ig_c
        m_new = jnp.maximum(total + m_s, jnp.max(gl, axis=0, keepdims=True))
        wg = jnp.exp(gl - m_new)
        decay = jnp.exp(total + m_s - m_new)
        wva = jnp.concatenate([(wg * v).astype(BF16), jnp.where(one_col, wg, 0.0).astype(BF16)], axis=1)
        cta_ref[st][...] = decay * cta + jnp.dot(k.astype(BF16).T, wva, preferred_element_type=F32)
        m_ref[st][...] = m_new


def _mlstm(zq, gates, gate_b, batch, ctx_len, seq):
    t = zq.shape[0]
    cb = ctx_len // M_CHUNK
    nc = seq // M_CHUNK
    base = batch * cb
    steps = cb + nc
    half = 2 * M_HEADS
    g_dir = gates.reshape(t, 2, half).transpose(1, 0, 2)
    g_dir_t = g_dir.transpose(0, 2, 1)
    b_dir = gate_b.astype(F32).reshape(2, 1, half)
    b_dir_t = b_dir.transpose(0, 2, 1)

    def rb_f(b, i):
        return jnp.where(i < cb, b * cb + i, base + b * nc + (i - cb))

    def rb_b(b, i):
        return jnp.where(i < cb, b * cb + (cb - 1 - i), base + b * nc + (nc - 1 - (i - cb)))

    qk_w = M_HEADS * M_QK
    v_w = M_HEADS * M_V

    def chunk_specs(d, rb):
        return [pl.BlockSpec((M_CHUNK, qk_w), lambda b, i: (rb(b, i), 0)),
                pl.BlockSpec((M_CHUNK, qk_w), lambda b, i: (rb(b, i), 1)),
                pl.BlockSpec((M_CHUNK, v_w), lambda b, i: (rb(b, i), 1)),
                pl.BlockSpec((None, M_CHUNK, half), lambda b, i: (d, rb(b, i), 0)),
                pl.BlockSpec((None, half, M_CHUNK), lambda b, i: (d, 0, rb(b, i)))]

    return pl.pallas_call(
        _mlstm_kernel,
        grid=(batch, steps),
        in_specs=chunk_specs(0, rb_f) + chunk_specs(1, rb_b) + [
            pl.BlockSpec((2, 1, half), lambda b, i: (0, 0, 0)),
            pl.BlockSpec((2, half, 1), lambda b, i: (0, 0, 0))],
        out_specs=[pl.BlockSpec((M_CHUNK, v_w), lambda b, i: (rb_f(b, i), 0)),
                   pl.BlockSpec((M_CHUNK, v_w), lambda b, i: (rb_b(b, i), 0))],
        out_shape=[jax.ShapeDtypeStruct((t, v_w), F32), jax.ShapeDtypeStruct((t, v_w), F32)],
        scratch_shapes=([pltpu.VMEM((M_QK, M_V + 128), F32)] * (2 * M_HEADS)
                        + [pltpu.VMEM((1, 1), F32)] * (2 * M_HEADS)),
        compiler_params=_params(2),
        name="mlstm",
    )(zq, zq, zq, g_dir, g_dir_t, zq, zq, zq, g_dir, g_dir_t, b_dir, b_dir_t)


def _fresh_weights(be_ref, rb):
    return (rb == 0) | (be_ref[rb] != be_ref[jnp.maximum(rb - 1, 0)])


HALF_TILE = ROW_TILE // 2


def _block_state(nv_ref, rb):
    valid = rb < nv_ref[0]
    half = nv_ref[1 + rb] != 0
    return valid & jnp.logical_not(half), valid & half, jnp.logical_not(valid)


def _moe_up_kernel(be_ref, nv_ref, x_ref, wg_ref, wu_ref, o_ref, wgb_ref, wub_ref):
    rb = pl.program_id(1)
    full, half, unused = _block_state(nv_ref, rb)

    @pl.when(jnp.logical_not(unused) & _fresh_weights(be_ref, rb))
    def _():
        wgb_ref[...] = wg_ref[...].astype(BF16)
        wub_ref[...] = wu_ref[...].astype(BF16)

    def swiglu(x):
        g = jnp.dot(x, wgb_ref[...], preferred_element_type=F32)
        u = jnp.dot(x, wub_ref[...], preferred_element_type=F32)
        return (_silu(g) * u).astype(o_ref.dtype)

    @pl.when(full)
    def _():
        o_ref[...] = swiglu(x_ref[...])

    @pl.when(half)
    def _():
        o_ref[:HALF_TILE, :] = swiglu(x_ref[:HALF_TILE, :])
        o_ref[HALF_TILE:, :] = jnp.zeros((ROW_TILE - HALF_TILE, o_ref.shape[1]), o_ref.dtype)

    @pl.when(unused)
    def _():
        o_ref[...] = jnp.zeros_like(o_ref)


def _moe_down_kernel(be_ref, nv_ref, x_ref, w_ref, o_ref, wb_ref):
    rb = pl.program_id(1)
    full, half, unused = _block_state(nv_ref, rb)

    @pl.when(jnp.logical_not(unused) & _fresh_weights(be_ref, rb))
    def _():
        wb_ref[...] = w_ref[...].astype(BF16)

    @pl.when(full)
    def _():
        o_ref[...] = jnp.dot(x_ref[...], wb_ref[...], preferred_element_type=F32).astype(o_ref.dtype)

    @pl.when(half)
    def _():
        o_ref[:HALF_TILE, :] = jnp.dot(x_ref[:HALF_TILE, :], wb_ref[...],
                                       preferred_element_type=F32).astype(o_ref.dtype)
        o_ref[HALF_TILE:, :] = jnp.zeros((ROW_TILE - HALF_TILE, o_ref.shape[1]), o_ref.dtype)

    @pl.when(unused)
    def _():
        o_ref[...] = jnp.zeros_like(o_ref)


def _moe_experts(x_sorted, block_expert, n_valid, w_gate, w_up, w_down, layer, tf=512, tn=512):
    p, d = x_sorted.shape
    f = w_gate.shape[3]
    nblk = p // ROW_TILE

    def row(rb, nv):
        return jnp.minimum(rb, nv[0] - 1)

    g = pl.pallas_call(
        _moe_up_kernel,
        grid_spec=pltpu.PrefetchScalarGridSpec(
            num_scalar_prefetch=2,
            grid=(f // tf, nblk),
            in_specs=[pl.BlockSpec((ROW_TILE, d), lambda j, rb, be, nv: (row(rb, nv), 0)),
                      pl.BlockSpec((None, None, d, tf), lambda j, rb, be, nv: (layer, be[rb], 0, j)),
                      pl.BlockSpec((None, None, d, tf), lambda j, rb, be, nv: (layer, be[rb], 0, j))],
            out_specs=pl.BlockSpec((ROW_TILE, tf), lambda j, rb, be, nv: (rb, j)),
            scratch_shapes=[pltpu.VMEM((d, tf), BF16), pltpu.VMEM((d, tf), BF16)]),
        out_shape=jax.ShapeDtypeStruct((p, f), BF16),
        compiler_params=_params(2),
        name="moe_up",
    )(block_expert, n_valid, x_sorted, w_gate, w_up)
    return pl.pallas_call(
        _moe_down_kernel,
        grid_spec=pltpu.PrefetchScalarGridSpec(
            num_scalar_prefetch=2,
            grid=(d // tn, nblk),
            in_specs=[pl.BlockSpec((ROW_TILE, f), lambda j, rb, be, nv: (row(rb, nv), 0)),
                      pl.BlockSpec((None, None, f, tn), lambda j, rb, be, nv: (layer, be[rb], 0, j))],
            out_specs=pl.BlockSpec((ROW_TILE, tn), lambda j, rb, be, nv: (rb, j)),
            scratch_shapes=[pltpu.VMEM((f, tn), BF16)]),
        out_shape=jax.ShapeDtypeStruct((p, d), BF16),
        compiler_params=_params(2),
        name="moe_down",
    )(block_expert, n_valid, g, w_down)


def _routing_plan(route):
    t = route.shape[0]
    tm = ROW_TILE
    n_assign = TOP_K * t
    nblk = (n_assign + N_EXPERTS * (tm - 1)) // tm
    e = route[:, :TOP_K].astype(jnp.int32).reshape(-1)
    onehot = (e[:, None] == jnp.arange(N_EXPERTS, dtype=jnp.int32)[None, :]).astype(jnp.int32)
    csum = jnp.cumsum(onehot, axis=0)
    counts = csum[-1]
    padded = ((counts + tm - 1) // tm) * tm
    ends = jnp.cumsum(padded)
    starts = ends - padded
    dest = jnp.sum((csum - onehot + starts[None, :]) * onehot, axis=1)
    src_tok = (jnp.arange(nblk * tm, dtype=jnp.int32) % t).at[dest].set(
        jnp.arange(n_assign, dtype=jnp.int32) // TOP_K, unique_indices=True, mode="promise_in_bounds")
    n_valid = (ends[-1] // tm).astype(jnp.int32)
    blk_start = jnp.arange(nblk, dtype=jnp.int32) * tm
    be = jnp.sum((ends[None, :] <= blk_start[:, None]).astype(jnp.int32), axis=1)
    be = jnp.minimum(be, N_EXPERTS - 1).astype(jnp.int32)
    group_rows_end = jnp.sum((be[:, None] == jnp.arange(N_EXPERTS, dtype=jnp.int32)[None, :]).astype(jnp.int32)
                             * (starts + counts)[None, :], axis=1)
    half = (group_rows_end - blk_start <= HALF_TILE).astype(jnp.int32)
    block_info = jnp.concatenate([n_valid.reshape(1), half])
    return src_tok, be, block_info, dest.reshape(t, TOP_K)


def _combine_kernel(h_ref, y0_ref, y1_ref, r_ref, g_ref, nw_ref, sh_ref, sc_ref, *out_refs):
    r = r_ref[...]
    moe = r[:, TOP_K:TOP_K + 1] * y0_ref[...].astype(F32) + r[:, TOP_K + 1:TOP_K + 2] * y1_ref[...].astype(F32)
    hn = h_ref[...] + g_ref[...] * moe
    if len(out_refs) == 2:
        out_refs[0][...] = hn
    out_refs[-1][...] = _normmod(hn, nw_ref[...], sh_ref[...], sc_ref[...]).astype(out_refs[-1].dtype)


def _combine_norm(h, y_pair, route, gate, norm_w, shift, scale, tiles_per_seq, row_offset_tiles, emit_h, a_dtype):
    d = h.shape[1]
    n_rows = y_pair.shape[0] // TOP_K
    n_tiles = n_rows // ROW_TILE
    off = row_offset_tiles
    seg = _seg_index(tiles_per_seq)
    shifted = lambda w: pl.BlockSpec((ROW_TILE, w), lambda i: (i + off, 0))
    tile = pl.BlockSpec((ROW_TILE, d), lambda i: (i, 0))
    tile1 = pl.BlockSpec((ROW_TILE, d), lambda i: (i + n_tiles, 0))
    modv = pl.BlockSpec((None, 1, d), lambda i: (seg(i + off), 0, 0))
    out_specs = [tile, tile] if emit_h else [tile]
    out_shape = [jax.ShapeDtypeStruct((n_rows, d), a_dtype)]
    if emit_h:
        out_shape.insert(0, jax.ShapeDtypeStruct((n_rows, d), F32))
    return pl.pallas_call(
        _combine_kernel,
        grid=(n_rows // ROW_TILE,),
        in_specs=[shifted(d), tile, tile1, shifted(128), modv, pl.BlockSpec((1, d), lambda i: (0, 0)), modv, modv],
        out_specs=out_specs,
        out_shape=out_shape,
        compiler_params=_params(1),
        name="moe_combine_norm",
    )(h, y_pair, y_pair, route, gate, norm_w.reshape(1, d), shift, scale)


def _rope_table(seq):
    rows = seq // GRID_W
    row = jnp.repeat(jnp.arange(rows, dtype=F32), GRID_W)
    col = jnp.tile(jnp.arange(GRID_W, dtype=F32), rows)
    n_freq = HEAD_DIM // 4
    inv_freq = ROPE_BASE ** (-jnp.arange(n_freq, dtype=F32) / n_freq)
    ang = jnp.concatenate([row[:, None] * inv_freq, col[:, None] * inv_freq], axis=-1)
    cos, sin = jnp.cos(ang), jnp.sin(ang)
    return jnp.concatenate([cos, cos, -sin, sin], axis=-1)


def kernel(x, c, ctx, c_ctx, ada_w, ada_b, norm1_w, norm2_w, ev_w_in, ev_sink, ev_conv_w, ev_w_out, ffn_w_gate, ffn_w_up, ffn_w_down, od_w_in, od_gate_b, od_mnorm_w, od_dw_w, od_dw_b, od_ln_w, od_ln_b, od_w_out, moe_router_w, moe_router_b, moe_w_gate, moe_w_up, moe_w_down, final_w):
    batch, seq, d = x.shape
    ctx_len = ctx.shape[1]
    depth = ada_w.shape[0]
    assert batch * ctx_len == ROW_TILE and seq % ROW_TILE == 0 and ctx_len % CONV_TILE == 0
    assert 1 + batch <= 8
    tps = seq // ROW_TILE
    n_ctx_rows = batch * ctx_len

    cs = _rope_table(seq)
    cvec = jnp.zeros((8, d), F32).at[0].set(c_ctx).at[1:1 + batch].set(c)
    mod_table = _ada_table(cvec, ada_w, ada_b)

    def mods_of(layer):
        mods = mod_table[layer, :1 + batch].reshape(1 + batch, 6, 1, d)
        return [mods[:, i] for i in range(6)]

    zero_mod = jnp.zeros((1 + batch, 1, d), F32)
    ctx_tiles = n_ctx_rows // ROW_TILE
    ev_w_out_b = ev_w_out.astype(BF16)
    od_w_out_b = od_w_out.astype(BF16)

    def take(rows, idx):
        return rows.at[idx].get(mode="promise_in_bounds")

    out = None
    mod = mods_of(0)
    h, a1 = _embed_norm(ctx.reshape(n_ctx_rows, d), x.reshape(batch * seq, d), norm1_w[0], mod[0], mod[1], tps)
    for layer in range(depth):
        j = layer // 2
        last = layer == depth - 1
        next_mod = None if last else mods_of(layer + 1)
        if layer % 2 == 0:
            z = _matmul(a1, ev_w_in, j, ev_w_in.shape[2], tn=1536, out_dtype=BF16)
            o_a = _window_attention(z, cs, ev_sink[j], batch, ctx_len, seq)
            o_b = _short_conv(z, ev_conv_w, j, batch, ctx_len, seq)
            h, a2 = _out_proj_norm(o_a, o_b, ev_w_out_b, j, h, mod[2], norm2_w[layer], mod[3], mod[4], tps)
            g = _ffn_up(a2, ffn_w_gate, ffn_w_up, j)
            h = _ffn_down_residual(g, ffn_w_down, j, h, mod[5], tps)
            if last:
                out = _norm_modulate(h, final_w, zero_mod, zero_mod, tps, out_dtype=F32,
                                     row_offset_tiles=ctx_tiles, n_rows=batch * seq)
            else:
                a1 = _norm_modulate(h, norm1_w[layer + 1], next_mod[0], next_mod[1], tps)
        else:
            qkv_w = M_HEADS * (2 * M_QK + M_V)
            n_gate = 4 * M_HEADS
            zq = _matmul(a1, od_w_in, j, qkv_w, tn=1024)
            w_gate_cols = jnp.pad(od_w_in[j, :, qkv_w:qkv_w + n_gate], ((0, 0), (0, 128 - n_gate)))[None]
            gates = _matmul(a1, w_gate_cols, 0, 128, tn=128)[:, :n_gate]
            w_rest = od_w_in[j, :, qkv_w + n_gate:][None]
            z_rest = _matmul(a1, w_rest, 0, w_rest.shape[2], tn=1536)
            h_f, h_b = _mlstm(zq, gates, od_gate_b[j], batch, ctx_len, seq)
            u_out = _conformer_conv(z_rest, 1, 2, od_dw_w, od_dw_b, od_ln_w, od_ln_b, j, batch, ctx_len, seq)
            h, a2, route = _out_proj_norm((h_f, h_b, z_rest, od_mnorm_w[j]), u_out, od_w_out_b, j, h, mod[2],
                                          norm2_w[layer], mod[3], mod[4], tps,
                                          router=(moe_router_w[j], moe_router_b[j]))
            row0 = n_ctx_rows if last else 0
            src_tok, block_expert, n_valid, pos = _routing_plan(route[row0:])
            x_sorted = jnp.take(a2, src_tok + row0, axis=0, mode="clip")
            y = _moe_experts(x_sorted, block_expert, n_valid, moe_w_gate, moe_w_up, moe_w_down, j)
            y_pair = take(y, pos.T.reshape(-1))
            if last:
                out = _combine_norm(h, y_pair, route, mod[5], final_w, zero_mod, zero_mod, tps, ctx_tiles,
                                    emit_h=False, a_dtype=F32)[0]
            else:
                h, a1 = _combine_norm(h, y_pair, route, mod[5], norm1_w[layer + 1], next_mod[0], next_mod[1], tps,
                                      0, emit_h=True, a_dtype=BF16)
        mod = next_mod
    return out.reshape(batch, seq, d)
```

```python
import functools

import jax
import jax.numpy as jnp
from jax import lax
from jax.experimental import pallas as pl
from jax.experimental.pallas import tpu as pltpu

F32 = jnp.float32
BF16 = jnp.bfloat16

NORM_EPS = 1e-6
ROPE_BASE = 10000.0
GRID_W = 64
HEAD_DIM = 128
Q_HEADS = 8
KV_HEADS = 2
ATT_BLOCK = 128
M_HEADS = 4
M_QK = 128
M_V = 256
M_CHUNK = 128
CONV_SHORT = 3
CONV_LONG = 31
N_EXPERTS = 8
TOP_K = 2

ROW_TILE = 512
CONV_TILE = 256
HALO = 16
NEG = -1e30
V7X_VMEM_LIMIT = 56 * 1024 * 1024


def _params(n_axes):
    return pltpu.CompilerParams(dimension_semantics=("arbitrary",) * n_axes, vmem_limit_bytes=V7X_VMEM_LIMIT)


def _sigmoid(x):
    return 1.0 / (1.0 + jnp.exp(-x))


def _silu(x):
    return x * _sigmoid(x)


def _ada_kernel(c_ref, w_ref, b_ref, o_ref):
    s = _silu(c_ref[...]).astype(BF16)
    o_ref[...] = jnp.dot(s, w_ref[...].astype(BF16), preferred_element_type=F32) + b_ref[...]


def _ada_table(cvec, ada_w, ada_b, tn=1024):
    depth, d, n = ada_w.shape
    rows = cvec.shape[0]
    return pl.pallas_call(
        _ada_kernel,
        grid=(depth, n // tn),
        in_specs=[pl.BlockSpec((rows, d), lambda l, j: (0, 0)),
                  pl.BlockSpec((None, d, tn), lambda l, j: (l, 0, j)),
                  pl.BlockSpec((None, 1, tn), lambda l, j: (l, 0, j))],
        out_specs=pl.BlockSpec((None, rows, tn), lambda l, j: (l, 0, j)),
        out_shape=jax.ShapeDtypeStruct((depth, rows, n), F32),
        compiler_params=_params(2),
        name="ada_table",
    )(cvec, ada_w, ada_b.reshape(depth, 1, n))


def _normmod(x, w, shift, scale):
    ms = jnp.mean(x * x, axis=-1, keepdims=True)
    y = x * lax.rsqrt(ms + NORM_EPS) * w
    return y * (1.0 + scale) + shift


def _normmod_kernel(h_ref, w_ref, sh_ref, sc_ref, o_ref):
    o_ref[...] = _normmod(h_ref[...], w_ref[...], sh_ref[...], sc_ref[...]).astype(o_ref.dtype)


def _embed_kernel(ctx_ref, x_ref, w_ref, sh_ref, sc_ref, h_ref, a_ref):
    rows = jnp.where(pl.program_id(0) == 0, ctx_ref[...], x_ref[...])
    h_ref[...] = rows
    a_ref[...] = _normmod(rows, w_ref[...], sh_ref[...], sc_ref[...]).astype(a_ref.dtype)


def _embed_norm(ctx_rows, x_rows, w, shift, scale, tiles_per_seq):
    n_ctx, d = ctx_rows.shape
    assert n_ctx == ROW_TILE
    t = n_ctx + x_rows.shape[0]
    seg = _seg_index(tiles_per_seq)
    tile = pl.BlockSpec((ROW_TILE, d), lambda i: (i, 0))
    modv = pl.BlockSpec((None, 1, d), lambda i: (seg(i), 0, 0))
    return pl.pallas_call(
        _embed_kernel,
        grid=(t // ROW_TILE,),
        in_specs=[pl.BlockSpec((ROW_TILE, d), lambda i: (0, 0)),
                  pl.BlockSpec((ROW_TILE, d), lambda i: (jnp.maximum(i - 1, 0), 0)),
                  pl.BlockSpec((1, d), lambda i: (0, 0)), modv, modv],
        out_specs=[tile, tile],
        out_shape=[jax.ShapeDtypeStruct((t, d), F32), jax.ShapeDtypeStruct((t, d), BF16)],
        compiler_params=_params(1),
        name="embed_norm",
    )(ctx_rows, x_rows, w.reshape(1, d), shift, scale)


def _top2_route(a, rwh_ref, rwl_ref, rb_ref):
    a_hi = a.astype(BF16)
    a_lo = (a - a_hi.astype(F32)).astype(BF16)
    w_hi = rwh_ref[...]
    logits = (jnp.dot(a_hi, w_hi, preferred_element_type=F32)
              + (jnp.dot(a_lo, w_hi, preferred_element_type=F32)
                 + jnp.dot(a_hi, rwl_ref[...], preferred_element_type=F32))) + rb_ref[...]
    lane = lax.broadcasted_iota(jnp.int32, logits.shape, 1)
    logits = jnp.where(lane < N_EXPERTS, logits, NEG)
    big = jnp.int32(1 << 20)
    m1 = jnp.max(logits, axis=-1, keepdims=True)
    i1 = jnp.min(jnp.where(logits == m1, lane, big), axis=-1, keepdims=True)
    rest = jnp.where(lane == i1, NEG, logits)
    m2 = jnp.max(rest, axis=-1, keepdims=True)
    i2 = jnp.min(jnp.where(rest == m2, lane, big), axis=-1, keepdims=True)
    e2 = jnp.exp(m2 - m1)
    w1 = 1.0 / (1.0 + e2)
    w2 = e2 / (1.0 + e2)
    r = jnp.where(lane == 0, i1.astype(F32), 0.0)
    r = jnp.where(lane == 1, i2.astype(F32), r)
    r = jnp.where(lane == 2, w1, r)
    r = jnp.where(lane == 3, w2, r)
    return r


def _seg_index(tiles_per_seq):
    return lambda i: (i + tiles_per_seq - 1) // tiles_per_seq


def _norm_modulate(h, w, shift, scale, tiles_per_seq, out_dtype=BF16, row_offset_tiles=0, n_rows=None):
    t, d = h.shape
    n_rows = t if n_rows is None else n_rows
    seg = _seg_index(tiles_per_seq)
    off = row_offset_tiles
    return pl.pallas_call(
        _normmod_kernel,
        grid=(n_rows // ROW_TILE,),
        in_specs=[pl.BlockSpec((ROW_TILE, d), lambda i: (i + off, 0)),
                  pl.BlockSpec((1, d), lambda i: (0, 0)),
                  pl.BlockSpec((None, 1, d), lambda i: (seg(i + off), 0, 0)),
                  pl.BlockSpec((None, 1, d), lambda i: (seg(i + off), 0, 0))],
        out_specs=pl.BlockSpec((ROW_TILE, d), lambda i: (i, 0)),
        out_shape=jax.ShapeDtypeStruct((n_rows, d), out_dtype),
        compiler_params=_params(1),
        name="norm_modulate",
    )(h, w.reshape(1, d), shift, scale)


def _mm_kernel(x_ref, w_ref, o_ref, wb_ref):
    @pl.when(pl.program_id(1) == 0)
    def _():
        wb_ref[...] = w_ref[...].astype(BF16)

    o_ref[...] = jnp.dot(x_ref[...], wb_ref[...], preferred_element_type=F32).astype(o_ref.dtype)


def _matmul(x, w3, layer, n_cols, tn=512, out_dtype=F32):
    t, k = x.shape
    return pl.pallas_call(
        _mm_kernel,
        grid=(n_cols // tn, t // ROW_TILE),
        in_specs=[pl.BlockSpec((ROW_TILE, k), lambda j, i: (i, 0)),
                  pl.BlockSpec((None, k, tn), lambda j, i: (layer, 0, j))],
        out_specs=pl.BlockSpec((ROW_TILE, tn), lambda j, i: (i, j)),
        out_shape=jax.ShapeDtypeStruct((t, n_cols), out_dtype),
        scratch_shapes=[pltpu.VMEM((k, tn), BF16)],
        compiler_params=_params(2),
        name="matmul",
    )(x, w3)


def _mlstm_head_out_rows(hf_ref, hb_ref, op_ref, mw_ref):
    parts = []
    for hd in range(M_HEADS):
        sl = slice(hd * M_V, (hd + 1) * M_V)
        x = hf_ref[:, sl] + hb_ref[:, sl]
        mu = jnp.mean(x, axis=-1, keepdims=True)
        xc = x - mu
        var = jnp.mean(xc * xc, axis=-1, keepdims=True)
        y = xc * lax.rsqrt(var + NORM_EPS) * mw_ref[:, sl]
        parts.append((y * _sigmoid(op_ref[:, sl])).astype(BF16))
    return jnp.concatenate(parts, axis=1)


def _out_proj_kernel(*refs, route, head_out):
    refs = list(refs)
    if head_out:
        x1 = _mlstm_head_out_rows(*refs[:4])
        refs = refs[4:]
    else:
        x1 = refs.pop(0)[...]
    if route:
        (x2_ref, w1_ref, w2_ref, h_ref, g_ref, nw_ref, sh_ref, sc_ref, rwh_ref, rwl_ref, rb_ref,
         ho_ref, a_ref, r_ref) = refs
    else:
        x2_ref, w1_ref, w2_ref, h_ref, g_ref, nw_ref, sh_ref, sc_ref, ho_ref, a_ref = refs
    y = jnp.dot(x1, w1_ref[...], preferred_element_type=F32)
    y = y + jnp.dot(x2_ref[...], w2_ref[...], preferred_element_type=F32)
    hn = h_ref[...] + g_ref[...] * y
    ho_ref[...] = hn
    a = _normmod(hn, nw_ref[...], sh_ref[...], sc_ref[...])
    a_ref[...] = a.astype(a_ref.dtype)
    if route:
        r_ref[...] = _top2_route(a, rwh_ref, rwl_ref, rb_ref)


def _out_proj_norm(x1, x2, w3b, layer, h, gate, norm_w, shift, scale, tiles_per_seq, router=None):
    t, k2 = x2.shape
    k1 = k2
    d = w3b.shape[2]
    seg = _seg_index(tiles_per_seq)
    row = lambda w: pl.BlockSpec((ROW_TILE, w), lambda i: (i, 0))
    modv = pl.BlockSpec((None, 1, d), lambda i: (seg(i), 0, 0))
    head_out = isinstance(x1, tuple)
    if head_out:
        h_f, h_b, z_rest, mnorm_w = x1
        assert h_f.shape == (t, k1)
        x1_specs = [row(k1), row(k1), row(k1), pl.BlockSpec((1, k1), lambda i: (0, 0))]
        x1_args = [h_f, h_b, z_rest, mnorm_w.reshape(1, k1)]
    else:
        assert x1.shape == (t, k1)
        x1_specs, x1_args = [row(k1)], [x1]
    in_specs = x1_specs + [
        row(k2),
        pl.BlockSpec((None, k1, d), lambda i: (layer, 0, 0), pipeline_mode=pl.Buffered(1)),
        pl.BlockSpec((None, k2, d), lambda i: (layer, 1, 0), pipeline_mode=pl.Buffered(1)),
        row(d), modv, pl.BlockSpec((1, d), lambda i: (0, 0)), modv, modv]
    args = x1_args + [x2, w3b, w3b, h, gate, norm_w.reshape(1, d), shift, scale]
    out_specs = [row(d), row(d)]
    out_shape = [jax.ShapeDtypeStruct((t, d), F32), jax.ShapeDtypeStruct((t, d), BF16)]
    if router is not None:
        router_w, router_b = router
        rw = jnp.pad(router_w.astype(F32), ((0, 0), (0, 128 - N_EXPERTS)))
        rw_hi = rw.astype(BF16)
        rw_lo = (rw - rw_hi.astype(F32)).astype(BF16)
        in_specs += [pl.BlockSpec((d, 128), lambda i: (0, 0)), pl.BlockSpec((d, 128), lambda i: (0, 0)),
                     pl.BlockSpec((1, 128), lambda i: (0, 0))]
        args += [rw_hi, rw_lo, jnp.pad(router_b, (0, 128 - N_EXPERTS)).reshape(1, 128)]
        out_specs.append(row(128))
        out_shape.append(jax.ShapeDtypeStruct((t, 128), F32))
    return pl.pallas_call(
        functools.partial(_out_proj_kernel, route=router is not None, head_out=head_out),
        grid=(t // ROW_TILE,),
        in_specs=in_specs,
        out_specs=out_specs,
        out_shape=out_shape,
        compiler_params=_params(1),
        name="out_proj_norm",
    )(*args)


def _ffn_up_kernel(x_ref, wg_ref, wu_ref, o_ref, wgb_ref, wub_ref):
    @pl.when(pl.program_id(1) == 0)
    def _():
        wgb_ref[...] = wg_ref[...].astype(BF16)
        wub_ref[...] = wu_ref[...].astype(BF16)

    x = x_ref[...]
    g = jnp.dot(x, wgb_ref[...], preferred_element_type=F32)
    u = jnp.dot(x, wub_ref[...], preferred_element_type=F32)
    o_ref[...] = (_silu(g) * u).astype(o_ref.dtype)


def _ffn_up(x, w_gate, w_up, layer, tf=512):
    t, k = x.shape
    f = w_gate.shape[2]
    tm = 3 * ROW_TILE if t % (3 * ROW_TILE) == 0 else ROW_TILE
    return pl.pallas_call(
        _ffn_up_kernel,
        grid=(f // tf, t // tm),
        in_specs=[pl.BlockSpec((tm, k), lambda j, i: (i, 0)),
                  pl.BlockSpec((None, k, tf), lambda j, i: (layer, 0, j)),
                  pl.BlockSpec((None, k, tf), lambda j, i: (layer, 0, j))],
        out_specs=pl.BlockSpec((tm, tf), lambda j, i: (i, j)),
        out_shape=jax.ShapeDtypeStruct((t, f), BF16),
        scratch_shapes=[pltpu.VMEM((k, tf), BF16), pltpu.VMEM((k, tf), BF16)],
        compiler_params=_params(2),
        name="ffn_up",
    )(x, w_gate, w_up)


def _mm_resid_kernel(x_ref, w_ref, h_ref, g_ref, o_ref, wb_ref):
    @pl.when(pl.program_id(1) == 0)
    def _():
        wb_ref[...] = w_ref[...].astype(BF16)

    y = jnp.dot(x_ref[...], wb_ref[...], preferred_element_type=F32)
    o_ref[...] = h_ref[...] + g_ref[...] * y


def _ffn_down_residual(x, w3, layer, h, gate, tiles_per_seq, tn=512):
    t, k = x.shape
    n = w3.shape[2]
    seg = _seg_index(tiles_per_seq)
    return pl.pallas_call(
        _mm_resid_kernel,
        grid=(n // tn, t // ROW_TILE),
        in_specs=[pl.BlockSpec((ROW_TILE, k), lambda j, i: (i, 0)),
                  pl.BlockSpec((None, k, tn), lambda j, i: (layer, 0, j)),
                  pl.BlockSpec((ROW_TILE, tn), lambda j, i: (i, j)),
                  pl.BlockSpec((None, 1, tn), lambda j, i: (seg(i), 0, j))],
        out_specs=pl.BlockSpec((ROW_TILE, tn), lambda j, i: (i, j)),
        out_shape=jax.ShapeDtypeStruct((t, n), F32),
        scratch_shapes=[pltpu.VMEM((k, tn), BF16)],
        compiler_params=_params(2),
        name="ffn_down_residual",
    )(x, w3, h, gate)


def _rope(x, cs):
    x = x.astype(F32)
    return x * cs[:, :HEAD_DIM] + pltpu.roll(x, HEAD_DIM // 2, axis=1) * cs[:, HEAD_DIM:]


def _attend(q_all, k_parts, v_parts, sink_ref, o_ref, mask_fn):
    group = Q_HEADS // KV_HEADS
    scale = HEAD_DIM ** -0.5
    for g in range(KV_HEADS):
        qg = jnp.concatenate([q_all[:, (g * group + r) * HEAD_DIM:(g * group + r + 1) * HEAD_DIM]
                              for r in range(group)], axis=0).astype(BF16)
        kg = jnp.concatenate([kp[:, g * HEAD_DIM:(g + 1) * HEAD_DIM] for kp in k_parts], axis=0).astype(BF16)
        vg = jnp.concatenate([vp[:, g * HEAD_DIM:(g + 1) * HEAD_DIM] for vp in v_parts], axis=0).astype(BF16)
        s = lax.dot_general(qg, kg, (((1,), (1,)), ((), ())), preferred_element_type=F32) * scale
        if mask_fn is not None:
            s = jnp.where(mask_fn(s.shape), s, NEG)
        sink = sink_ref[g]
        m = jnp.maximum(jnp.max(s, axis=-1, keepdims=True), sink)
        p = jnp.exp(s - m)
        denom = jnp.sum(p, axis=-1, keepdims=True) + jnp.exp(sink - m)
        o = jnp.dot(p.astype(BF16), vg, preferred_element_type=F32) / denom
        for r in range(group):
            hq = g * group + r
            o_ref[:, hq * HEAD_DIM:(hq + 1) * HEAD_DIM] = o[r * ATT_BLOCK:(r + 1) * ATT_BLOCK].astype(o_ref.dtype)


def _attn_kernel(q_ref, kp_ref, kc_ref, kn_ref, kx_ref, vp_ref, vc_ref, vn_ref, vx_ref,
                 csp_ref, csc_ref, csn_ref, sink_ref, o_ref, *, n_blocks, ctx_blocks):
    step = pl.program_id(1)

    @pl.when(step < ctx_blocks)
    def _():
        _attend(q_ref[...], [kx_ref[...]], [vx_ref[...]], sink_ref, o_ref, None)

    @pl.when(step >= ctx_blocks)
    def _():
        _win_attn_body(q_ref, kp_ref, kc_ref, kn_ref, kx_ref, vp_ref, vc_ref, vn_ref, vx_ref,
                       csp_ref, csc_ref, csn_ref, sink_ref, o_ref, step - ctx_blocks, n_blocks)


def _win_attn_body(q_ref, kp_ref, kc_ref, kn_ref, kx_ref, vp_ref, vc_ref, vn_ref, vx_ref,
                   csp_ref, csc_ref, csn_ref, sink_ref, o_ref, n, n_blocks):
    csc = csc_ref[...]
    q_all = jnp.concatenate([_rope(q_ref[:, h * HEAD_DIM:(h + 1) * HEAD_DIM], csc) for h in range(Q_HEADS)], axis=1)

    def rope_kv(k_ref, cs):
        return jnp.concatenate([_rope(k_ref[:, g * HEAD_DIM:(g + 1) * HEAD_DIM], cs) for g in range(KV_HEADS)], axis=1)

    k_parts = [rope_kv(kp_ref, csp_ref[...]), rope_kv(kc_ref, csc), rope_kv(kn_ref, csn_ref[...]), kx_ref[...]]
    v_parts = [vp_ref[...], vc_ref[...], vn_ref[...], vx_ref[...]]

    def mask_fn(shape):
        row = lax.broadcasted_iota(jnp.int32, shape, 0) & (ATT_BLOCK - 1)
        col = lax.broadcasted_iota(jnp.int32, shape, 1)
        band = (col >= row) & (col <= row + 2 * ATT_BLOCK)
        ok_prev = (col >= ATT_BLOCK) | (n > 0)
        ok_next = (col < 2 * ATT_BLOCK) | (n < n_blocks - 1)
        return (band & ok_prev & ok_next) | (col >= 3 * ATT_BLOCK)

    _attend(q_all, k_parts, v_parts, sink_ref, o_ref, mask_fn)


def _sink_rows(sink):
    group = Q_HEADS // KV_HEADS
    return jnp.repeat(sink.astype(F32).reshape(KV_HEADS, group), ATT_BLOCK, axis=1).reshape(
        KV_HEADS, group * ATT_BLOCK, 1)


def _window_attention(z, cs, sink, batch, ctx_len, seq):
    t = z.shape[0]
    nb = seq // ATT_BLOCK
    cb = ctx_len // ATT_BLOCK
    base = batch * cb
    qw = Q_HEADS * HEAD_DIM
    kw = KV_HEADS * HEAD_DIM
    kcol = qw // kw
    vcol = kcol + 1
    sink_rows = _sink_rows(sink)

    def lat(s):
        return jnp.clip(s - cb, 0, nb - 1)

    def q_block(b, s):
        return jnp.where(s < cb, b * cb + s, base + b * nb + lat(s))

    def kv(shift, col):
        return pl.BlockSpec((ATT_BLOCK, kw), lambda b, s: (base + b * nb + lat(s + shift), col))

    def rot(shift):
        return pl.BlockSpec((ATT_BLOCK, 2 * HEAD_DIM), lambda b, s: (lat(s + shift), 0))

    return pl.pallas_call(
        functools.partial(_attn_kernel, n_blocks=nb, ctx_blocks=cb),
        grid=(batch, cb + nb),
        in_specs=[pl.BlockSpec((ATT_BLOCK, qw), lambda b, s: (q_block(b, s), 0)),
                  kv(-1, kcol), kv(0, kcol), kv(1, kcol),
                  pl.BlockSpec((ctx_len, kw), lambda b, s: (b, kcol)),
                  kv(-1, vcol), kv(0, vcol), kv(1, vcol),
                  pl.BlockSpec((ctx_len, kw), lambda b, s: (b, vcol)),
                  rot(-1), rot(0), rot(1),
                  pl.BlockSpec(sink_rows.shape, lambda b, s: (0, 0, 0))],
        out_specs=pl.BlockSpec((ATT_BLOCK, qw), lambda b, s: (q_block(b, s), 0)),
        out_shape=jax.ShapeDtypeStruct((t, qw), BF16),
        compiler_params=_params(2),
        name="window_attention",
    )(z, z, z, z, z, z, z, z, z, cs, cs, cs, sink_rows)


def _seq_edges(t, batch, ctx_tiles, lat_tiles):
    n_ctx = batch * ctx_tiles
    u = t - n_ctx
    is_ctx = t < n_ctx
    first = jnp.where(is_ctx, lax.rem(t, ctx_tiles) == 0, lax.rem(u, lat_tiles) == 0)
    last = jnp.where(is_ctx, lax.rem(t, ctx_tiles) == ctx_tiles - 1, lax.rem(u, lat_tiles) == lat_tiles - 1)
    return first, last


def _fill_ext(ext_ref, cur, prev, nxt, first, last):
    ext_ref[HALO:HALO + CONV_TILE, :] = cur
    ext_ref[0:HALO, :] = jnp.where(first, 0.0, prev)
    ext_ref[HALO + CONV_TILE:, :] = jnp.where(last, 0.0, nxt)


def _dwconv_rows(ext_ref, w_ref, taps, r0, n_rows):
    pad = taps // 2
    acc = None
    for k in range(taps):
        start = HALO + r0 + k - pad
        term = w_ref[k:k + 1, :] * ext_ref[start:start + n_rows, :]
        acc = term if acc is None else acc + term
    return acc


CONV_ROWS = 32


def _short_conv_kernel(bg_ref, cg_ref, xb_ref, cgp_ref, xbp_ref, cgn_ref, xbn_ref, w_ref, o_ref, ext_ref,
                       *, batch, ctx_tiles, lat_tiles):
    first, last = _seq_edges(pl.program_id(0), batch, ctx_tiles, lat_tiles)
    def prod(a_ref, b_ref):
        return a_ref[...].astype(F32) * b_ref[...].astype(F32)

    _fill_ext(ext_ref, prod(cg_ref, xb_ref), prod(cgp_ref, xbp_ref), prod(cgn_ref, xbn_ref), first, last)
    for r0 in range(0, CONV_TILE, CONV_ROWS):
        conv = _dwconv_rows(ext_ref, w_ref, CONV_SHORT, r0, CONV_ROWS)
        o_ref[r0:r0 + CONV_ROWS, :] = (bg_ref[r0:r0 + CONV_ROWS, :].astype(F32) * conv).astype(o_ref.dtype)


def _halo_specs(width, col, halo_per_tile, n_halo_blocks):
    prev = pl.BlockSpec((HALO, width), lambda t, c: (jnp.maximum(t * halo_per_tile - 1, 0), col(c)))
    nxt = pl.BlockSpec((HALO, width), lambda t, c: (jnp.minimum((t + 1) * halo_per_tile, n_halo_blocks - 1), col(c)))
    return prev, nxt


def _short_conv(z, conv_w3, layer, batch, ctx_len, seq):
    t = z.shape[0]
    cw = 512
    n_c = 1024 // cw
    bg0, cg0, xb0 = 1536 // cw, 2560 // cw, 3584 // cw
    hpt = CONV_TILE // HALO
    nh = t // HALO
    cgp, cgn = _halo_specs(cw, lambda c: cg0 + c, hpt, nh)
    xbp, xbn = _halo_specs(cw, lambda c: xb0 + c, hpt, nh)
    kern = functools.partial(_short_conv_kernel, batch=batch, ctx_tiles=ctx_len // CONV_TILE,
                             lat_tiles=seq // CONV_TILE)
    return pl.pallas_call(
        kern,
        grid=(t // CONV_TILE, n_c),
        in_specs=[pl.BlockSpec((CONV_TILE, cw), lambda t_, c: (t_, bg0 + c)),
                  pl.BlockSpec((CONV_TILE, cw), lambda t_, c: (t_, cg0 + c)),
                  pl.BlockSpec((CONV_TILE, cw), lambda t_, c: (t_, xb0 + c)),
                  cgp, xbp, cgn, xbn,
                  pl.BlockSpec((None, CONV_SHORT, cw), lambda t_, c: (layer, 0, c))],
        out_specs=pl.BlockSpec((CONV_TILE, cw), lambda t_, c: (t_, c)),
        out_shape=jax.ShapeDtypeStruct((t, 1024), BF16),
        scratch_shapes=[pltpu.VMEM((CONV_TILE + 2 * HALO, cw), F32)],
        compiler_params=_params(2),
        name="short_conv",
    )(z, z, z, z, z, z, z, conv_w3)


SUBLANES = 8
SHIFT_ROWS = CONV_TILE + 2 * HALO - SUBLANES


def _dwconv_rows_aligned(ext_ref, sh_ref, w8_ref, taps, r0, n_rows):
    pad = taps // 2
    width = ext_ref.shape[1]
    acc = None
    for k in range(taps):
        start = HALO + r0 + k - pad
        b = start % SUBLANES
        a8 = start - b
        src = ext_ref[a8:a8 + n_rows, :] if b == 0 else sh_ref[b - 1, a8:a8 + n_rows, :]
        term = (src.reshape(n_rows // SUBLANES, SUBLANES, width) * w8_ref[k][None]).reshape(n_rows, width)
        acc = term if acc is None else acc + term
    return acc


def _conformer_kernel(a_ref, g_ref, ap_ref, gp_ref, an_ref, gn_ref, w8_ref, b_ref, lw_ref, lb_ref, o_ref,
                      ext_ref, sh_ref, *, batch, ctx_tiles, lat_tiles):
    first, last = _seq_edges(pl.program_id(0), batch, ctx_tiles, lat_tiles)
    _fill_ext(ext_ref, a_ref[...] * _sigmoid(g_ref[...]), ap_ref[...] * _sigmoid(gp_ref[...]),
              an_ref[...] * _sigmoid(gn_ref[...]), first, last)
    for b in range(1, SUBLANES):
        sh_ref[b - 1] = ext_ref[b:b + SHIFT_ROWS, :]
    for r0 in range(0, CONV_TILE, CONV_ROWS):
        u = _dwconv_rows_aligned(ext_ref, sh_ref, w8_ref, CONV_LONG, r0, CONV_ROWS) + b_ref[...]
        mu = jnp.mean(u, axis=-1, keepdims=True)
        uc = u - mu
        var = jnp.mean(uc * uc, axis=-1, keepdims=True)
        y = uc * lax.rsqrt(var + NORM_EPS) * lw_ref[...] + lb_ref[...]
        o_ref[r0:r0 + CONV_ROWS, :] = _silu(y).astype(o_ref.dtype)


def _conformer_conv(z, a_col, g_col, dw_w3, dw_b, ln_w, ln_b, layer, batch, ctx_len, seq):
    t = z.shape[0]
    cw = 1024
    hpt = CONV_TILE // HALO
    nh = t // HALO
    ap, an = _halo_specs(cw, lambda c: a_col, hpt, nh)
    gp, gn = _halo_specs(cw, lambda c: g_col, hpt, nh)
    kern = functools.partial(_conformer_kernel, batch=batch, ctx_tiles=ctx_len // CONV_TILE,
                             lat_tiles=seq // CONV_TILE)
    vec = pl.BlockSpec((None, 1, cw), lambda t_, c: (layer, 0, 0))
    n_layers = dw_b.shape[0]
    return pl.pallas_call(
        kern,
        grid=(t // CONV_TILE, 1),
        in_specs=[pl.BlockSpec((CONV_TILE, cw), lambda t_, c: (t_, a_col)),
                  pl.BlockSpec((CONV_TILE, cw), lambda t_, c: (t_, g_col)),
                  ap, gp, an, gn,
                  pl.BlockSpec((None, CONV_LONG, SUBLANES, cw), lambda t_, c: (layer, 0, 0, 0)),
                  vec, vec, vec],
        out_specs=pl.BlockSpec((CONV_TILE, cw), lambda t_, c: (t_, 0)),
        out_shape=jax.ShapeDtypeStruct((t, cw), BF16),
        scratch_shapes=[pltpu.VMEM((CONV_TILE + 2 * HALO, cw), F32),
                        pltpu.VMEM((SUBLANES - 1, SHIFT_ROWS, cw), F32)],
        compiler_params=_params(2),
        name="conformer_conv",
    )(z, z, z, z, z, z, jnp.broadcast_to(dw_w3[:, :, None, :], (n_layers, CONV_LONG, SUBLANES, cw)),
      dw_b.reshape(n_layers, 1, cw), ln_w.reshape(n_layers, 1, cw), ln_b.reshape(n_layers, 1, cw))


def _log_sigmoid(x):
    return jnp.minimum(x, 0.0) - jnp.log(1.0 + jnp.exp(-jnp.abs(x)))


def _mlstm_kernel(qf_ref, kf_ref, vf_ref, gcf_ref, grf_ref, qb_ref, kb_ref, vb_ref, gcb_ref, grb_ref,
                  bc_ref, br_ref, of_ref, ob_ref, *state_refs):
    n_chain = 2 * M_HEADS
    cta_ref, m_ref = state_refs[:n_chain], state_refs[n_chain:]

    @pl.when(pl.program_id(1) == 0)
    def _():
        for ref in state_refs:
            ref[...] = jnp.zeros_like(ref)

    ln = M_CHUNK
    row = lax.broadcasted_iota(jnp.int32, (ln, ln), 0)
    col = lax.broadcasted_iota(jnp.int32, (ln, ln), 1)
    lower = col <= row
    upper = col >= row
    _mlstm_direction(qf_ref, kf_ref, vf_ref, gcf_ref[...] + bc_ref[0], grf_ref[...] + br_ref[0], lower, upper,
                     of_ref, cta_ref, m_ref, 0)
    _mlstm_direction(qb_ref, kb_ref, vb_ref, gcb_ref[...] + bc_ref[1], grb_ref[...] + br_ref[1], upper, lower,
                     ob_ref, cta_ref, m_ref, M_HEADS)


def _mlstm_direction(q_ref, k_ref, v_ref, gc, gr, mask, mask_t, o_ref, cta_ref, m_ref, state0):
    ln = M_CHUNK
    one_col = lax.broadcasted_iota(jnp.int32, (ln, 128), 1) == 0
    ones_blk = jnp.where(one_col, 1.0, 0.0).astype(BF16)
    for h in range(M_HEADS):
        st = state0 + h
        ig_c = gc[:, h:h + 1]
        lf_c = _log_sigmoid(gc[:, M_HEADS + h:M_HEADS + h + 1])
        ig_r = gr[h:h + 1, :]
        lf_r = _log_sigmoid(gr[M_HEADS + h:M_HEADS + h + 1, :])
        qb = (q_ref[:, h * M_QK:(h + 1) * M_QK] * (M_QK ** -0.5)).astype(BF16)
        k = k_ref[:, h * M_QK:(h + 1) * M_QK]
        v = v_ref[:, h * M_V:(h + 1) * M_V]
        cum_c = jnp.sum(jnp.where(mask, lf_r, 0.0), axis=1, keepdims=True)
        cum_r = jnp.sum(jnp.where(mask_t, lf_c, 0.0), axis=0, keepdims=True)
        total = jnp.sum(lf_r, axis=1, keepdims=True)
        m_s = m_ref[st][...]
        g = jnp.where(mask, ig_r - cum_r, NEG)
        mm = jnp.maximum(m_s, jnp.max(g, axis=1, keepdims=True))
        w_inter = jnp.exp(m_s - mm)
        s = lax.dot_general(qb, k.astype(BF16), (((1,), (1,)), ((), ())), preferred_element_type=F32) * jnp.exp(g - mm)
        cta = cta_ref[st][...]
        va = jnp.concatenate([v.astype(BF16), ones_blk], axis=1)
        inter = jnp.dot(qb, cta.astype(BF16), preferred_element_type=F32)
        intra = jnp.dot(s.astype(BF16), va, preferred_element_type=F32)
        num = w_inter * inter[:, :M_V] + intra[:, :M_V]
        den = w_inter * inter[:, M_V:M_V + 1] + intra[:, M_V:M_V + 1]
        o_ref[:, h * M_V:(h + 1) * M_V] = num / jnp.maximum(jnp.abs(den), jnp.exp(-(cum_c + mm)))
        gl = total - cum_c + ig_c
        m_new = jnp.maximum(total + m_s, jnp.max(gl, axis=0, keepdims=True))
        wg = jnp.exp(gl - m_new)
        decay = jnp.exp(total + m_s - m_new)
        wva = jnp.concatenate([(wg * v).astype(BF16), jnp.where(one_col, wg, 0.0).astype(BF16)], axis=1)
        cta_ref[st][...] = decay * cta + jnp.dot(k.astype(BF16).T, wva, preferred_element_type=F32)
        m_ref[st][...] = m_new


def _mlstm(zq, gates, gate_b, batch, ctx_len, seq):
    t = zq.shape[0]
    cb = ctx_len // M_CHUNK
    nc = seq // M_CHUNK
    base = batch * cb
    steps = cb + nc
    half = 2 * M_HEADS
    g_dir = gates.reshape(t, 2, half).transpose(1, 0, 2)
    g_dir_t = g_dir.transpose(0, 2, 1)
    b_dir = gate_b.astype(F32).reshape(2, 1, half)
    b_dir_t = b_dir.transpose(0, 2, 1)

    def rb_f(b, i):
        return jnp.where(i < cb, b * cb + i, base + b * nc + (i - cb))

    def rb_b(b, i):
        return jnp.where(i < cb, b * cb + (cb - 1 - i), base + b * nc + (nc - 1 - (i - cb)))

    qk_w = M_HEADS * M_QK
    v_w = M_HEADS * M_V

    def chunk_specs(d, rb):
        return [pl.BlockSpec((M_CHUNK, qk_w), lambda b, i: (rb(b, i), 0)),
                pl.BlockSpec((M_CHUNK, qk_w), lambda b, i: (rb(b, i), 1)),
                pl.BlockSpec((M_CHUNK, v_w), lambda b, i: (rb(b, i), 1)),
                pl.BlockSpec((None, M_CHUNK, half), lambda b, i: (d, rb(b, i), 0)),
                pl.BlockSpec((None, half, M_CHUNK), lambda b, i: (d, 0, rb(b, i)))]

    return pl.pallas_call(
        _mlstm_kernel,
        grid=(batch, steps),
        in_specs=chunk_specs(0, rb_f) + chunk_specs(1, rb_b) + [
            pl.BlockSpec((2, 1, half), lambda b, i: (0, 0, 0)),
            pl.BlockSpec((2, half, 1), lambda b, i: (0, 0, 0))],
        out_specs=[pl.BlockSpec((M_CHUNK, v_w), lambda b, i: (rb_f(b, i), 0)),
                   pl.BlockSpec((M_CHUNK, v_w), lambda b, i: (rb_b(b, i), 0))],
        out_shape=[jax.ShapeDtypeStruct((t, v_w), F32), jax.ShapeDtypeStruct((t, v_w), F32)],
        scratch_shapes=([pltpu.VMEM((M_QK, M_V + 128), F32)] * (2 * M_HEADS)
                        + [pltpu.VMEM((1, 1), F32)] * (2 * M_HEADS)),
        compiler_params=_params(2),
        name="mlstm",
    )(zq, zq, zq, g_dir, g_dir_t, zq, zq, zq, g_dir, g_dir_t, b_dir, b_dir_t)


def _fresh_weights(be_ref, rb):
    return (rb == 0) | (be_ref[rb] != be_ref[jnp.maximum(rb - 1, 0)])


SUB_TILE = 128
N_SUB = ROW_TILE // SUB_TILE


def _per_fill(nv_ref, rb, o_ref, rows_fn):
    pieces = jnp.where(rb < nv_ref[0], nv_ref[1 + rb], 0)
    for n in range(N_SUB + 1):
        @pl.when(pieces == n)
        def _(n=n):
            rows = n * SUB_TILE
            if rows:
                o_ref[:rows, :] = rows_fn(rows)
            if rows < ROW_TILE:
                o_ref[rows:, :] = jnp.zeros((ROW_TILE - rows, o_ref.shape[1]), o_ref.dtype)


def _moe_up_kernel(be_ref, nv_ref, x_ref, wg_ref, wu_ref, o_ref, wgb_ref, wub_ref):
    rb = pl.program_id(1)

    @pl.when((rb < nv_ref[0]) & _fresh_weights(be_ref, rb))
    def _():
        wgb_ref[...] = wg_ref[...].astype(BF16)
        wub_ref[...] = wu_ref[...].astype(BF16)

    def swiglu(rows):
        x = x_ref[:rows, :]
        g = jnp.dot(x, wgb_ref[...], preferred_element_type=F32)
        u = jnp.dot(x, wub_ref[...], preferred_element_type=F32)
        return (_silu(g) * u).astype(o_ref.dtype)

    _per_fill(nv_ref, rb, o_ref, swiglu)


def _moe_down_kernel(be_ref, nv_ref, x_ref, w_ref, o_ref, wb_ref):
    rb = pl.program_id(1)

    @pl.when((rb < nv_ref[0]) & _fresh_weights(be_ref, rb))
    def _():
        wb_ref[...] = w_ref[...].astype(BF16)

    def project(rows):
        return jnp.dot(x_ref[:rows, :], wb_ref[...], preferred_element_type=F32).astype(o_ref.dtype)

    _per_fill(nv_ref, rb, o_ref, project)


def _moe_experts(x_sorted, block_expert, n_valid, w_gate, w_up, w_down, layer, tf=512, tn=512):
    p, d = x_sorted.shape
    f = w_gate.shape[3]
    nblk = p // ROW_TILE

    def row(rb, nv):
        return jnp.minimum(rb, nv[0] - 1)

    g = pl.pallas_call(
        _moe_up_kernel,
        grid_spec=pltpu.PrefetchScalarGridSpec(
            num_scalar_prefetch=2,
            grid=(f // tf, nblk),
            in_specs=[pl.BlockSpec((ROW_TILE, d), lambda j, rb, be, nv: (row(rb, nv), 0)),
                      pl.BlockSpec((None, None, d, tf), lambda j, rb, be, nv: (layer, be[rb], 0, j)),
                      pl.BlockSpec((None, None, d, tf), lambda j, rb, be, nv: (layer, be[rb], 0, j))],
            out_specs=pl.BlockSpec((ROW_TILE, tf), lambda j, rb, be, nv: (rb, j)),
            scratch_shapes=[pltpu.VMEM((d, tf), BF16), pltpu.VMEM((d, tf), BF16)]),
        out_shape=jax.ShapeDtypeStruct((p, f), BF16),
        compiler_params=_params(2),
        name="moe_up",
    )(block_expert, n_valid, x_sorted, w_gate, w_up)
    return pl.pallas_call(
        _moe_down_kernel,
        grid_spec=pltpu.PrefetchScalarGridSpec(
            num_scalar_prefetch=2,
            grid=(d // tn, nblk),
            in_specs=[pl.BlockSpec((ROW_TILE, f), lambda j, rb, be, nv: (row(rb, nv), 0)),
                      pl.BlockSpec((None, None, f, tn), lambda j, rb, be, nv: (layer, be[rb], 0, j))],
            out_specs=pl.BlockSpec((ROW_TILE, tn), lambda j, rb, be, nv: (rb, j)),
            scratch_shapes=[pltpu.VMEM((f, tn), BF16)]),
        out_shape=jax.ShapeDtypeStruct((p, d), BF16),
        compiler_params=_params(2),
        name="moe_down",
    )(block_expert, n_valid, g, w_down)


def _routing_plan(route):
    t = route.shape[0]
    tm = ROW_TILE
    n_assign = TOP_K * t
    nblk = (n_assign + N_EXPERTS * (tm - 1)) // tm
    e = route[:, :TOP_K].astype(jnp.int32).reshape(-1)
    onehot = (e[:, None] == jnp.arange(N_EXPERTS, dtype=jnp.int32)[None, :]).astype(jnp.int32)
    csum = jnp.cumsum(onehot, axis=0)
    counts = csum[-1]
    padded = ((counts + tm - 1) // tm) * tm
    ends = jnp.cumsum(padded)
    starts = ends - padded
    dest = jnp.sum((csum - onehot + starts[None, :]) * onehot, axis=1)
    src_tok = (jnp.arange(nblk * tm, dtype=jnp.int32) % t).at[dest].set(
        jnp.arange(n_assign, dtype=jnp.int32) // TOP_K, unique_indices=True, mode="promise_in_bounds")
    n_valid = (ends[-1] // tm).astype(jnp.int32)
    blk_start = jnp.arange(nblk, dtype=jnp.int32) * tm
    be = jnp.sum((ends[None, :] <= blk_start[:, None]).astype(jnp.int32), axis=1)
    be = jnp.minimum(be, N_EXPERTS - 1).astype(jnp.int32)
    group_rows_end = jnp.sum((be[:, None] == jnp.arange(N_EXPERTS, dtype=jnp.int32)[None, :]).astype(jnp.int32)
                             * (starts + counts)[None, :], axis=1)
    pieces = jnp.clip((group_rows_end - blk_start + SUB_TILE - 1) // SUB_TILE, 0, N_SUB).astype(jnp.int32)
    block_info = jnp.concatenate([n_valid.reshape(1), pieces])
    return src_tok, be, block_info, dest.reshape(t, TOP_K)


def _combine_kernel(h_ref, y0_ref, y1_ref, r_ref, g_ref, nw_ref, sh_ref, sc_ref, *out_refs):
    r = r_ref[...]
    moe = r[:, TOP_K:TOP_K + 1] * y0_ref[...].astype(F32) + r[:, TOP_K + 1:TOP_K + 2] * y1_ref[...].astype(F32)
    hn = h_ref[...] + g_ref[...] * moe
    if len(out_refs) == 2:
        out_refs[0][...] = hn
    out_refs[-1][...] = _normmod(hn, nw_ref[...], sh_ref[...], sc_ref[...]).astype(out_refs[-1].dtype)


def _combine_norm(h, y_pair, route, gate, norm_w, shift, scale, tiles_per_seq, row_offset_tiles, emit_h, a_dtype):
    d = h.shape[1]
    n_rows = y_pair.shape[0] // TOP_K
    n_tiles = n_rows // ROW_TILE
    off = row_offset_tiles
    seg = _seg_index(tiles_per_seq)
    shifted = lambda w: pl.BlockSpec((ROW_TILE, w), lambda i: (i + off, 0))
    tile = pl.BlockSpec((ROW_TILE, d), lambda i: (i, 0))
    tile1 = pl.BlockSpec((ROW_TILE, d), lambda i: (i + n_tiles, 0))
    modv = pl.BlockSpec((None, 1, d), lambda i: (seg(i + off), 0, 0))
    out_specs = [tile, tile] if emit_h else [tile]
    out_shape = [jax.ShapeDtypeStruct((n_rows, d), a_dtype)]
    if emit_h:
        out_shape.insert(0, jax.ShapeDtypeStruct((n_rows, d), F32))
    return pl.pallas_call(
        _combine_kernel,
        grid=(n_rows // ROW_TILE,),
        in_specs=[shifted(d), tile, tile1, shifted(128), modv, pl.BlockSpec((1, d), lambda i: (0, 0)), modv, modv],
        out_specs=out_specs,
        out_shape=out_shape,
        compiler_params=_params(1),
        name="moe_combine_norm",
    )(h, y_pair, y_pair, route, gate, norm_w.reshape(1, d), shift, scale)


def _rope_table(seq):
    rows = seq // GRID_W
    row = jnp.repeat(jnp.arange(rows, dtype=F32), GRID_W)
    col = jnp.tile(jnp.arange(GRID_W, dtype=F32), rows)
    n_freq = HEAD_DIM // 4
    inv_freq = ROPE_BASE ** (-jnp.arange(n_freq, dtype=F32) / n_freq)
    ang = jnp.concatenate([row[:, None] * inv_freq, col[:, None] * inv_freq], axis=-1)
    cos, sin = jnp.cos(ang), jnp.sin(ang)
    return jnp.concatenate([cos, cos, -sin, sin], axis=-1)


def kernel(x, c, ctx, c_ctx, ada_w, ada_b, norm1_w, norm2_w, ev_w_in, ev_sink, ev_conv_w, ev_w_out, ffn_w_gate, ffn_w_up, ffn_w_down, od_w_in, od_gate_b, od_mnorm_w, od_dw_w, od_dw_b, od_ln_w, od_ln_b, od_w_out, moe_router_w, moe_router_b, moe_w_gate, moe_w_up, moe_w_down, final_w):
    batch, seq, d = x.shape
    ctx_len = ctx.shape[1]
    depth = ada_w.shape[0]
    assert batch * ctx_len == ROW_TILE and seq % ROW_TILE == 0 and ctx_len % CONV_TILE == 0
    assert 1 + batch <= 8
    tps = seq // ROW_TILE
    n_ctx_rows = batch * ctx_len

    cs = _rope_table(seq)
    cvec = jnp.zeros((8, d), F32).at[0].set(c_ctx).at[1:1 + batch].set(c)
    mod_table = _ada_table(cvec, ada_w, ada_b)

    def mods_of(layer):
        mods = mod_table[layer, :1 + batch].reshape(1 + batch, 6, 1, d)
        return [mods[:, i] for i in range(6)]

    zero_mod = jnp.zeros((1 + batch, 1, d), F32)
    ctx_tiles = n_ctx_rows // ROW_TILE
    ev_w_out_b = ev_w_out.astype(BF16)
    od_w_out_b = od_w_out.astype(BF16)

    def take(rows, idx):
        return rows.at[idx].get(mode="promise_in_bounds")

    out = None
    mod = mods_of(0)
    h, a1 = _embed_norm(ctx.reshape(n_ctx_rows, d), x.reshape(batch * seq, d), norm1_w[0], mod[0], mod[1], tps)
    for layer in range(depth):
        j = layer // 2
        last = layer == depth - 1
        next_mod = None if last else mods_of(layer + 1)
        if layer % 2 == 0:
            z = _matmul(a1, ev_w_in, j, ev_w_in.shape[2], tn=1536, out_dtype=BF16)
            o_a = _window_attention(z, cs, ev_sink[j], batch, ctx_len, seq)
            o_b = _short_conv(z, ev_conv_w, j, batch, ctx_len, seq)
            h, a2 = _out_proj_norm(o_a, o_b, ev_w_out_b, j, h, mod[2], norm2_w[layer], mod[3], mod[4], tps)
            g = _ffn_up(a2, ffn_w_gate, ffn_w_up, j)
            h = _ffn_down_residual(g, ffn_w_down, j, h, mod[5], tps)
            if last:
                out = _norm_modulate(h, final_w, zero_mod, zero_mod, tps, out_dtype=F32,
                                     row_offset_tiles=ctx_tiles, n_rows=batch * seq)
            else:
                a1 = _norm_modulate(h, norm1_w[layer + 1], next_mod[0], next_mod[1], tps)
        else:
            qkv_w = M_HEADS * (2 * M_QK + M_V)
            n_gate = 4 * M_HEADS
            zq = _matmul(a1, od_w_in, j, qkv_w, tn=1024)
            w_gate_cols = jnp.pad(od_w_in[j, :, qkv_w:qkv_w + n_gate], ((0, 0), (0, 128 - n_gate)))[None]
            gates = _matmul(a1, w_gate_cols, 0, 128, tn=128)[:, :n_gate]
            w_rest = od_w_in[j, :, qkv_w + n_gate:][None]
            z_rest = _matmul(a1, w_rest, 0, w_rest.shape[2], tn=1536)
            h_f, h_b = _mlstm(zq, gates, od_gate_b[j], batch, ctx_len, seq)
            u_out = _conformer_conv(z_rest, 1, 2, od_dw_w, od_dw_b, od_ln_w, od_ln_b, j, batch, ctx_len, seq)
            h, a2, route = _out_proj_norm((h_f, h_b, z_rest, od_mnorm_w[j]), u_out, od_w_out_b, j, h, mod[2],
                                          norm2_w[layer], mod[3], mod[4], tps,
                                          router=(moe_router_w[j], moe_router_b[j]))
            row0 = n_ctx_rows if last else 0
            src_tok, block_expert, n_valid, pos = _routing_plan(route[row0:])
            x_sorted = jnp.take(a2, src_tok + row0, axis=0, mode="clip")
            y = _moe_experts(x_sorted, block_expert, n_valid, moe_w_gate, moe_w_up, moe_w_down, j)
            y_pair = take(y, pos.T.reshape(-1))
            if last:
                out = _combine_norm(h, y_pair, route, mod[5], final_w, zero_mod, zero_mod, tps, ctx_tiles,
                                    emit_h=False, a_dtype=F32)[0]
            else:
                h, a1 = _combine_norm(h, y_pair, route, mod[5], norm1_w[layer + 1], next_mod[0], next_mod[1], tps,
                                      0, emit_h=True, a_dtype=BF16)
        mod = next_mod
    return out.reshape(batch, seq, d)
```

```python
import functools

import jax
import jax.numpy as jnp
from jax import lax
from jax.experimental import pallas as pl
from jax.experimental.pallas import tpu as pltpu

F32 = jnp.float32
BF16 = jnp.bfloat16

NORM_EPS = 1e-6
ROPE_BASE = 10000.0
GRID_W = 64
HEAD_DIM = 128
Q_HEADS = 8
KV_HEADS = 2
ATT_BLOCK = 128
M_HEADS = 4
M_QK = 128
M_V = 256
M_CHUNK = 128
CONV_SHORT = 3
CONV_LONG = 31
N_EXPERTS = 8
TOP_K = 2

ROW_TILE = 512
CONV_TILE = 256
HALO = 16
NEG = -1e30
V7X_VMEM_LIMIT = 56 * 1024 * 1024


def _params(n_axes):
    return pltpu.CompilerParams(dimension_semantics=("arbitrary",) * n_axes, vmem_limit_bytes=V7X_VMEM_LIMIT)


def _sigmoid(x):
    return 1.0 / (1.0 + jnp.exp(-x))


def _silu(x):
    return x * _sigmoid(x)


def _ada_kernel(c_ref, w_ref, b_ref, o_ref):
    s = _silu(c_ref[...]).astype(BF16)
    o_ref[...] = jnp.dot(s, w_ref[...].astype(BF16), preferred_element_type=F32) + b_ref[...]


def _ada_table(cvec, ada_w, ada_b, tn=1024):
    depth, d, n = ada_w.shape
    rows = cvec.shape[0]
    return pl.pallas_call(
        _ada_kernel,
        grid=(depth, n // tn),
        in_specs=[pl.BlockSpec((rows, d), lambda l, j: (0, 0)),
                  pl.BlockSpec((None, d, tn), lambda l, j: (l, 0, j)),
                  pl.BlockSpec((None, 1, tn), lambda l, j: (l, 0, j))],
        out_specs=pl.BlockSpec((None, rows, tn), lambda l, j: (l, 0, j)),
        out_shape=jax.ShapeDtypeStruct((depth, rows, n), F32),
        compiler_params=_params(2),
        name="ada_table",
    )(cvec, ada_w, ada_b.reshape(depth, 1, n))


def _normmod(x, w, shift, scale):
    ms = jnp.mean(x * x, axis=-1, keepdims=True)
    y = x * lax.rsqrt(ms + NORM_EPS) * w
    return y * (1.0 + scale) + shift


def _normmod_kernel(h_ref, w_ref, sh_ref, sc_ref, o_ref):
    o_ref[...] = _normmod(h_ref[...], w_ref[...], sh_ref[...], sc_ref[...]).astype(o_ref.dtype)


def _embed_kernel(ctx_ref, x_ref, w_ref, sh_ref, sc_ref, h_ref, a_ref):
    rows = jnp.where(pl.program_id(0) == 0, ctx_ref[...], x_ref[...])
    h_ref[...] = rows
    a_ref[...] = _normmod(rows, w_ref[...], sh_ref[...], sc_ref[...]).astype(a_ref.dtype)


def _embed_norm(ctx_rows, x_rows, w, shift, scale, tiles_per_seq):
    n_ctx, d = ctx_rows.shape
    assert n_ctx == ROW_TILE
    t = n_ctx + x_rows.shape[0]
    seg = _seg_index(tiles_per_seq)
    tile = pl.BlockSpec((ROW_TILE, d), lambda i: (i, 0))
    modv = pl.BlockSpec((None, 1, d), lambda i: (seg(i), 0, 0))
    return pl.pallas_call(
        _embed_kernel,
        grid=(t // ROW_TILE,),
        in_specs=[pl.BlockSpec((ROW_TILE, d), lambda i: (0, 0)),
                  pl.BlockSpec((ROW_TILE, d), lambda i: (jnp.maximum(i - 1, 0), 0)),
                  pl.BlockSpec((1, d), lambda i: (0, 0)), modv, modv],
        out_specs=[tile, tile],
        out_shape=[jax.ShapeDtypeStruct((t, d), F32), jax.ShapeDtypeStruct((t, d), BF16)],
        compiler_params=_params(1),
        name="embed_norm",
    )(ctx_rows, x_rows, w.reshape(1, d), shift, scale)


def _top2_route(a, rwh_ref, rwl_ref, rb_ref):
    a_hi = a.astype(BF16)
    a_lo = (a - a_hi.astype(F32)).astype(BF16)
    w_hi = rwh_ref[...]
    logits = (jnp.dot(a_hi, w_hi, preferred_element_type=F32)
              + (jnp.dot(a_lo, w_hi, preferred_element_type=F32)
                 + jnp.dot(a_hi, rwl_ref[...], preferred_element_type=F32))) + rb_ref[...]
    lane = lax.broadcasted_iota(jnp.int32, logits.shape, 1)
    logits = jnp.where(lane < N_EXPERTS, logits, NEG)
    big = jnp.int32(1 << 20)
    m1 = jnp.max(logits, axis=-1, keepdims=True)
    i1 = jnp.min(jnp.where(logits == m1, lane, big), axis=-1, keepdims=True)
    rest = jnp.where(lane == i1, NEG, logits)
    m2 = jnp.max(rest, axis=-1, keepdims=True)
    i2 = jnp.min(jnp.where(rest == m2, lane, big), axis=-1, keepdims=True)
    e2 = jnp.exp(m2 - m1)
    w1 = 1.0 / (1.0 + e2)
    w2 = e2 / (1.0 + e2)
    r = jnp.where(lane == 0, i1.astype(F32), 0.0)
    r = jnp.where(lane == 1, i2.astype(F32), r)
    r = jnp.where(lane == 2, w1, r)
    r = jnp.where(lane == 3, w2, r)
    return r


def _seg_index(tiles_per_seq):
    return lambda i: (i + tiles_per_seq - 1) // tiles_per_seq


def _norm_modulate(h, w, shift, scale, tiles_per_seq, out_dtype=BF16, row_offset_tiles=0, n_rows=None):
    t, d = h.shape
    n_rows = t if n_rows is None else n_rows
    seg = _seg_index(tiles_per_seq)
    off = row_offset_tiles
    return pl.pallas_call(
        _normmod_kernel,
        grid=(n_rows // ROW_TILE,),
        in_specs=[pl.BlockSpec((ROW_TILE, d), lambda i: (i + off, 0)),
                  pl.BlockSpec((1, d), lambda i: (0, 0)),
                  pl.BlockSpec((None, 1, d), lambda i: (seg(i + off), 0, 0)),
                  pl.BlockSpec((None, 1, d), lambda i: (seg(i + off), 0, 0))],
        out_specs=pl.BlockSpec((ROW_TILE, d), lambda i: (i, 0)),
        out_shape=jax.ShapeDtypeStruct((n_rows, d), out_dtype),
        compiler_params=_params(1),
        name="norm_modulate",
    )(h, w.reshape(1, d), shift, scale)


def _mm_kernel(x_ref, w_ref, o_ref, wb_ref):
    @pl.when(pl.program_id(1) == 0)
    def _():
        wb_ref[...] = w_ref[...].astype(BF16)

    o_ref[...] = jnp.dot(x_ref[...], wb_ref[...], preferred_element_type=F32).astype(o_ref.dtype)


def _mm_side_kernel(x_ref, w_ref, ws_ref, o_ref, os_ref, wb_ref):
    @pl.when(pl.program_id(1) == 0)
    def _():
        wb_ref[...] = w_ref[...].astype(BF16)

    x = x_ref[...]
    o_ref[...] = jnp.dot(x, wb_ref[...], preferred_element_type=F32).astype(o_ref.dtype)
    os_ref[...] = jnp.dot(x, ws_ref[...].astype(BF16), preferred_element_type=F32)


def _matmul_with_side(x, w3, layer, n_cols, w_side, tn=512):
    t, k = x.shape
    ns = w_side.shape[1]
    z, side = pl.pallas_call(
        _mm_side_kernel,
        grid=(n_cols // tn, t // ROW_TILE),
        in_specs=[pl.BlockSpec((ROW_TILE, k), lambda j, i: (i, 0)),
                  pl.BlockSpec((None, k, tn), lambda j, i: (layer, 0, j)),
                  pl.BlockSpec((k, ns), lambda j, i: (0, 0))],
        out_specs=[pl.BlockSpec((ROW_TILE, tn), lambda j, i: (i, j)),
                   pl.BlockSpec((None, ROW_TILE, ns), lambda j, i: (j, i, 0))],
        out_shape=[jax.ShapeDtypeStruct((t, n_cols), F32), jax.ShapeDtypeStruct((n_cols // tn, t, ns), F32)],
        scratch_shapes=[pltpu.VMEM((k, tn), BF16)],
        compiler_params=_params(2),
        name="matmul_with_side",
    )(x, w3, w_side)
    return z, side[0]


def _matmul(x, w3, layer, n_cols, tn=512, out_dtype=F32):
    t, k = x.shape
    return pl.pallas_call(
        _mm_kernel,
        grid=(n_cols // tn, t // ROW_TILE),
        in_specs=[pl.BlockSpec((ROW_TILE, k), lambda j, i: (i, 0)),
                  pl.BlockSpec((None, k, tn), lambda j, i: (layer, 0, j))],
        out_specs=pl.BlockSpec((ROW_TILE, tn), lambda j, i: (i, j)),
        out_shape=jax.ShapeDtypeStruct((t, n_cols), out_dtype),
        scratch_shapes=[pltpu.VMEM((k, tn), BF16)],
        compiler_params=_params(2),
        name="matmul",
    )(x, w3)


def _mlstm_head_out_rows(hf_ref, hb_ref, op_ref, mw_ref):
    parts = []
    for hd in range(M_HEADS):
        sl = slice(hd * M_V, (hd + 1) * M_V)
        x = hf_ref[:, sl] + hb_ref[:, sl]
        mu = jnp.mean(x, axis=-1, keepdims=True)
        xc = x - mu
        var = jnp.mean(xc * xc, axis=-1, keepdims=True)
        y = xc * lax.rsqrt(var + NORM_EPS) * mw_ref[:, sl]
        parts.append((y * _sigmoid(op_ref[:, sl])).astype(BF16))
    return jnp.concatenate(parts, axis=1)


def _out_proj_kernel(*refs, route, head_out):
    refs = list(refs)
    if head_out:
        x1 = _mlstm_head_out_rows(*refs[:4])
        refs = refs[4:]
    else:
        x1 = refs.pop(0)[...]
    if route:
        (x2_ref, w1_ref, w2_ref, h_ref, g_ref, nw_ref, sh_ref, sc_ref, rwh_ref, rwl_ref, rb_ref,
         ho_ref, a_ref, r_ref) = refs
    else:
        x2_ref, w1_ref, w2_ref, h_ref, g_ref, nw_ref, sh_ref, sc_ref, ho_ref, a_ref = refs
    y = jnp.dot(x1, w1_ref[...], preferred_element_type=F32)
    y = y + jnp.dot(x2_ref[...], w2_ref[...], preferred_element_type=F32)
    hn = h_ref[...] + g_ref[...] * y
    ho_ref[...] = hn
    a = _normmod(hn, nw_ref[...], sh_ref[...], sc_ref[...])
    a_ref[...] = a.astype(a_ref.dtype)
    if route:
        r_ref[...] = _top2_route(a, rwh_ref, rwl_ref, rb_ref)


def _out_proj_norm(x1, x2, w3b, layer, h, gate, norm_w, shift, scale, tiles_per_seq, router=None):
    t, k2 = x2.shape
    k1 = k2
    d = w3b.shape[2]
    seg = _seg_index(tiles_per_seq)
    row = lambda w: pl.BlockSpec((ROW_TILE, w), lambda i: (i, 0))
    modv = pl.BlockSpec((None, 1, d), lambda i: (seg(i), 0, 0))
    head_out = isinstance(x1, tuple)
    if head_out:
        h_f, h_b, z_rest, mnorm_w = x1
        assert h_f.shape == (t, k1)
        x1_specs = [row(k1), row(k1), row(k1), pl.BlockSpec((1, k1), lambda i: (0, 0))]
        x1_args = [h_f, h_b, z_rest, mnorm_w.reshape(1, k1)]
    else:
        assert x1.shape == (t, k1)
        x1_specs, x1_args = [row(k1)], [x1]
    in_specs = x1_specs + [
        row(k2),
        pl.BlockSpec((None, k1, d), lambda i: (layer, 0, 0), pipeline_mode=pl.Buffered(1)),
        pl.BlockSpec((None, k2, d), lambda i: (layer, 1, 0), pipeline_mode=pl.Buffered(1)),
        row(d), modv, pl.BlockSpec((1, d), lambda i: (0, 0)), modv, modv]
    args = x1_args + [x2, w3b, w3b, h, gate, norm_w.reshape(1, d), shift, scale]
    out_specs = [row(d), row(d)]
    out_shape = [jax.ShapeDtypeStruct((t, d), F32), jax.ShapeDtypeStruct((t, d), BF16)]
    if router is not None:
        router_w, router_b = router
        rw = jnp.pad(router_w.astype(F32), ((0, 0), (0, 128 - N_EXPERTS)))
        rw_hi = rw.astype(BF16)
        rw_lo = (rw - rw_hi.astype(F32)).astype(BF16)
        in_specs += [pl.BlockSpec((d, 128), lambda i: (0, 0)), pl.BlockSpec((d, 128), lambda i: (0, 0)),
                     pl.BlockSpec((1, 128), lambda i: (0, 0))]
        args += [rw_hi, rw_lo, jnp.pad(router_b, (0, 128 - N_EXPERTS)).reshape(1, 128)]
        out_specs.append(row(128))
        out_shape.append(jax.ShapeDtypeStruct((t, 128), F32))
    return pl.pallas_call(
        functools.partial(_out_proj_kernel, route=router is not None, head_out=head_out),
        grid=(t // ROW_TILE,),
        in_specs=in_specs,
        out_specs=out_specs,
        out_shape=out_shape,
        compiler_params=_params(1),
        name="out_proj_norm",
    )(*args)


def _ffn_up_kernel(x_ref, wg_ref, wu_ref, o_ref, wgb_ref, wub_ref):
    @pl.when(pl.program_id(1) == 0)
    def _():
        wgb_ref[...] = wg_ref[...].astype(BF16)
        wub_ref[...] = wu_ref[...].astype(BF16)

    x = x_ref[...]
    g = jnp.dot(x, wgb_ref[...], preferred_element_type=F32)
    u = jnp.dot(x, wub_ref[...], preferred_element_type=F32)
    o_ref[...] = (_silu(g) * u).astype(o_ref.dtype)


def _ffn_up(x, w_gate, w_up, layer, tf=512):
    t, k = x.shape
    f = w_gate.shape[2]
    tm = 3 * ROW_TILE if t % (3 * ROW_TILE) == 0 else ROW_TILE
    return pl.pallas_call(
        _ffn_up_kernel,
        grid=(f // tf, t // tm),
        in_specs=[pl.BlockSpec((tm, k), lambda j, i: (i, 0)),
                  pl.BlockSpec((None, k, tf), lambda j, i: (layer, 0, j)),
                  pl.BlockSpec((None, k, tf), lambda j, i: (layer, 0, j))],
        out_specs=pl.BlockSpec((tm, tf), lambda j, i: (i, j)),
        out_shape=jax.ShapeDtypeStruct((t, f), BF16),
        scratch_shapes=[pltpu.VMEM((k, tf), BF16), pltpu.VMEM((k, tf), BF16)],
        compiler_params=_params(2),
        name="ffn_up",
    )(x, w_gate, w_up)


def _mm_resid_kernel(x_ref, w_ref, h_ref, g_ref, o_ref, wb_ref):
    @pl.when(pl.program_id(1) == 0)
    def _():
        wb_ref[...] = w_ref[...].astype(BF16)

    y = jnp.dot(x_ref[...], wb_ref[...], preferred_element_type=F32)
    o_ref[...] = h_ref[...] + g_ref[...] * y


def _ffn_down_residual(x, w3, layer, h, gate, tiles_per_seq, tn=512):
    t, k = x.shape
    n = w3.shape[2]
    seg = _seg_index(tiles_per_seq)
    return pl.pallas_call(
        _mm_resid_kernel,
        grid=(n // tn, t // ROW_TILE),
        in_specs=[pl.BlockSpec((ROW_TILE, k), lambda j, i: (i, 0)),
                  pl.BlockSpec((None, k, tn), lambda j, i: (layer, 0, j)),
                  pl.BlockSpec((ROW_TILE, tn), lambda j, i: (i, j)),
                  pl.BlockSpec((None, 1, tn), lambda j, i: (seg(i), 0, j))],
        out_specs=pl.BlockSpec((ROW_TILE, tn), lambda j, i: (i, j)),
        out_shape=jax.ShapeDtypeStruct((t, n), F32),
        scratch_shapes=[pltpu.VMEM((k, tn), BF16)],
        compiler_params=_params(2),
        name="ffn_down_residual",
    )(x, w3, h, gate)


def _rope(x, cs):
    x = x.astype(F32)
    return x * cs[:, :HEAD_DIM] + pltpu.roll(x, HEAD_DIM // 2, axis=1) * cs[:, HEAD_DIM:]


def _attend(q_all, k_parts, v_parts, sink_ref, o_ref, mask_fn):
    group = Q_HEADS // KV_HEADS
    scale = HEAD_DIM ** -0.5
    for g in range(KV_HEADS):
        qg = jnp.concatenate([q_all[:, (g * group + r) * HEAD_DIM:(g * group + r + 1) * HEAD_DIM]
                              for r in range(group)], axis=0).astype(BF16)
        kg = jnp.concatenate([kp[:, g * HEAD_DIM:(g + 1) * HEAD_DIM] for kp in k_parts], axis=0).astype(BF16)
        vg = jnp.concatenate([vp[:, g * HEAD_DIM:(g + 1) * HEAD_DIM] for vp in v_parts], axis=0).astype(BF16)
        s = lax.dot_general(qg, kg, (((1,), (1,)), ((), ())), preferred_element_type=F32) * scale
        if mask_fn is not None:
            s = jnp.where(mask_fn(s.shape), s, NEG)
        sink = sink_ref[g]
        m = jnp.maximum(jnp.max(s, axis=-1, keepdims=True), sink)
        p = jnp.exp(s - m)
        denom = jnp.sum(p, axis=-1, keepdims=True) + jnp.exp(sink - m)
        o = jnp.dot(p.astype(BF16), vg, preferred_element_type=F32) / denom
        for r in range(group):
            hq = g * group + r
            o_ref[:, hq * HEAD_DIM:(hq + 1) * HEAD_DIM] = o[r * ATT_BLOCK:(r + 1) * ATT_BLOCK].astype(o_ref.dtype)


def _attn_kernel(q_ref, kp_ref, kc_ref, kn_ref, kx_ref, vp_ref, vc_ref, vn_ref, vx_ref,
                 csp_ref, csc_ref, csn_ref, sink_ref, o_ref, *, n_blocks, ctx_blocks):
    step = pl.program_id(1)

    @pl.when(step < ctx_blocks)
    def _():
        _attend(q_ref[...], [kx_ref[...]], [vx_ref[...]], sink_ref, o_ref, None)

    @pl.when(step >= ctx_blocks)
    def _():
        _win_attn_body(q_ref, kp_ref, kc_ref, kn_ref, kx_ref, vp_ref, vc_ref, vn_ref, vx_ref,
                       csp_ref, csc_ref, csn_ref, sink_ref, o_ref, step - ctx_blocks, n_blocks)


def _win_attn_body(q_ref, kp_ref, kc_ref, kn_ref, kx_ref, vp_ref, vc_ref, vn_ref, vx_ref,
                   csp_ref, csc_ref, csn_ref, sink_ref, o_ref, n, n_blocks):
    csc = csc_ref[...]
    q_all = jnp.concatenate([_rope(q_ref[:, h * HEAD_DIM:(h + 1) * HEAD_DIM], csc) for h in range(Q_HEADS)], axis=1)

    def rope_kv(k_ref, cs):
        return jnp.concatenate([_rope(k_ref[:, g * HEAD_DIM:(g + 1) * HEAD_DIM], cs) for g in range(KV_HEADS)], axis=1)

    k_parts = [rope_kv(kp_ref, csp_ref[...]), rope_kv(kc_ref, csc), rope_kv(kn_ref, csn_ref[...]), kx_ref[...]]
    v_parts = [vp_ref[...], vc_ref[...], vn_ref[...], vx_ref[...]]

    def mask_fn(shape):
        row = lax.broadcasted_iota(jnp.int32, shape, 0) & (ATT_BLOCK - 1)
        col = lax.broadcasted_iota(jnp.int32, shape, 1)
        band = (col >= row) & (col <= row + 2 * ATT_BLOCK)
        ok_prev = (col >= ATT_BLOCK) | (n > 0)
        ok_next = (col < 2 * ATT_BLOCK) | (n < n_blocks - 1)
        return (band & ok_prev & ok_next) | (col >= 3 * ATT_BLOCK)

    _attend(q_all, k_parts, v_parts, sink_ref, o_ref, mask_fn)


def _sink_rows(sink):
    group = Q_HEADS // KV_HEADS
    return jnp.repeat(sink.astype(F32).reshape(KV_HEADS, group), ATT_BLOCK, axis=1).reshape(
        KV_HEADS, group * ATT_BLOCK, 1)


def _window_attention(z, cs, sink, batch, ctx_len, seq):
    t = z.shape[0]
    nb = seq // ATT_BLOCK
    cb = ctx_len // ATT_BLOCK
    base = batch * cb
    qw = Q_HEADS * HEAD_DIM
    kw = KV_HEADS * HEAD_DIM
    kcol = qw // kw
    vcol = kcol + 1
    sink_rows = _sink_rows(sink)

    def lat(s):
        return jnp.clip(s - cb, 0, nb - 1)

    def q_block(b, s):
        return jnp.where(s < cb, b * cb + s, base + b * nb + lat(s))

    def kv(shift, col):
        return pl.BlockSpec((ATT_BLOCK, kw), lambda b, s: (base + b * nb + lat(s + shift), col))

    def rot(shift):
        return pl.BlockSpec((ATT_BLOCK, 2 * HEAD_DIM), lambda b, s: (lat(s + shift), 0))

    return pl.pallas_call(
        functools.partial(_attn_kernel, n_blocks=nb, ctx_blocks=cb),
        grid=(batch, cb + nb),
        in_specs=[pl.BlockSpec((ATT_BLOCK, qw), lambda b, s: (q_block(b, s), 0)),
                  kv(-1, kcol), kv(0, kcol), kv(1, kcol),
                  pl.BlockSpec((ctx_len, kw), lambda b, s: (b, kcol)),
                  kv(-1, vcol), kv(0, vcol), kv(1, vcol),
                  pl.BlockSpec((ctx_len, kw), lambda b, s: (b, vcol)),
                  rot(-1), rot(0), rot(1),
                  pl.BlockSpec(sink_rows.shape, lambda b, s: (0, 0, 0))],
        out_specs=pl.BlockSpec((ATT_BLOCK, qw), lambda b, s: (q_block(b, s), 0)),
        out_shape=jax.ShapeDtypeStruct((t, qw), BF16),
        compiler_params=_params(2),
        name="window_attention",
    )(z, z, z, z, z, z, z, z, z, cs, cs, cs, sink_rows)


def _seq_edges(t, batch, ctx_tiles, lat_tiles):
    n_ctx = batch * ctx_tiles
    u = t - n_ctx
    is_ctx = t < n_ctx
    first = jnp.where(is_ctx, lax.rem(t, ctx_tiles) == 0, lax.rem(u, lat_tiles) == 0)
    last = jnp.where(is_ctx, lax.rem(t, ctx_tiles) == ctx_tiles - 1, lax.rem(u, lat_tiles) == lat_tiles - 1)
    return first, last


def _fill_ext(ext_ref, cur, prev, nxt, first, last):
    ext_ref[HALO:HALO + CONV_TILE, :] = cur
    ext_ref[0:HALO, :] = jnp.where(first, 0.0, prev)
    ext_ref[HALO + CONV_TILE:, :] = jnp.where(last, 0.0, nxt)


def _dwconv_rows(ext_ref, w_ref, taps, r0, n_rows):
    pad = taps // 2
    acc = None
    for k in range(taps):
        start = HALO + r0 + k - pad
        term = w_ref[k:k + 1, :] * ext_ref[start:start + n_rows, :]
        acc = term if acc is None else acc + term
    return acc


CONV_ROWS = 32


def _short_conv_kernel(bg_ref, cg_ref, xb_ref, cgp_ref, xbp_ref, cgn_ref, xbn_ref, w_ref, o_ref, ext_ref,
                       *, batch, ctx_tiles, lat_tiles):
    first, last = _seq_edges(pl.program_id(0), batch, ctx_tiles, lat_tiles)
    def prod(a_ref, b_ref):
        return a_ref[...].astype(F32) * b_ref[...].astype(F32)

    _fill_ext(ext_ref, prod(cg_ref, xb_ref), prod(cgp_ref, xbp_ref), prod(cgn_ref, xbn_ref), first, last)
    for r0 in range(0, CONV_TILE, CONV_ROWS):
        conv = _dwconv_rows(ext_ref, w_ref, CONV_SHORT, r0, CONV_ROWS)
        o_ref[r0:r0 + CONV_ROWS, :] = (bg_ref[r0:r0 + CONV_ROWS, :].astype(F32) * conv).astype(o_ref.dtype)


def _halo_specs(width, col, halo_per_tile, n_halo_blocks):
    prev = pl.BlockSpec((HALO, width), lambda t, c: (jnp.maximum(t * halo_per_tile - 1, 0), col(c)))
    nxt = pl.BlockSpec((HALO, width), lambda t, c: (jnp.minimum((t + 1) * halo_per_tile, n_halo_blocks - 1), col(c)))
    return prev, nxt


def _short_conv(z, conv_w3, layer, batch, ctx_len, seq):
    t = z.shape[0]
    cw = 512
    n_c = 1024 // cw
    bg0, cg0, xb0 = 1536 // cw, 2560 // cw, 3584 // cw
    hpt = CONV_TILE // HALO
    nh = t // HALO
    cgp, cgn = _halo_specs(cw, lambda c: cg0 + c, hpt, nh)
    xbp, xbn = _halo_specs(cw, lambda c: xb0 + c, hpt, nh)
    kern = functools.partial(_short_conv_kernel, batch=batch, ctx_tiles=ctx_len // CONV_TILE,
                             lat_tiles=seq // CONV_TILE)
    return pl.pallas_call(
        kern,
        grid=(t // CONV_TILE, n_c),
        in_specs=[pl.BlockSpec((CONV_TILE, cw), lambda t_, c: (t_, bg0 + c)),
                  pl.BlockSpec((CONV_TILE, cw), lambda t_, c: (t_, cg0 + c)),
                  pl.BlockSpec((CONV_TILE, cw), lambda t_, c: (t_, xb0 + c)),
                  cgp, xbp, cgn, xbn,
                  pl.BlockSpec((None, CONV_SHORT, cw), lambda t_, c: (layer, 0, c))],
        out_specs=pl.BlockSpec((CONV_TILE, cw), lambda t_, c: (t_, c)),
        out_shape=jax.ShapeDtypeStruct((t, 1024), BF16),
        scratch_shapes=[pltpu.VMEM((CONV_TILE + 2 * HALO, cw), F32)],
        compiler_params=_params(2),
        name="short_conv",
    )(z, z, z, z, z, z, z, conv_w3)


SUBLANES = 8
SHIFT_ROWS = CONV_TILE + 2 * HALO - SUBLANES


def _dwconv_rows_aligned(ext_ref, sh_ref, w8_ref, taps, r0, n_rows):
    pad = taps // 2
    width = ext_ref.shape[1]
    acc = None
    for k in range(taps):
        start = HALO + r0 + k - pad
        b = start % SUBLANES
        a8 = start - b
        src = ext_ref[a8:a8 + n_rows, :] if b == 0 else sh_ref[b - 1, a8:a8 + n_rows, :]
        term = (src.reshape(n_rows // SUBLANES, SUBLANES, width) * w8_ref[k][None]).reshape(n_rows, width)
        acc = term if acc is None else acc + term
    return acc


def _conformer_kernel(a_ref, g_ref, ap_ref, gp_ref, an_ref, gn_ref, w8_ref, b_ref, lw_ref, lb_ref, o_ref,
                      ext_ref, sh_ref, *, batch, ctx_tiles, lat_tiles):
    first, last = _seq_edges(pl.program_id(0), batch, ctx_tiles, lat_tiles)
    _fill_ext(ext_ref, a_ref[...] * _sigmoid(g_ref[...]), ap_ref[...] * _sigmoid(gp_ref[...]),
              an_ref[...] * _sigmoid(gn_ref[...]), first, last)
    for b in range(1, SUBLANES):
        sh_ref[b - 1] = ext_ref[b:b + SHIFT_ROWS, :]
    for r0 in range(0, CONV_TILE, CONV_ROWS):
        u = _dwconv_rows_aligned(ext_ref, sh_ref, w8_ref, CONV_LONG, r0, CONV_ROWS) + b_ref[...]
        mu = jnp.mean(u, axis=-1, keepdims=True)
        uc = u - mu
        var = jnp.mean(uc * uc, axis=-1, keepdims=True)
        y = uc * lax.rsqrt(var + NORM_EPS) * lw_ref[...] + lb_ref[...]
        o_ref[r0:r0 + CONV_ROWS, :] = _silu(y).astype(o_ref.dtype)


def _conformer_conv(z, a_col, g_col, dw_w3, dw_b, ln_w, ln_b, layer, batch, ctx_len, seq):
    t = z.shape[0]
    cw = 1024
    hpt = CONV_TILE // HALO
    nh = t // HALO
    ap, an = _halo_specs(cw, lambda c: a_col, hpt, nh)
    gp, gn = _halo_specs(cw, lambda c: g_col, hpt, nh)
    kern = functools.partial(_conformer_kernel, batch=batch, ctx_tiles=ctx_len // CONV_TILE,
                             lat_tiles=seq // CONV_TILE)
    vec = pl.BlockSpec((None, 1, cw), lambda t_, c: (layer, 0, 0))
    n_layers = dw_b.shape[0]
    return pl.pallas_call(
        kern,
        grid=(t // CONV_TILE, 1),
        in_specs=[pl.BlockSpec((CONV_TILE, cw), lambda t_, c: (t_, a_col)),
                  pl.BlockSpec((CONV_TILE, cw), lambda t_, c: (t_, g_col)),
                  ap, gp, an, gn,
                  pl.BlockSpec((None, CONV_LONG, SUBLANES, cw), lambda t_, c: (layer, 0, 0, 0)),
                  vec, vec, vec],
        out_specs=pl.BlockSpec((CONV_TILE, cw), lambda t_, c: (t_, 0)),
        out_shape=jax.ShapeDtypeStruct((t, cw), BF16),
        scratch_shapes=[pltpu.VMEM((CONV_TILE + 2 * HALO, cw), F32),
                        pltpu.VMEM((SUBLANES - 1, SHIFT_ROWS, cw), F32)],
        compiler_params=_params(2),
        name="conformer_conv",
    )(z, z, z, z, z, z, jnp.broadcast_to(dw_w3[:, :, None, :], (n_layers, CONV_LONG, SUBLANES, cw)),
      dw_b.reshape(n_layers, 1, cw), ln_w.reshape(n_layers, 1, cw), ln_b.reshape(n_layers, 1, cw))


def _log_sigmoid(x):
    return jnp.minimum(x, 0.0) - jnp.log(1.0 + jnp.exp(-jnp.abs(x)))


def _mlstm_kernel(qf_ref, kf_ref, vf_ref, gcf_ref, grf_ref, qb_ref, kb_ref, vb_ref, gcb_ref, grb_ref,
                  bc_ref, br_ref, of_ref, ob_ref, *state_refs):
    n_chain = 2 * M_HEADS
    cta_ref, m_ref = state_refs[:n_chain], state_refs[n_chain:]

    @pl.when(pl.program_id(1) == 0)
    def _():
        for ref in state_refs:
            ref[...] = jnp.zeros_like(ref)

    ln = M_CHUNK
    row = lax.broadcasted_iota(jnp.int32, (ln, ln), 0)
    col = lax.broadcasted_iota(jnp.int32, (ln, ln), 1)
    lower = col <= row
    upper = col >= row
    _mlstm_direction(qf_ref, kf_ref, vf_ref, gcf_ref[...] + bc_ref[0], grf_ref[...] + br_ref[0], lower, upper,
                     of_ref, cta_ref, m_ref, 0)
    _mlstm_direction(qb_ref, kb_ref, vb_ref, gcb_ref[...] + bc_ref[1], grb_ref[...] + br_ref[1], upper, lower,
                     ob_ref, cta_ref, m_ref, M_HEADS)


def _mlstm_direction(q_ref, k_ref, v_ref, gc, gr, mask, mask_t, o_ref, cta_ref, m_ref, state0):
    ln = M_CHUNK
    one_col = lax.broadcasted_iota(jnp.int32, (ln, 128), 1) == 0
    ones_blk = jnp.where(one_col, 1.0, 0.0).astype(BF16)
    for h in range(M_HEADS):
        st = state0 + h
        ig_c = gc[:, h:h + 1]
        lf_c = _log_sigmoid(gc[:, M_HEADS + h:M_HEADS + h + 1])
        ig_r = gr[h:h + 1, :]
        lf_r = _log_sigmoid(gr[M_HEADS + h:M_HEADS + h + 1, :])
        qb = (q_ref[:, h * M_QK:(h + 1) * M_QK] * (M_QK ** -0.5)).astype(BF16)
        k = k_ref[:, h * M_QK:(h + 1) * M_QK]
        v = v_ref[:, h * M_V:(h + 1) * M_V]
        cum_c = jnp.sum(jnp.where(mask, lf_r, 0.0), axis=1, keepdims=True)
        cum_r = jnp.sum(jnp.where(mask_t, lf_c, 0.0), axis=0, keepdims=True)
        total = jnp.sum(lf_r, axis=1, keepdims=True)
        m_s = m_ref[st][...]
        g = jnp.where(mask, ig_r - cum_r, NEG)
        mm = jnp.maximum(m_s, jnp.max(g, axis=1, keepdims=True))
        w_inter = jnp.exp(m_s - mm)
        s = lax.dot_general(qb, k.astype(BF16), (((1,), (1,)), ((), ())), preferred_element_type=F32) * jnp.exp(g - mm)
        cta = cta_ref[st][...]
        va = jnp.concatenate([v.astype(BF16), ones_blk], axis=1)
        inter = jnp.dot(qb, cta.astype(BF16), preferred_element_type=F32)
        intra = jnp.dot(s.astype(BF16), va, preferred_element_type=F32)
        num = w_inter * inter[:, :M_V] + intra[:, :M_V]
        den = w_inter * inter[:, M_V:M_V + 1] + intra[:, M_V:M_V + 1]
        o_ref[:, h * M_V:(h + 1) * M_V] = num / jnp.maximum(jnp.abs(den), jnp.exp(-(cum_c + mm)))
        gl = total - cum_c + ig_c
        m_new = jnp.maximum(total + m_s, jnp.max(gl, axis=0, keepdims=True))
        wg = jnp.exp(gl - m_new)
        decay = jnp.exp(total + m_s - m_new)
        wva = jnp.concatenate([(wg * v).astype(BF16), jnp.where(one_col, wg, 0.0).astype(BF16)], axis=1)
        cta_ref[st][...] = decay * cta + jnp.dot(k.astype(BF16).T, wva, preferred_element_type=F32)
        m_ref[st][...] = m_new


def _mlstm(zq, gates, gate_b, batch, ctx_len, seq):
    t = zq.shape[0]
    cb = ctx_len // M_CHUNK
    nc = seq // M_CHUNK
    base = batch * cb
    steps = cb + nc
    half = 2 * M_HEADS
    g_dir = gates.reshape(t, 2, half).transpose(1, 0, 2)
    g_dir_t = g_dir.transpose(0, 2, 1)
    b_dir = gate_b.astype(F32).reshape(2, 1, half)
    b_dir_t = b_dir.transpose(0, 2, 1)

    def rb_f(b, i):
        return jnp.where(i < cb, b * cb + i, base + b * nc + (i - cb))

    def rb_b(b, i):
        return jnp.where(i < cb, b * cb + (cb - 1 - i), base + b * nc + (nc - 1 - (i - cb)))

    qk_w = M_HEADS * M_QK
    v_w = M_HEADS * M_V

    def chunk_specs(d, rb):
        return [pl.BlockSpec((M_CHUNK, qk_w), lambda b, i: (rb(b, i), 0)),
                pl.BlockSpec((M_CHUNK, qk_w), lambda b, i: (rb(b, i), 1)),
                pl.BlockSpec((M_CHUNK, v_w), lambda b, i: (rb(b, i), 1)),
                pl.BlockSpec((None, M_CHUNK, half), lambda b, i: (d, rb(b, i), 0)),
                pl.BlockSpec((None, half, M_CHUNK), lambda b, i: (d, 0, rb(b, i)))]

    return pl.pallas_call(
        _mlstm_kernel,
        grid=(batch, steps),
        in_specs=chunk_specs(0, rb_f) + chunk_specs(1, rb_b) + [
            pl.BlockSpec((2, 1, half), lambda b, i: (0, 0, 0)),
            pl.BlockSpec((2, half, 1), lambda b, i: (0, 0, 0))],
        out_specs=[pl.BlockSpec((M_CHUNK, v_w), lambda b, i: (rb_f(b, i), 0)),
                   pl.BlockSpec((M_CHUNK, v_w), lambda b, i: (rb_b(b, i), 0))],
        out_shape=[jax.ShapeDtypeStruct((t, v_w), F32), jax.ShapeDtypeStruct((t, v_w), F32)],
        scratch_shapes=([pltpu.VMEM((M_QK, M_V + 128), F32)] * (2 * M_HEADS)
                        + [pltpu.VMEM((1, 1), F32)] * (2 * M_HEADS)),
        compiler_params=_params(2),
        name="mlstm",
    )(zq, zq, zq, g_dir, g_dir_t, zq, zq, zq, g_dir, g_dir_t, b_dir, b_dir_t)


def _fresh_weights(be_ref, rb):
    return (rb == 0) | (be_ref[rb] != be_ref[jnp.maximum(rb - 1, 0)])


HALF_TILE = ROW_TILE // 2


def _block_state(nv_ref, rb):
    valid = rb < nv_ref[0]
    half = nv_ref[1 + rb] != 0
    return valid & jnp.logical_not(half), valid & half, jnp.logical_not(valid)


def _moe_up_kernel(be_ref, nv_ref, x_ref, wg_ref, wu_ref, o_ref, wgb_ref, wub_ref):
    rb = pl.program_id(1)
    full, half, unused = _block_state(nv_ref, rb)

    @pl.when(jnp.logical_not(unused) & _fresh_weights(be_ref, rb))
    def _():
        wgb_ref[...] = wg_ref[...].astype(BF16)
        wub_ref[...] = wu_ref[...].astype(BF16)

    def swiglu(x):
        g = jnp.dot(x, wgb_ref[...], preferred_element_type=F32)
        u = jnp.dot(x, wub_ref[...], preferred_element_type=F32)
        return (_silu(g) * u).astype(o_ref.dtype)

    @pl.when(full)
    def _():
        o_ref[...] = swiglu(x_ref[...])

    @pl.when(half)
    def _():
        o_ref[:HALF_TILE, :] = swiglu(x_ref[:HALF_TILE, :])
        o_ref[HALF_TILE:, :] = jnp.zeros((ROW_TILE - HALF_TILE, o_ref.shape[1]), o_ref.dtype)

    @pl.when(unused)
    def _():
        o_ref[...] = jnp.zeros_like(o_ref)


def _moe_down_kernel(be_ref, nv_ref, x_ref, w_ref, o_ref, wb_ref):
    rb = pl.program_id(1)
    full, half, unused = _block_state(nv_ref, rb)

    @pl.when(jnp.logical_not(unused) & _fresh_weights(be_ref, rb))
    def _():
        wb_ref[...] = w_ref[...].astype(BF16)

    @pl.when(full)
    def _():
        o_ref[...] = jnp.dot(x_ref[...], wb_ref[...], preferred_element_type=F32).astype(o_ref.dtype)

    @pl.when(half)
    def _():
        o_ref[:HALF_TILE, :] = jnp.dot(x_ref[:HALF_TILE, :], wb_ref[...],
                                       preferred_element_type=F32).astype(o_ref.dtype)
        o_ref[HALF_TILE:, :] = jnp.zeros((ROW_TILE - HALF_TILE, o_ref.shape[1]), o_ref.dtype)

    @pl.when(unused)
    def _():
        o_ref[...] = jnp.zeros_like(o_ref)


def _moe_experts(x_sorted, block_expert, n_valid, w_gate, w_up, w_down, layer, tf=512, tn=512):
    p, d = x_sorted.shape
    f = w_gate.shape[3]
    nblk = p // ROW_TILE

    def row(rb, nv):
        return jnp.minimum(rb, nv[0] - 1)

    g = pl.pallas_call(
        _moe_up_kernel,
        grid_spec=pltpu.PrefetchScalarGridSpec(
            num_scalar_prefetch=2,
            grid=(f // tf, nblk),
            in_specs=[pl.BlockSpec((ROW_TILE, d), lambda j, rb, be, nv: (row(rb, nv), 0)),
                      pl.BlockSpec((None, None, d, tf), lambda j, rb, be, nv: (layer, be[rb], 0, j)),
                      pl.BlockSpec((None, None, d, tf), lambda j, rb, be, nv: (layer, be[rb], 0, j))],
            out_specs=pl.BlockSpec((ROW_TILE, tf), lambda j, rb, be, nv: (rb, j)),
            scratch_shapes=[pltpu.VMEM((d, tf), BF16), pltpu.VMEM((d, tf), BF16)]),
        out_shape=jax.ShapeDtypeStruct((p, f), BF16),
        compiler_params=_params(2),
        name="moe_up",
    )(block_expert, n_valid, x_sorted, w_gate, w_up)
    return pl.pallas_call(
        _moe_down_kernel,
        grid_spec=pltpu.PrefetchScalarGridSpec(
            num_scalar_prefetch=2,
            grid=(d // tn, nblk),
            in_specs=[pl.BlockSpec((ROW_TILE, f), lambda j, rb, be, nv: (row(rb, nv), 0)),
                      pl.BlockSpec((None, None, f, tn), lambda j, rb, be, nv: (layer, be[rb], 0, j))],
            out_specs=pl.BlockSpec((ROW_TILE, tn), lambda j, rb, be, nv: (rb, j)),
            scratch_shapes=[pltpu.VMEM((f, tn), BF16)]),
        out_shape=jax.ShapeDtypeStruct((p, d), BF16),
        compiler_params=_params(2),
        name="moe_down",
    )(block_expert, n_valid, g, w_down)


def _routing_plan(route):
    t = route.shape[0]
    tm = ROW_TILE
    n_assign = TOP_K * t
    nblk = (n_assign + N_EXPERTS * (tm - 1)) // tm
    e = route[:, :TOP_K].astype(jnp.int32).reshape(-1)
    onehot = (e[:, None] == jnp.arange(N_EXPERTS, dtype=jnp.int32)[None, :]).astype(jnp.int32)
    csum = jnp.cumsum(onehot, axis=0)
    counts = csum[-1]
    padded = ((counts + tm - 1) // tm) * tm
    ends = jnp.cumsum(padded)
    starts = ends - padded
    dest = jnp.sum((csum - onehot + starts[None, :]) * onehot, axis=1)
    src_tok = (jnp.arange(nblk * tm, dtype=jnp.int32) % t).at[dest].set(
        jnp.arange(n_assign, dtype=jnp.int32) // TOP_K, unique_indices=True, mode="promise_in_bounds")
    n_valid = (ends[-1] // tm).astype(jnp.int32)
    blk_start = jnp.arange(nblk, dtype=jnp.int32) * tm
    be = jnp.sum((ends[None, :] <= blk_start[:, None]).astype(jnp.int32), axis=1)
    be = jnp.minimum(be, N_EXPERTS - 1).astype(jnp.int32)
    group_rows_end = jnp.sum((be[:, None] == jnp.arange(N_EXPERTS, dtype=jnp.int32)[None, :]).astype(jnp.int32)
                             * (starts + counts)[None, :], axis=1)
    half = (group_rows_end - blk_start <= HALF_TILE).astype(jnp.int32)
    block_info = jnp.concatenate([n_valid.reshape(1), half])
    return src_tok, be, block_info, dest.reshape(t, TOP_K)


def _combine_kernel(h_ref, y0_ref, y1_ref, r_ref, g_ref, nw_ref, sh_ref, sc_ref, *out_refs):
    r = r_ref[...]
    moe = r[:, TOP_K:TOP_K + 1] * y0_ref[...].astype(F32) + r[:, TOP_K + 1:TOP_K + 2] * y1_ref[...].astype(F32)
    hn = h_ref[...] + g_ref[...] * moe
    if len(out_refs) == 2:
        out_refs[0][...] = hn
    out_refs[-1][...] = _normmod(hn, nw_ref[...], sh_ref[...], sc_ref[...]).astype(out_refs[-1].dtype)


def _combine_norm(h, y_pair, route, gate, norm_w, shift, scale, tiles_per_seq, row_offset_tiles, emit_h, a_dtype):
    d = h.shape[1]
    n_rows = y_pair.shape[0] // TOP_K
    n_tiles = n_rows // ROW_TILE
    off = row_offset_tiles
    seg = _seg_index(tiles_per_seq)
    shifted = lambda w: pl.BlockSpec((ROW_TILE, w), lambda i: (i + off, 0))
    tile = pl.BlockSpec((ROW_TILE, d), lambda i: (i, 0))
    tile1 = pl.BlockSpec((ROW_TILE, d), lambda i: (i + n_tiles, 0))
    modv = pl.BlockSpec((None, 1, d), lambda i: (seg(i + off), 0, 0))
    out_specs = [tile, tile] if emit_h else [tile]
    out_shape = [jax.ShapeDtypeStruct((n_rows, d), a_dtype)]
    if emit_h:
        out_shape.insert(0, jax.ShapeDtypeStruct((n_rows, d), F32))
    return pl.pallas_call(
        _combine_kernel,
        grid=(n_rows // ROW_TILE,),
        in_specs=[shifted(d), tile, tile1, shifted(128), modv, pl.BlockSpec((1, d), lambda i: (0, 0)), modv, modv],
        out_specs=out_specs,
        out_shape=out_shape,
        compiler_params=_params(1),
        name="moe_combine_norm",
    )(h, y_pair, y_pair, route, gate, norm_w.reshape(1, d), shift, scale)


def _rope_table(seq):
    rows = seq // GRID_W
    row = jnp.repeat(jnp.arange(rows, dtype=F32), GRID_W)
    col = jnp.tile(jnp.arange(GRID_W, dtype=F32), rows)
    n_freq = HEAD_DIM // 4
    inv_freq = ROPE_BASE ** (-jnp.arange(n_freq, dtype=F32) / n_freq)
    ang = jnp.concatenate([row[:, None] * inv_freq, col[:, None] * inv_freq], axis=-1)
    cos, sin = jnp.cos(ang), jnp.sin(ang)
    return jnp.concatenate([cos, cos, -sin, sin], axis=-1)


def kernel(x, c, ctx, c_ctx, ada_w, ada_b, norm1_w, norm2_w, ev_w_in, ev_sink, ev_conv_w, ev_w_out, ffn_w_gate, ffn_w_up, ffn_w_down, od_w_in, od_gate_b, od_mnorm_w, od_dw_w, od_dw_b, od_ln_w, od_ln_b, od_w_out, moe_router_w, moe_router_b, moe_w_gate, moe_w_up, moe_w_down, final_w):
    batch, seq, d = x.shape
    ctx_len = ctx.shape[1]
    depth = ada_w.shape[0]
    assert batch * ctx_len == ROW_TILE and seq % ROW_TILE == 0 and ctx_len % CONV_TILE == 0
    assert 1 + batch <= 8
    tps = seq // ROW_TILE
    n_ctx_rows = batch * ctx_len

    cs = _rope_table(seq)
    cvec = jnp.zeros((8, d), F32).at[0].set(c_ctx).at[1:1 + batch].set(c)
    mod_table = _ada_table(cvec, ada_w, ada_b)

    def mods_of(layer):
        mods = mod_table[layer, :1 + batch].reshape(1 + batch, 6, 1, d)
        return [mods[:, i] for i in range(6)]

    zero_mod = jnp.zeros((1 + batch, 1, d), F32)
    ctx_tiles = n_ctx_rows // ROW_TILE
    ev_w_out_b = ev_w_out.astype(BF16)
    od_w_out_b = od_w_out.astype(BF16)

    def take(rows, idx):
        return rows.at[idx].get(mode="promise_in_bounds")

    out = None
    mod = mods_of(0)
    h, a1 = _embed_norm(ctx.reshape(n_ctx_rows, d), x.reshape(batch * seq, d), norm1_w[0], mod[0], mod[1], tps)
    for layer in range(depth):
        j = layer // 2
        last = layer == depth - 1
        next_mod = None if last else mods_of(layer + 1)
        if layer % 2 == 0:
            z = _matmul(a1, ev_w_in, j, ev_w_in.shape[2], tn=1536, out_dtype=BF16)
            o_a = _window_attention(z, cs, ev_sink[j], batch, ctx_len, seq)
            o_b = _short_conv(z, ev_conv_w, j, batch, ctx_len, seq)
            h, a2 = _out_proj_norm(o_a, o_b, ev_w_out_b, j, h, mod[2], norm2_w[layer], mod[3], mod[4], tps)
            g = _ffn_up(a2, ffn_w_gate, ffn_w_up, j)
            h = _ffn_down_residual(g, ffn_w_down, j, h, mod[5], tps)
            if last:
                out = _norm_modulate(h, final_w, zero_mod, zero_mod, tps, out_dtype=F32,
                                     row_offset_tiles=ctx_tiles, n_rows=batch * seq)
            else:
                a1 = _norm_modulate(h, norm1_w[layer + 1], next_mod[0], next_mod[1], tps)
        else:
            qkv_w = M_HEADS * (2 * M_QK + M_V)
            n_gate = 4 * M_HEADS
            w_gate_cols = jnp.pad(od_w_in[j, :, qkv_w:qkv_w + n_gate], ((0, 0), (0, 128 - n_gate)))
            zq, gates = _matmul_with_side(a1, od_w_in, j, qkv_w, w_gate_cols, tn=1024)
            gates = gates[:, :n_gate]
            w_rest = od_w_in[j, :, qkv_w + n_gate:][None]
            z_rest = _matmul(a1, w_rest, 0, w_rest.shape[2], tn=1536)
            h_f, h_b = _mlstm(zq, gates, od_gate_b[j], batch, ctx_len, seq)
            u_out = _conformer_conv(z_rest, 1, 2, od_dw_w, od_dw_b, od_ln_w, od_ln_b, j, batch, ctx_len, seq)
            h, a2, route = _out_proj_norm((h_f, h_b, z_rest, od_mnorm_w[j]), u_out, od_w_out_b, j, h, mod[2],
                                          norm2_w[layer], mod[3], mod[4], tps,
                                          router=(moe_router_w[j], moe_router_b[j]))
            row0 = n_ctx_rows if last else 0
            src_tok, block_expert, n_valid, pos = _routing_plan(route[row0:])
            x_sorted = jnp.take(a2, src_tok + row0, axis=0, mode="clip")
            y = _moe_experts(x_sorted, block_expert, n_valid, moe_w_gate, moe_w_up, moe_w_down, j)
            y_pair = take(y, pos.T.reshape(-1))
            if last:
                out = _combine_norm(h, y_pair, route, mod[5], final_w, zero_mod, zero_mod, tps, ctx_tiles,
                                    emit_h=False, a_dtype=F32)[0]
            else:
                h, a1 = _combine_norm(h, y_pair, route, mod[5], norm1_w[layer + 1], next_mod[0], next_mod[1], tps,
                                      0, emit_h=True, a_dtype=BF16)
        mod = next_mod
    return out.reshape(batch, seq, d)
```
